```python
import math
import jax, jax.numpy as jnp
from jax import lax
import numpy as np

D_MODEL = 2048
BATCH = 2
SEQ = 4096
DEPTH = 4

N_EVEN = (DEPTH + 1) // 2
N_ODD = DEPTH // 2

S5_WIDTH = D_MODEL // 2
RWKV_WIDTH = D_MODEL - S5_WIDTH
S5_GROUP = 16
S5_GROUPS = S5_WIDTH // S5_GROUP
S5_STATE = 64
S5_DT_MIN = 0.001
S5_DT_MAX = 0.1
RWKV_HEAD = 64
RWKV_HEADS = RWKV_WIDTH // RWKV_HEAD
RWKV_W_RANK = 64
RWKV_A_RANK = 64
RWKV_G_RANK = 160
EVEN_IN = S5_WIDTH + 3 * RWKV_WIDTH + RWKV_W_RANK + RWKV_A_RANK + RWKV_G_RANK
SHIFT_COLS = EVEN_IN - S5_WIDTH
GN_EPS = 64e-5

LRU_WIDTH = D_MODEL
LRU_BLOCKS = 8
LRU_BLOCK = LRU_WIDTH // LRU_BLOCKS
CONV_WIDTH = 4
LRU_C = 8.0

D_FF = -(-8 * D_MODEL // (3 * 256)) * 256
NORM_EPS = 1e-6

kernel_name = "hybrid_s5_rwkv7_rglru_trunk"


def rms_norm(x, g):
    xf = x.astype(jnp.float32)
    y = xf * lax.rsqrt(jnp.mean(xf * xf, axis=-1, keepdims=True) + NORM_EPS)
    return (y * g.astype(jnp.float32)).astype(x.dtype)


def token_shift(z):
    return jnp.pad(z, ((0, 0), (1, 0), (0, 0)))[:, :-1]


def complex_linear_combine(e1, e2):
    a1r, a1i, b1r, b1i = e1
    a2r, a2i, b2r, b2i = e2
    ar = a1r * a2r - a1i * a2i
    ai = a1r * a2i + a1i * a2r
    br = a2r * b1r - a2i * b1i + b2r
    bi = a2r * b1i + a2i * b1r + b2i
    return ar, ai, br, bi


def real_linear_combine(e1, e2):
    a1, b1 = e1
    a2, b2 = e2
    return a1 * a2, a2 * b1 + b2


def s5_mixer(u, lam_re, lam_im, log_dt, b_re, b_im, c_re, c_im, d_skip, w_glu):
    f32 = jnp.float32
    bsz, t_len, _ = u.shape
    uf = u.astype(f32).reshape(bsz, t_len, S5_GROUPS, S5_GROUP)
    dt = jnp.exp(log_dt.astype(f32))[:, None]
    lr = lam_re.astype(f32)
    li = lam_im.astype(f32)
    mag = jnp.exp(lr * dt)
    abar_re = mag * jnp.cos(li * dt)
    abar_im = mag * jnp.sin(li * dt)
    den = lr * lr + li * li
    nr = abar_re - 1.0
    ni = abar_im
    gam_re = (nr * lr + ni * li) / den
    gam_im = (ni * lr - nr * li) / den
    br_ = b_re.astype(f32)
    bi_ = b_im.astype(f32)
    bb_re = gam_re[..., None] * br_ - gam_im[..., None] * bi_
    bb_im = gam_re[..., None] * bi_ + gam_im[..., None] * br_
    bu_re = jnp.einsum('btgc,gpc->tbgp', uf, bb_re)
    bu_im = jnp.einsum('btgc,gpc->tbgp', uf, bb_im)
    a_re = jnp.broadcast_to(abar_re, (t_len, 1, S5_GROUPS, S5_STATE))
    a_im = jnp.broadcast_to(abar_im, (t_len, 1, S5_GROUPS, S5_STATE))
    _, _, s_re, s_im = lax.associative_scan(
        complex_linear_combine, (a_re, a_im, bu_re, bu_im), axis=0)
    y = (jnp.einsum('tbgp,gcp->btgc', s_re, c_re.astype(f32))
         - jnp.einsum('tbgp,gcp->btgc', s_im, c_im.astype(f32)))
    y = y + d_skip.astype(f32).reshape(S5_GROUPS, S5_GROUP) * uf
    y = jax.nn.gelu(y.reshape(bsz, t_len, S5_WIDTH)).astype(u.dtype)
    return y * jax.nn.sigmoid(y @ w_glu)


def rwkv7_mixer(r, k, v, w_lr, a_lr, g_lr, w0, w2, a0, a2, g2, k_k, k_a, r_k,
                lnx_w, lnx_b):
    f32 = jnp.float32
    bsz, t_len, _ = r.shape
    r, k, v = r.astype(f32), k.astype(f32), v.astype(f32)
    w = -jax.nn.softplus(-(w0.astype(f32) + jnp.tanh(w_lr.astype(f32)) @ w2.astype(f32))) - 0.5
    decay = jnp.exp(-jnp.exp(w))
    a = jax.nn.sigmoid(a0.astype(f32) + a_lr.astype(f32) @ a2.astype(f32))
    g = jax.nn.sigmoid(g_lr.astype(f32)) @ g2.astype(f32)
    hs = (bsz, t_len, RWKV_HEADS, RWKV_HEAD)
    kk = (k * k_k.astype(f32)).reshape(hs)
    kk = kk * lax.rsqrt(jnp.maximum(jnp.sum(kk * kk, -1, keepdims=True), 1e-24))
    k = k * (1.0 + (a - 1.0) * k_a.astype(f32))
    rh, kh, vh = r.reshape(hs), k.reshape(hs), v.reshape(hs)
    ah = a.reshape(hs)
    wh = decay.reshape(hs)
    vec_a = -kk
    vec_b = kk * ah
    tf = lambda z: jnp.swapaxes(z, 0, 1)

    def step(state, inp):
        r_t, w_t, k_t, v_t, a_t, b_t = inp
        sa = jnp.einsum('bhvk,bhk->bhv', state, a_t)
        state = (state * w_t[:, :, None, :]
                 + sa[..., None] * b_t[:, :, None, :]
                 + v_t[..., None] * k_t[:, :, None, :])
        return state, jnp.einsum('bhvk,bhk->bhv', state, r_t)

    s0 = jnp.zeros((bsz, RWKV_HEADS, RWKV_HEAD, RWKV_HEAD), f32)
    _, y = lax.scan(step, s0, (tf(rh), tf(wh), tf(kh), tf(vh), tf(vec_a), tf(vec_b)))
    y = tf(y)
    mu = jnp.mean(y, -1, keepdims=True)
    var = jnp.mean(jnp.square(y - mu), -1, keepdims=True)
    y = (y - mu) * lax.rsqrt(var + GN_EPS)
    y = y * lnx_w.astype(f32).reshape(RWKV_HEADS, RWKV_HEAD) + lnx_b.astype(f32).reshape(RWKV_HEADS, RWKV_HEAD)
    bonus = jnp.sum(rh * kh * r_k.astype(f32), -1, keepdims=True) * vh
    y = (y + bonus).reshape(bsz, t_len, RWKV_WIDTH)
    return y * g


def even_mixer(h, w_in, shift_mu, s5_lam_re, s5_lam_im, s5_log_dt, s5_b_re, s5_b_im,
               s5_c_re, s5_c_im, s5_d, s5_w_glu, rw_w0, rw_w2, rw_a0, rw_a2, rw_g2,
               rw_k_k, rw_k_a, rw_r_k, rw_lnx_w, rw_lnx_b, w_out):
    p = h @ w_in
    u = p[..., :S5_WIDTH]
    z = p[..., S5_WIDTH:]
    z = z + (token_shift(z) - z) * shift_mu
    o = 0
    r = z[..., o:o + RWKV_WIDTH]; o += RWKV_WIDTH
    k = z[..., o:o + RWKV_WIDTH]; o += RWKV_WIDTH
    v = z[..., o:o + RWKV_WIDTH]; o += RWKV_WIDTH
    w_lr = z[..., o:o + RWKV_W_RANK]; o += RWKV_W_RANK
    a_lr = z[..., o:o + RWKV_A_RANK]; o += RWKV_A_RANK
    g_lr = z[..., o:o + RWKV_G_RANK]
    y_s5 = s5_mixer(u, s5_lam_re, s5_lam_im, s5_log_dt, s5_b_re, s5_b_im,
                    s5_c_re, s5_c_im, s5_d, s5_w_glu)
    y_rw = rwkv7_mixer(r, k, v, w_lr, a_lr, g_lr, rw_w0, rw_w2, rw_a0, rw_a2, rw_g2,
                       rw_k_k, rw_k_a, rw_r_k, rw_lnx_w, rw_lnx_b)
    y = jnp.concatenate([y_s5, y_rw.astype(h.dtype)], axis=-1)
    return y @ w_out


def odd_mixer(h, w_in, conv_w, conv_b, w_r, b_r, w_i, b_i, lam, w_out):
    f32 = jnp.float32
    bsz, t_len, _ = h.shape
    p = h @ w_in
    gate = jax.nn.gelu(p[..., :LRU_WIDTH])
    xb = p[..., LRU_WIDTH:]
    xc = lax.conv_general_dilated(
        xb, conv_w.astype(xb.dtype)[:, None, :], window_strides=(1,),
        padding=[(CONV_WIDTH - 1, 0)], dimension_numbers=('NWC', 'WIO', 'NWC'),
        feature_group_count=LRU_WIDTH) + conv_b
    xf = xc.astype(f32)
    xblk = xf.reshape(bsz, t_len, LRU_BLOCKS, LRU_BLOCK)
    gr = (jnp.einsum('btnc,ncd->btnd', xblk, w_r.astype(f32)).reshape(bsz, t_len, LRU_WIDTH)
          + b_r.astype(f32))
    gi = (jnp.einsum('btnc,ncd->btnd', xblk, w_i.astype(f32)).reshape(bsz, t_len, LRU_WIDTH)
          + b_i.astype(f32))
    log_a = -LRU_C * jax.nn.sigmoid(gr) * jax.nn.softplus(-lam.astype(f32))
    a = jnp.exp(log_a)
    mult = jnp.sqrt(-jnp.expm1(2.0 * log_a))
    bx = mult * jax.nn.sigmoid(gi) * xf
    _, hseq = lax.associative_scan(real_linear_combine, (a, bx), axis=1)
    return (hseq.astype(h.dtype) * gate) @ w_out


def swiglu(h, w_gate, w_up, w_down):
    return (jax.nn.silu(h @ w_gate) * (h @ w_up)) @ w_down


def setup_inputs(seed: int = 0) -> dict:
    key = jax.random.key(seed)
    ks = iter(jax.random.split(key, 48))
    nrm = lambda shape, s: jax.random.normal(next(ks), shape, jnp.float32) * s
    uni = lambda shape, lo, hi: jax.random.uniform(next(ks), shape, jnp.float32, lo, hi)
    E, O = N_EVEN, N_ODD
    G, P, C = S5_GROUPS, S5_STATE, S5_GROUP
    a8 = uni((O, LRU_WIDTH), 0.9, 0.999)
    a_base = a8 ** (1.0 / LRU_C)
    return {
        "x": nrm((BATCH, SEQ, D_MODEL), 1.0),
        "ev_w_in": nrm((E, D_MODEL, EVEN_IN), D_MODEL ** -0.5),
        "ev_shift_mu": uni((E, SHIFT_COLS), 0.0, 1.0),
        "s5_lam_re": -0.5 + nrm((E, G, P), 0.01),
        "s5_lam_im": math.pi * jnp.arange(P, dtype=jnp.float32) + nrm((E, G, P), 0.01),
        "s5_log_dt": uni((E, G), math.log(S5_DT_MIN), math.log(S5_DT_MAX)),
        "s5_b_re": nrm((E, G, P, C), (2.0 * C) ** -0.5),
        "s5_b_im": nrm((E, G, P, C), (2.0 * C) ** -0.5),
        "s5_c_re": nrm((E, G, C, P), (2.0 * P) ** -0.5),
        "s5_c_im": nrm((E, G, C, P), (2.0 * P) ** -0.5),
        "s5_d": nrm((E, S5_WIDTH), 1.0),
        "s5_w_glu": nrm((E, S5_WIDTH, S5_WIDTH), S5_WIDTH ** -0.5),
        "rw_w0": uni((E, RWKV_WIDTH), -5.0, 1.0),
        "rw_w2": nrm((E, RWKV_W_RANK, RWKV_WIDTH), 0.1),
        "rw_a0": nrm((E, RWKV_WIDTH), 0.1),
        "rw_a2": nrm((E, RWKV_A_RANK, RWKV_WIDTH), 0.5 * RWKV_A_RANK ** -0.5),
        "rw_g2": nrm((E, RWKV_G_RANK, RWKV_WIDTH), RWKV_G_RANK ** -0.5),
        "rw_k_k": 0.85 + nrm((E, RWKV_WIDTH), 0.02),
        "rw_k_a": 1.0 + nrm((E, RWKV_WIDTH), 0.02),
        "rw_r_k": nrm((E, RWKV_HEADS, RWKV_HEAD), 0.1),
        "rw_lnx_w": 1.0 + nrm((E, RWKV_WIDTH), 0.02),
        "rw_lnx_b": nrm((E, RWKV_WIDTH), 0.01),
        "ev_w_out": nrm((E, D_MODEL, D_MODEL), D_MODEL ** -0.5),
        "od_w_in": nrm((O, D_MODEL, 2 * LRU_WIDTH), D_MODEL ** -0.5),
        "od_conv_w": nrm((O, CONV_WIDTH, LRU_WIDTH), CONV_WIDTH ** -0.5),
        "od_conv_b": nrm((O, LRU_WIDTH), 0.01),
        "lru_w_r": nrm((O, LRU_BLOCKS, LRU_BLOCK, LRU_BLOCK), LRU_BLOCK ** -0.5),
        "lru_b_r": nrm((O, LRU_WIDTH), 0.01),
        "lru_w_i": nrm((O, LRU_BLOCKS, LRU_BLOCK, LRU_BLOCK), LRU_BLOCK ** -0.5),
        "lru_b_i": nrm((O, LRU_WIDTH), 0.01),
        "lru_lam": jnp.log(a_base) - jnp.log1p(-a_base),
        "od_w_out": nrm((O, LRU_WIDTH, D_MODEL), LRU_WIDTH ** -0.5),
        "ffn_w_gate": nrm((DEPTH, D_MODEL, D_FF), D_MODEL ** -0.5),
        "ffn_w_up": nrm((DEPTH, D_MODEL, D_FF), D_MODEL ** -0.5),
        "ffn_w_down": nrm((DEPTH, D_FF, D_MODEL), D_FF ** -0.5),
        "norm_mix_pre": 1.0 + nrm((DEPTH, D_MODEL), 0.02),
        "norm_mix_post": 1.0 + nrm((DEPTH, D_MODEL), 0.02),
        "norm_ffn_pre": 1.0 + nrm((DEPTH, D_MODEL), 0.02),
        "norm_ffn_post": 1.0 + nrm((DEPTH, D_MODEL), 0.02),
    }


def reference(x, ev_w_in, ev_shift_mu, s5_lam_re, s5_lam_im, s5_log_dt, s5_b_re, s5_b_im,
              s5_c_re, s5_c_im, s5_d, s5_w_glu, rw_w0, rw_w2, rw_a0, rw_a2, rw_g2,
              rw_k_k, rw_k_a, rw_r_k, rw_lnx_w, rw_lnx_b, ev_w_out,
              od_w_in, od_conv_w, od_conv_b, lru_w_r, lru_b_r, lru_w_i, lru_b_i, lru_lam,
              od_w_out, ffn_w_gate, ffn_w_up, ffn_w_down,
              norm_mix_pre, norm_mix_post, norm_ffn_pre, norm_ffn_post):
    for layer in range(DEPTH):
        i = layer // 2
        h = rms_norm(x, norm_mix_pre[layer])
        if layer % 2 == 0:
            y = even_mixer(h, ev_w_in[i], ev_shift_mu[i], s5_lam_re[i], s5_lam_im[i],
                           s5_log_dt[i], s5_b_re[i], s5_b_im[i], s5_c_re[i], s5_c_im[i],
                           s5_d[i], s5_w_glu[i], rw_w0[i], rw_w2[i], rw_a0[i], rw_a2[i],
                           rw_g2[i], rw_k_k[i], rw_k_a[i], rw_r_k[i], rw_lnx_w[i],
                           rw_lnx_b[i], ev_w_out[i])
        else:
            y = odd_mixer(h, od_w_in[i], od_conv_w[i], od_conv_b[i], lru_w_r[i], lru_b_r[i],
                          lru_w_i[i], lru_b_i[i], lru_lam[i], od_w_out[i])
        x = x + rms_norm(y.astype(x.dtype), norm_mix_post[layer])
        h = rms_norm(x, norm_ffn_pre[layer])
        y = swiglu(h, ffn_w_gate[layer], ffn_w_up[layer], ffn_w_down[layer])
        x = x + rms_norm(y, norm_ffn_post[layer])
    return x
```

```python
import functools
import math

import jax
import jax.numpy as jnp
from jax import lax
from jax.experimental import pallas as pl
from jax.experimental.pallas import tpu as pltpu

F32 = jnp.float32
BF16 = jnp.bfloat16

NORM_EPS = 1e-6
GN_EPS = 64e-5
LRU_C = 8.0
S5_GROUP = 16
S5_STATE = 64
RWKV_HEAD = 64
CONV_WIDTH = 4
LRU_BLOCK = 256

LANE = 128
SUBLANE = 8
VMEM_LIMIT = 56 * 1024 * 1024

RW_CHUNK = 64
HI = lax.Precision.HIGHEST


def _cparams(sem):
    return pltpu.CompilerParams(dimension_semantics=sem, vmem_limit_bytes=VMEM_LIMIT)


def _gelu(x):
    c = math.sqrt(2.0 / math.pi)
    return 0.5 * x * (1.0 + jnp.tanh(c * (x + 0.044715 * (x * x * x))))


def _sigmoid(x):
    return 1.0 / (1.0 + jnp.exp(-x))


def _softplus(x):
    return jnp.maximum(x, 0.0) + jnp.log(1.0 + jnp.exp(-jnp.abs(x)))


def _rms_rows(y, g):
    ms = jnp.mean(y * y, axis=-1, keepdims=True)
    return y * lax.rsqrt(ms + NORM_EPS) * g


def _norm_mm_kernel(x_ref, g_ref, w_ref, o_ref, h_ref, *, n_gelu_tiles):
    j = pl.program_id(1)

    @pl.when(j == 0)
    def _():
        h_ref[...] = _rms_rows(x_ref[...], g_ref[...]).astype(BF16)

    acc = jnp.dot(h_ref[...], w_ref[...].astype(BF16), preferred_element_type=F32)
    if n_gelu_tiles == 0:
        o_ref[...] = acc.astype(o_ref.dtype)
    else:
        @pl.when(j < n_gelu_tiles)
        def _():
            o_ref[...] = _gelu(acc).astype(o_ref.dtype)

        @pl.when(j >= n_gelu_tiles)
        def _():
            o_ref[...] = acc.astype(o_ref.dtype)


def norm_matmul(x, g, w, n_out, *, tm, tn, n_gelu_tiles=0, name):
    m, d = x.shape
    assert m % tm == 0 and n_out % tn == 0 and w.shape[0] == d
    return pl.pallas_call(
        functools.partial(_norm_mm_kernel, n_gelu_tiles=n_gelu_tiles),
        grid=(m // tm, n_out // tn),
        in_specs=[
            pl.BlockSpec((tm, d), lambda i, j: (i, 0)),
            pl.BlockSpec((1, d), lambda i, j: (0, 0)),
            pl.BlockSpec((d, tn), lambda i, j: (0, j)),
        ],
        out_specs=pl.BlockSpec((tm, tn), lambda i, j: (i, j)),
        out_shape=jax.ShapeDtypeStruct((m, n_out), F32),
        scratch_shapes=[pltpu.VMEM((tm, d), BF16)],
        compiler_params=_cparams(("arbitrary", "arbitrary")),
        name=name,
    )(x, g.reshape(1, d), w)


def _norm_swiglu_kernel(x_ref, g_ref, wg_ref, wu_ref, o_ref, h_ref):
    j = pl.program_id(1)

    @pl.when(j == 0)
    def _():
        h_ref[...] = _rms_rows(x_ref[...], g_ref[...]).astype(BF16)

    h = h_ref[...]
    gate = jnp.dot(h, wg_ref[...].astype(BF16), preferred_element_type=F32)
    up = jnp.dot(h, wu_ref[...].astype(BF16), preferred_element_type=F32)
    o_ref[...] = (gate * _sigmoid(gate) * up).astype(o_ref.dtype)


def norm_swiglu_up(x, g, w_gate, w_up, *, tm, tn):
    m, d = x.shape
    n = w_gate.shape[1]
    assert m % tm == 0 and n % tn == 0
    return pl.pallas_call(
        _norm_swiglu_kernel,
        grid=(m // tm, n // tn),
        in_specs=[
            pl.BlockSpec((tm, d), lambda i, j: (i, 0)),
            pl.BlockSpec((1, d), lambda i, j: (0, 0)),
            pl.BlockSpec((d, tn), lambda i, j: (0, j)),
            pl.BlockSpec((d, tn), lambda i, j: (0, j)),
        ],
        out_specs=pl.BlockSpec((tm, tn), lambda i, j: (i, j)),
        out_shape=jax.ShapeDtypeStruct((m, n), BF16),
        scratch_shapes=[pltpu.VMEM((tm, d), BF16)],
        compiler_params=_cparams(("arbitrary", "arbitrary")),
        name="ffn_up",
    )(x, g.reshape(1, d), w_gate, w_up)


def _mm_norm_res_kernel(a_ref, w_ref, x_ref, g_ref, o_ref, acc_ref):
    k = pl.program_id(1)
    part = jnp.dot(a_ref[...], w_ref[...].astype(BF16), preferred_element_type=F32)

    @pl.when(k == 0)
    def _():
        acc_ref[...] = part

    @pl.when(k > 0)
    def _():
        acc_ref[...] += part

    @pl.when(k == pl.num_programs(1) - 1)
    def _():
        o_ref[...] = x_ref[...] + _rms_rows(acc_ref[...], g_ref[...])


def matmul_norm_residual(a, w, x, g, *, tm, tk, name):
    m, kdim = a.shape
    d = w.shape[1]
    assert m % tm == 0 and kdim % tk == 0 and w.shape[0] == kdim
    return pl.pallas_call(
        _mm_norm_res_kernel,
        grid=(m // tm, kdim // tk),
        in_specs=[
            pl.BlockSpec((tm, tk), lambda i, k: (i, k)),
            pl.BlockSpec((tk, d), lambda i, k: (k, 0)),
            pl.BlockSpec((tm, d), lambda i, k: (i, 0)),
            pl.BlockSpec((1, d), lambda i, k: (0, 0)),
        ],
        out_specs=pl.BlockSpec((tm, d), lambda i, k: (i, 0)),
        out_shape=jax.ShapeDtypeStruct((m, d), F32),
        scratch_shapes=[pltpu.VMEM((tm, d), F32)],
        compiler_params=_cparams(("arbitrary", "arbitrary")),
        name=name,
    )(a, w, x, g.reshape(1, d))


def _mm2_norm_res_kernel(a0_ref, a1_ref, w_ref, x_ref, g_ref, o_ref, acc_ref):
    k = pl.program_id(1)
    w = w_ref[...].astype(BF16)

    @pl.when(k == 0)
    def _():
        acc_ref[...] = jnp.dot(a0_ref[...], w, preferred_element_type=F32)

    @pl.when(k == 1)
    def _():
        y = acc_ref[...] + jnp.dot(a1_ref[...], w, preferred_element_type=F32)
        o_ref[...] = x_ref[...] + _rms_rows(y, g_ref[...])


def concat_matmul_norm_residual(a0, a1, w, x, g, *, tm):
    m, half = a0.shape
    d = w.shape[1]
    assert a1.shape == a0.shape and w.shape[0] == 2 * half and m % tm == 0
    return pl.pallas_call(
        _mm2_norm_res_kernel,
        grid=(m // tm, 2),
        in_specs=[
            pl.BlockSpec((tm, half), lambda i, k: (i, 0)),
            pl.BlockSpec((tm, half), lambda i, k: (i, 0)),
            pl.BlockSpec((half, d), lambda i, k: (k, 0)),
            pl.BlockSpec((tm, d), lambda i, k: (i, 0)),
            pl.BlockSpec((1, d), lambda i, k: (0, 0)),
        ],
        out_specs=pl.BlockSpec((tm, d), lambda i, k: (i, 0)),
        out_shape=jax.ShapeDtypeStruct((m, d), F32),
        scratch_shapes=[pltpu.VMEM((tm, d), F32)],
        compiler_params=_cparams(("arbitrary", "arbitrary")),
        name="even_out_proj",
    )(a0, a1, w, x, g.reshape(1, d))


def _block_ones(n, seg):
    r = lax.broadcasted_iota(jnp.int32, (n, n), 0) // seg
    c = lax.broadcasted_iota(jnp.int32, (n, n), 1) // seg
    return (r == c).astype(BF16)


def _seg_sum(x, ones):
    outs = []
    for j in range(x.shape[1] // LANE):
        xj = x[:, j * LANE:(j + 1) * LANE]
        hi = xj.astype(BF16)
        r1 = xj - hi.astype(F32)
        mid = r1.astype(BF16)
        lo = (r1 - mid.astype(F32)).astype(BF16)
        s = (jnp.dot(hi, ones, preferred_element_type=F32)
             + jnp.dot(mid, ones, preferred_element_type=F32)
             + jnp.dot(lo, ones, preferred_element_type=F32))
        outs.append(s)
    return jnp.concatenate(outs, axis=1) if len(outs) > 1 else outs[0]


def _s5_disc_kernel(lr_ref, li_ref, ldt_ref, bre_ref, bim_ref,
                    are_ref, aim_ref, bbre_ref, bbim_ref):
    lr = lr_ref[...]
    li = li_ref[...]
    dt = jnp.exp(ldt_ref[...])
    mag = jnp.exp(lr * dt)
    a_re = mag * jnp.cos(li * dt)
    a_im = mag * jnp.sin(li * dt)
    den = lr * lr + li * li
    nr = a_re - 1.0
    ni = a_im
    gam_re = (nr * lr + ni * li) / den
    gam_im = (ni * lr - nr * li) / den
    are_ref[...] = a_re
    aim_ref[...] = a_im
    for c in range(bre_ref.shape[0]):
        br = bre_ref[c]
        bi = bim_ref[c]
        bbre_ref[c] = gam_re * br - gam_im * bi
        bbim_ref[c] = gam_re * bi + gam_im * br


def s5_discretise(lam_re, lam_im, log_dt, b_re, b_im):
    g, p, c = b_re.shape
    ldt = jnp.broadcast_to(log_dt[:, None], (g, p))
    b_re_t = jnp.transpose(b_re, (2, 0, 1))
    b_im_t = jnp.transpose(b_im, (2, 0, 1))
    gp = jax.ShapeDtypeStruct((g, p), F32)
    cgp = jax.ShapeDtypeStruct((c, g, p), F32)
    return pl.pallas_call(
        _s5_disc_kernel,
        out_shape=(gp, gp, cgp, cgp),
        name="s5_discretise",
    )(lam_re, lam_im, ldt, b_re_t, b_im_t)


def _cmul(ar, ai, br, bi):
    return ar * br - ai * bi, ar * bi + ai * br


def _s5_scan_kernel(u_ref, bre_ref, bim_ref, cre_ref, cim_ref, are_ref, aim_ref, d_ref,
                    o_ref, sre_ref, sim_ref, bur_ref, bui_ref, *, tc):
    t_idx = pl.program_id(1)
    n_tiles = u_ref.shape[1] // LANE
    ml = bre_ref.shape[2]
    nblk = tc // SUBLANE

    @pl.when(t_idx == 0)
    def _():
        sre_ref[...] = jnp.zeros_like(sre_ref)
        sim_ref[...] = jnp.zeros_like(sim_ref)

    row = lax.broadcasted_iota(jnp.int32, (SUBLANE, ml), 0)

    for j in range(n_tiles):
        lanes = slice(j * ml, (j + 1) * ml)
        uj = u_ref[:, j * LANE:(j + 1) * LANE]
        ujb = uj.astype(BF16)
        bur_ref[...] = jnp.dot(ujb, bre_ref[j], preferred_element_type=F32)
        bui_ref[...] = jnp.dot(ujb, bim_ref[j], preferred_element_type=F32)

        a1r = jnp.broadcast_to(are_ref[:, lanes], (SUBLANE, ml))
        a1i = jnp.broadcast_to(aim_ref[:, lanes], (SUBLANE, ml))
        a2r, a2i = _cmul(a1r, a1i, a1r, a1i)
        a4r, a4i = _cmul(a2r, a2i, a2r, a2i)
        pr, pi = a1r, a1i
        cr, ci = a1r, a1i
        for r in range(1, SUBLANE):
            cr, ci = _cmul(cr, ci, a1r, a1i)
            pr = jnp.where(row >= r, cr, pr)
            pi = jnp.where(row >= r, ci, pi)
        zero = jnp.zeros((SUBLANE, ml), F32)
        coef = []
        for d, (xr, xi) in ((1, (a1r, a1i)), (2, (a2r, a2i)), (4, (a4r, a4i))):
            coef.append((d, jnp.where(row >= d, xr, zero), jnp.where(row >= d, xi, zero)))

        def blk(b, carry):
            cre, cim = carry
            r0 = pl.multiple_of(b * SUBLANE, SUBLANE)
            sr = bur_ref[pl.ds(r0, SUBLANE), :]
            si = bui_ref[pl.ds(r0, SUBLANE), :]
            for d, kr, ki in coef:
                rr = pltpu.roll(sr, d, axis=0)
                ri = pltpu.roll(si, d, axis=0)
                sr, si = sr + kr * rr - ki * ri, si + kr * ri + ki * rr
            cbr = jnp.broadcast_to(cre, (SUBLANE, ml))
            cbi = jnp.broadcast_to(cim, (SUBLANE, ml))
            sr = sr + pr * cbr - pi * cbi
            si = si + pr * cbi + pi * cbr
            bur_ref[pl.ds(r0, SUBLANE), :] = sr
            bui_ref[pl.ds(r0, SUBLANE), :] = si
            return sr[SUBLANE - 1:SUBLANE, :], si[SUBLANE - 1:SUBLANE, :]

        cre, cim = lax.fori_loop(0, nblk, blk, (sre_ref[:, lanes], sim_ref[:, lanes]))
        sre_ref[:, lanes] = cre
        sim_ref[:, lanes] = cim

        y = (jnp.dot(bur_ref[...].astype(BF16), cre_ref[j], preferred_element_type=F32)
             - jnp.dot(bui_ref[...].astype(BF16), cim_ref[j], preferred_element_type=F32))
        y = y + d_ref[:, j * LANE:(j + 1) * LANE] * uj
        o_ref[:, j * LANE:(j + 1) * LANE] = _gelu(y)


def s5_scan(p_main, bblk_re, bblk_im, cblk_re, cblk_im, a_re, a_im, d_skip, *, batch, tc):
    m = p_main.shape[0]
    t_len = m // batch
    n_tiles, _, ml = bblk_re.shape
    width = n_tiles * LANE
    modes = n_tiles * ml
    nt = t_len // tc
    assert t_len % tc == 0
    return pl.pallas_call(
        functools.partial(_s5_scan_kernel, tc=tc),
        grid=(batch, nt),
        in_specs=[
            pl.BlockSpec((tc, width), lambda b, t: (b * nt + t, 0)),
            pl.BlockSpec((n_tiles, LANE, ml), lambda b, t: (0, 0, 0)),
            pl.BlockSpec((n_tiles, LANE, ml), lambda b, t: (0, 0, 0)),
            pl.BlockSpec((n_tiles, ml, LANE), lambda b, t: (0, 0, 0)),
            pl.BlockSpec((n_tiles, ml, LANE), lambda b, t: (0, 0, 0)),
            pl.BlockSpec((1, modes), lambda b, t: (0, 0)),
            pl.BlockSpec((1, modes), lambda b, t: (0, 0)),
            pl.BlockSpec((1, width), lambda b, t: (0, 0)),
        ],
        out_specs=pl.BlockSpec((tc, width), lambda b, t: (b * nt + t, 0)),
        out_shape=jax.ShapeDtypeStruct((m, width), F32),
        scratch_shapes=[
            pltpu.VMEM((1, modes), F32), pltpu.VMEM((1, modes), F32),
            pltpu.VMEM((tc, ml), F32), pltpu.VMEM((tc, ml), F32),
        ],
        compiler_params=_cparams(("arbitrary", "arbitrary")),
        name="s5_scan",
    )(p_main, bblk_re, bblk_im, cblk_re, cblk_im, a_re, a_im, d_skip)


def _s5_glu_kernel(y_ref, yt_ref, w_ref, o_ref):
    z = jnp.dot(y_ref[...].astype(BF16), w_ref[...].astype(BF16), preferred_element_type=F32)
    o_ref[...] = (yt_ref[...] * _sigmoid(z)).astype(o_ref.dtype)


def s5_glu(y, w, *, tm, tn):
    m, d = y.shape
    return pl.pallas_call(
        _s5_glu_kernel,
        grid=(m // tm, d // tn),
        in_specs=[
            pl.BlockSpec((tm, d), lambda i, j: (i, 0)),
            pl.BlockSpec((tm, tn), lambda i, j: (i, j)),
            pl.BlockSpec((d, tn), lambda i, j: (0, j)),
        ],
        out_specs=pl.BlockSpec((tm, tn), lambda i, j: (i, j)),
        out_shape=jax.ShapeDtypeStruct((m, d), BF16),
        compiler_params=_cparams(("arbitrary", "arbitrary")),
        name="s5_glu",
    )(y, y, w)


def _shift_mix(z, prev_row, mu):
    row = lax.broadcasted_iota(jnp.int32, z.shape, 0)
    zs = jnp.where(row == 0, jnp.broadcast_to(prev_row, z.shape), pltpu.roll(z, 1, axis=0))
    return z + (zs - z) * mu


def _rwkv_prep_kernel(r_ref, k_ref, v_ref, lr_ref, rp_ref, kp_ref, vp_ref, lrp_ref,
                      mur_ref, muk_ref, muv_ref, mulr_ref,
                      w0_ref, w2_ref, a0_ref, a2_ref, g2_ref, kk_ref, ka_ref, rk_ref,
                      ro_ref, lw_ref, ko_ref, vo_ref, ao_ref, bo_ref, bon_ref, go_ref,
                      *, tiles_per_batch):
    i = pl.program_id(0)
    first = (i % tiles_per_batch) == 0
    last = SUBLANE - 1

    def prev(ref):
        return jnp.where(first, 0.0, ref[last:last + 1, :])

    r = _shift_mix(r_ref[...], prev(rp_ref), mur_ref[...])
    k = _shift_mix(k_ref[...], prev(kp_ref), muk_ref[...])
    v = _shift_mix(v_ref[...], prev(vp_ref), muv_ref[...])
    lr = _shift_mix(lr_ref[...], prev(lrp_ref), mulr_ref[...])

    wl = w0_ref[...] + jnp.dot(jnp.tanh(lr).astype(BF16), w2_ref[...].astype(BF16),
                               preferred_element_type=F32)
    w = -_softplus(-wl) - 0.5
    lw_ref[...] = -jnp.exp(w)
    a = _sigmoid(a0_ref[...] + jnp.dot(lr.astype(BF16), a2_ref[...].astype(BF16),
                                       preferred_element_type=F32))
    go_ref[...] = jnp.dot(_sigmoid(lr).astype(BF16), g2_ref[...].astype(BF16),
                          preferred_element_type=F32)

    ones = _block_ones(LANE, RWKV_HEAD)
    kk = k * kk_ref[...]
    kk = kk * lax.rsqrt(jnp.maximum(_seg_sum(kk * kk, ones), 1e-24))
    k = k * (1.0 + (a - 1.0) * ka_ref[...])
    ro_ref[...] = r
    ko_ref[...] = k
    vo_ref[...] = v
    ao_ref[...] = -kk
    bo_ref[...] = kk * a
    bon_ref[...] = _seg_sum(r * k * rk_ref[...], ones) * v


def rwkv_prep(p_main, lr, mu, w0, w2, a0, a2, g2, k_k, k_a, r_k, *, batch, tm):
    m = p_main.shape[0]
    t_len = m // batch
    hw = w0.shape[0]
    lrw = lr.shape[1]
    w_rank, a_rank, g_rank = w2.shape[0], a2.shape[0], g2.shape[0]
    nblk8 = tm // SUBLANE
    assert t_len % tm == 0 and p_main.shape[1] == 4 * hw

    def cur(c):
        return pl.BlockSpec((tm, hw), lambda i: (i, c))

    def prv(c):
        return pl.BlockSpec((SUBLANE, hw), lambda i: (jnp.maximum(i * nblk8 - 1, 0), c))

    row = lambda n: pl.BlockSpec((1, n), lambda i: (0, 0))
    mat = lambda a: pl.BlockSpec(a.shape, lambda i: (0, 0))
    used = w_rank + a_rank + g_rank
    mu_lr = jnp.pad(mu[3 * hw:], (0, lrw - used))
    w2 = jnp.pad(w2, ((0, lrw - w_rank), (0, 0)))
    a2 = jnp.pad(a2, ((w_rank, lrw - w_rank - a_rank), (0, 0)))
    g2 = jnp.pad(g2, ((w_rank + a_rank, lrw - used), (0, 0)))
    out = jax.ShapeDtypeStruct((m, hw), F32)
    return pl.pallas_call(
        functools.partial(_rwkv_prep_kernel, tiles_per_batch=t_len // tm),
        grid=(m // tm,),
        in_specs=[
            cur(1), cur(2), cur(3), pl.BlockSpec((tm, lrw), lambda i: (i, 0)),
            prv(1), prv(2), prv(3),
            pl.BlockSpec((SUBLANE, lrw), lambda i: (jnp.maximum(i * nblk8 - 1, 0), 0)),
            row(hw), row(hw), row(hw), row(lrw),
            row(hw), mat(w2), row(hw), mat(a2), mat(g2), row(hw), row(hw), row(hw),
        ],
        out_specs=[pl.BlockSpec((tm, hw), lambda i: (i, 0))] * 8,
        out_shape=(out,) * 8,
        compiler_params=_cparams(("arbitrary",)),
        name="rwkv_prep",
    )(p_main, p_main, p_main, lr, p_main, p_main, p_main, lr,
      mu[:hw].reshape(1, hw), mu[hw:2 * hw].reshape(1, hw), mu[2 * hw:3 * hw].reshape(1, hw),
      mu_lr.reshape(1, lrw),
      w0.reshape(1, hw), w2, a0.reshape(1, hw), a2, g2, k_k.reshape(1, hw),
      k_a.reshape(1, hw), r_k.reshape(1, hw))


def _dot(a, b):
    return jnp.dot(a, b, preferred_element_type=F32, precision=HI)


def _dot_nt(a, b):
    return lax.dot_general(a, b, (((1,), (1,)), ((), ())), preferred_element_type=F32,
                           precision=HI)


def _dot_tn(a, b):
    return lax.dot_general(a, b, (((0,), (0,)), ((), ())), preferred_element_type=F32,
                           precision=HI)


def _rwkv_scan_kernel(r_ref, lw_ref, k_ref, v_ref, a_ref, b_ref, y_ref, z_ref, *, n_chunks):
    c_len = RW_CHUNK
    n2 = 2 * c_len

    @pl.when(pl.program_id(2) == 0)
    def _():
        z_ref[...] = jnp.zeros_like(z_ref)

    lane = lax.broadcasted_iota(jnp.int32, (c_len, LANE), 1)
    head_a = lane < RWKV_HEAD
    ri = lax.broadcasted_iota(jnp.int32, (n2, n2), 0)
    ci = lax.broadcasted_iota(jnp.int32, (n2, n2), 1)
    same = (ri // c_len) == (ci // c_len)
    strict = same & ((ci % c_len) < (ri % c_len))
    incl = same & ((ci % c_len) <= (ri % c_len))
    eye = (ri == ci).astype(F32)
    rc = lax.broadcasted_iota(jnp.int32, (c_len, c_len), 0)
    cc = lax.broadcasted_iota(jnp.int32, (c_len, c_len), 1)
    tri = (cc <= rc).astype(F32)

    def stack(x):
        return jnp.concatenate([jnp.where(head_a, x, 0.0), jnp.where(head_a, 0.0, x)], axis=0)

    def chunk(c, carry):
        r0 = pl.multiple_of(c * c_len, c_len)
        rows = pl.ds(r0, c_len)
        r, lw, k, v = r_ref[rows, :], lw_ref[rows, :], k_ref[rows, :], v_ref[rows, :]
        a, b = a_ref[rows, :], b_ref[rows, :]
        cl = _dot(tri, lw)
        cl_end = cl[c_len - 1:c_len, :]
        g_inv = jnp.exp(-cl)
        g_tail = jnp.exp(cl_end - cl)
        a_s = stack(a * jnp.exp(cl - lw))
        r_s = stack(r * jnp.exp(cl))
        b_s = stack(b * g_inv)
        k_s = stack(k * g_inv)
        v_s = stack(v)
        bg_s = stack(b * g_tail)
        kg_s = stack(k * g_tail)

        l_ab = jnp.where(strict, _dot_nt(a_s, b_s), 0.0)
        l_ak = jnp.where(strict, _dot_nt(a_s, k_s), 0.0)
        m_rb = jnp.where(incl, _dot_nt(r_s, b_s), 0.0)
        m_rk = jnp.where(incl, _dot_nt(r_s, k_s), 0.0)

        x = l_ab
        t_inv = eye + x
        n = 2
        while n < c_len:
            x = _dot(x, x)
            t_inv = t_inv + _dot(t_inv, x)
            n *= 2

        zt = z_ref[...]
        u = _dot(t_inv, _dot_nt(a_s, zt) + _dot(l_ak, v_s))
        y_s = _dot_nt(r_s, zt) + _dot(m_rb, u) + _dot(m_rk, v_s)
        y_ref[rows, :] = y_s[:c_len, :] + y_s[c_len:, :]
        z_ref[...] = zt * jnp.exp(cl_end) + _dot_tn(u, bg_s) + _dot_tn(v_s, kg_s)
        return carry

    lax.fori_loop(0, n_chunks, chunk, 0)


def rwkv_scan(r, lw, k, v, a, b, *, batch, tb):
    m, hw = r.shape
    t_len = m // batch
    nt = t_len // tb
    assert t_len % tb == 0 and tb % RW_CHUNK == 0 and hw % LANE == 0
    spec = pl.BlockSpec((tb, LANE), lambda bi, p, t: (bi * nt + t, p))
    return pl.pallas_call(
        functools.partial(_rwkv_scan_kernel, n_chunks=tb // RW_CHUNK),
        grid=(batch, hw // LANE, nt),
        in_specs=[spec] * 6,
        out_specs=spec,
        out_shape=jax.ShapeDtypeStruct((m, hw), F32),
        scratch_shapes=[pltpu.VMEM((LANE, LANE), F32)],
        compiler_params=_cparams(("arbitrary", "arbitrary", "arbitrary")),
        name="rwkv_scan",
    )(r, lw, k, v, a, b)


def _rwkv_post_kernel(y_ref, bon_ref, g_ref, lw_ref, lb_ref, o_ref):
    ones = _block_ones(LANE, RWKV_HEAD)
    y = y_ref[...]
    inv_n = 1.0 / RWKV_HEAD
    mu = _seg_sum(y, ones) * inv_n
    yc = y - mu
    var = _seg_sum(yc * yc, ones) * inv_n
    yn = yc * lax.rsqrt(var + GN_EPS) * lw_ref[...] + lb_ref[...]
    o_ref[...] = ((yn + bon_ref[...]) * g_ref[...]).astype(o_ref.dtype)


def rwkv_post(y, bonus, g, lnx_w, lnx_b, *, tm):
    m, hw = y.shape
    spec = pl.BlockSpec((tm, hw), lambda i: (i, 0))
    row = pl.BlockSpec((1, hw), lambda i: (0, 0))
    return pl.pallas_call(
        _rwkv_post_kernel,
        grid=(m // tm,),
        in_specs=[spec, spec, spec, row, row],
        out_specs=spec,
        out_shape=jax.ShapeDtypeStruct((m, hw), BF16),
        compiler_params=_cparams(("arbitrary",)),
        name="rwkv_post",
    )(y, bonus, g, lnx_w.reshape(1, hw), lnx_b.reshape(1, hw))


def _lru_kernel(gate_ref, xb_ref, cw_ref, cb_ref, wr_ref, br_ref, wi_ref, bi_ref, lam_ref,
                o_ref, halo_ref, h_ref, a_scr, b_scr, *, tc):
    t_idx = pl.program_id(1)
    width = xb_ref.shape[1]
    nblk = tc // SUBLANE
    n_gate_blocks = width // LRU_BLOCK

    @pl.when(t_idx == 0)
    def _():
        halo_ref[...] = jnp.zeros_like(halo_ref)
        h_ref[...] = jnp.zeros_like(h_ref)

    xb = xb_ref[...]
    ext = jnp.concatenate([halo_ref[...], xb], axis=0)
    xc = xb * cw_ref[CONV_WIDTH - 1:CONV_WIDTH, :] + cb_ref[...]
    for d in range(1, CONV_WIDTH):
        sh = pltpu.roll(ext, d, axis=0)[SUBLANE:, :]
        xc = xc + sh * cw_ref[CONV_WIDTH - 1 - d:CONV_WIDTH - d, :]
    halo_ref[...] = xb[tc - SUBLANE:, :]

    sp = _softplus(-lam_ref[...])
    row = lax.broadcasted_iota(jnp.int32, (SUBLANE, LRU_BLOCK), 0)

    for n in range(n_gate_blocks):
        lanes = slice(n * LRU_BLOCK, (n + 1) * LRU_BLOCK)
        xn = xc[:, lanes]
        xnb = xn.astype(BF16)
        gr = jnp.dot(xnb, wr_ref[n].astype(BF16), preferred_element_type=F32) + br_ref[:, lanes]
        gi = jnp.dot(xnb, wi_ref[n].astype(BF16), preferred_element_type=F32) + bi_ref[:, lanes]
        log_a = -LRU_C * _sigmoid(gr) * sp[:, lanes]
        a = jnp.exp(log_a)
        mult = jnp.sqrt(-jnp.tanh(log_a) * (a * a + 1.0))
        a_scr[...] = a
        b_scr[...] = mult * _sigmoid(gi) * xn

        def blk(bidx, carry):
            r0 = pl.multiple_of(bidx * SUBLANE, SUBLANE)
            av = a_scr[pl.ds(r0, SUBLANE), :]
            bv = b_scr[pl.ds(r0, SUBLANE), :]
            for d in (1, 2, 4):
                keep = row >= d
                ash = jnp.where(keep, pltpu.roll(av, d, axis=0), 1.0)
                bsh = jnp.where(keep, pltpu.roll(bv, d, axis=0), 0.0)
                bv = bv + av * bsh
                av = av * ash
            hv = bv + av * jnp.broadcast_to(carry, (SUBLANE, LRU_BLOCK))
            b_scr[pl.ds(r0, SUBLANE), :] = hv
            return hv[SUBLANE - 1:SUBLANE, :]

        h_last = lax.fori_loop(0, nblk, blk, h_ref[:, lanes])
        h_ref[:, lanes] = h_last
        o_ref[:, lanes] = (b_scr[...] * gate_ref[:, lanes]).astype(o_ref.dtype)


def lru_block(p_odd, conv_w, conv_b, w_r, b_r, w_i, b_i, lam, *, batch, tc):
    m = p_odd.shape[0]
    width = lam.shape[0]
    t_len = m // batch
    nt = t_len // tc
    assert t_len % tc == 0 and p_odd.shape[1] == 2 * width
    row = pl.BlockSpec((1, width), lambda b, t: (0, 0))
    blkw = pl.BlockSpec(w_r.shape, lambda b, t: (0, 0, 0))
    return pl.pallas_call(
        functools.partial(_lru_kernel, tc=tc),
        grid=(batch, nt),
        in_specs=[
            pl.BlockSpec((tc, width), lambda b, t: (b * nt + t, 0)),
            pl.BlockSpec((tc, width), lambda b, t: (b * nt + t, 1)),
            pl.BlockSpec((CONV_WIDTH, width), lambda b, t: (0, 0)),
            row, blkw, row, blkw, row, row,
        ],
        out_specs=pl.BlockSpec((tc, width), lambda b, t: (b * nt + t, 0)),
        out_shape=jax.ShapeDtypeStruct((m, width), BF16),
        scratch_shapes=[
            pltpu.VMEM((SUBLANE, width), F32), pltpu.VMEM((1, width), F32),
            pltpu.VMEM((tc, LRU_BLOCK), F32), pltpu.VMEM((tc, LRU_BLOCK), F32),
        ],
        compiler_params=_cparams(("arbitrary", "arbitrary")),
        name="lru_block",
    )(p_odd, p_odd, conv_w, conv_b.reshape(1, width), w_r, b_r.reshape(1, width),
      w_i, b_i.reshape(1, width), lam.reshape(1, width))


def _s5_block_weights(bb_re, bb_im, c_re, c_im):
    c, g, p = bb_re.shape
    gpt = LANE // c
    tiles = g // gpt
    eye = jnp.eye(gpt, dtype=F32)

    def b_blk(bb):
        bb = bb.reshape(c, tiles, gpt, p)
        return jnp.einsum('cjgp,gh->jgchp', bb, eye).reshape(tiles, gpt * c, gpt * p)

    def c_blk(cm):
        cm = cm.reshape(tiles, gpt, c, p)
        return jnp.einsum('jgcp,gh->jhpgc', cm, eye).reshape(tiles, gpt * p, gpt * c)

    return (b_blk(bb_re).astype(BF16), b_blk(bb_im).astype(BF16),
            c_blk(c_re).astype(BF16), c_blk(c_im).astype(BF16))


def _even_mixer(x, g_pre, g_post, batch, w_in, shift_mu, lam_re, lam_im, log_dt, b_re, b_im,
                c_re, c_im, d_skip, w_glu, w0, w2, a0, a2, g2, k_k, k_a, r_k, lnx_w, lnx_b,
                w_out):
    hw = w0.shape[0]
    s5w = d_skip.shape[0]
    n_main = s5w + 3 * hw
    p_main = norm_matmul(x, g_pre, w_in, n_main, tm=1024, tn=512, name="even_in_proj")
    lrw = w_in.shape[1] - n_main
    lr_pad = -(-lrw // LANE) * LANE
    w_lr = jnp.pad(w_in[:, n_main:], ((0, 0), (0, lr_pad - lrw)))
    lr = norm_matmul(x, g_pre, w_lr, lr_pad, tm=1024, tn=lr_pad, name="even_lr_proj")

    a_re, a_im, bb_re, bb_im = s5_discretise(lam_re, lam_im, log_dt, b_re, b_im)
    bblk_re, bblk_im, cblk_re, cblk_im = _s5_block_weights(bb_re, bb_im, c_re, c_im)
    y_s5 = s5_scan(p_main, bblk_re, bblk_im, cblk_re, cblk_im,
                   a_re.reshape(1, -1), a_im.reshape(1, -1), d_skip.reshape(1, -1),
                   batch=batch, tc=256)
    y_s5 = s5_glu(y_s5, w_glu, tm=1024, tn=512)

    r, lw, k, v, a, b, bonus, g = rwkv_prep(p_main, lr, shift_mu, w0, w2, a0, a2, g2,
                                            k_k, k_a, r_k.reshape(-1), batch=batch, tm=256)
    y_rw = rwkv_scan(r, lw, k, v, a, b, batch=batch, tb=512)
    y_rw = rwkv_post(y_rw, bonus, g, lnx_w, lnx_b, tm=512)

    return concat_matmul_norm_residual(y_s5, y_rw, w_out, x, g_post, tm=512)


def _odd_mixer(x, g_pre, g_post, batch, w_in, conv_w, conv_b, w_r, b_r, w_i, b_i, lam, w_out):
    width = lam.shape[0]
    p_odd = norm_matmul(x, g_pre, w_in, 2 * width, tm=1024, tn=512,
                        n_gelu_tiles=width // 512, name="odd_in_proj")
    hg = lru_block(p_odd, conv_w, conv_b, w_r, b_r, w_i, b_i, lam, batch=batch, tc=256)
    return matmul_norm_residual(hg, w_out, x, g_post, tm=512, tk=1024, name="odd_out_proj")


def _ffn(x, g_pre, g_post, w_gate, w_up, w_down):
    act = norm_swiglu_up(x, g_pre, w_gate, w_up, tm=1024, tn=512)
    return matmul_norm_residual(act, w_down, x, g_post, tm=512, tk=512, name="ffn_down")


def kernel(x, ev_w_in, ev_shift_mu, s5_lam_re, s5_lam_im, s5_log_dt, s5_b_re, s5_b_im, s5_c_re, s5_c_im, s5_d, s5_w_glu, rw_w0, rw_w2, rw_a0, rw_a2, rw_g2, rw_k_k, rw_k_a, rw_r_k, rw_lnx_w, rw_lnx_b, ev_w_out, od_w_in, od_conv_w, od_conv_b, lru_w_r, lru_b_r, lru_w_i, lru_b_i, lru_lam, od_w_out, ffn_w_gate, ffn_w_up, ffn_w_down, norm_mix_pre, norm_mix_post, norm_ffn_pre, norm_ffn_post):
    batch, t_len, d = x.shape
    depth = ffn_w_gate.shape[0]
    xf = x.reshape(batch * t_len, d)
    for layer in range(depth):
        i = layer // 2
        if layer % 2 == 0:
            xf = _even_mixer(xf, norm_mix_pre[layer], norm_mix_post[layer], batch,
                             ev_w_in[i], ev_shift_mu[i], s5_lam_re[i], s5_lam_im[i],
                             s5_log_dt[i], s5_b_re[i], s5_b_im[i], s5_c_re[i], s5_c_im[i],
                             s5_d[i], s5_w_glu[i], rw_w0[i], rw_w2[i], rw_a0[i], rw_a2[i],
                             rw_g2[i], rw_k_k[i], rw_k_a[i], rw_r_k[i], rw_lnx_w[i],
                             rw_lnx_b[i], ev_w_out[i])
        else:
            xf = _odd_mixer(xf, norm_mix_pre[layer], norm_mix_post[layer], batch,
                            od_w_in[i], od_conv_w[i], od_conv_b[i], lru_w_r[i], lru_b_r[i],
                            lru_w_i[i], lru_b_i[i], lru_lam[i], od_w_out[i])
        xf = _ffn(xf, norm_ffn_pre[layer], norm_ffn_post[layer],
                  ffn_w_gate[layer], ffn_w_up[layer], ffn_w_down[layer])
    return xf.reshape(batch, t_len, d)
```

```python
import functools
import math

import jax
import jax.numpy as jnp
from jax import lax
from jax.experimental import pallas as pl
from jax.experimental.pallas import tpu as pltpu

F32 = jnp.float32
BF16 = jnp.bfloat16

NORM_EPS = 1e-6
GN_EPS = 64e-5
LRU_C = 8.0
S5_GROUP = 16
S5_STATE = 64
RWKV_HEAD = 64
CONV_WIDTH = 4
LRU_BLOCK = 256

LANE = 128
SUBLANE = 8
VMEM_LIMIT = 56 * 1024 * 1024

RW_CHUNK = 64


def _cparams(sem):
    return pltpu.CompilerParams(dimension_semantics=sem, vmem_limit_bytes=VMEM_LIMIT)


def _gelu(x):
    c = math.sqrt(2.0 / math.pi)
    return 0.5 * x * (1.0 + jnp.tanh(c * (x + 0.044715 * (x * x * x))))


def _sigmoid(x):
    return 1.0 / (1.0 + jnp.exp(-x))


def _softplus(x):
    return jnp.maximum(x, 0.0) + jnp.log(1.0 + jnp.exp(-jnp.abs(x)))


def _rms_rows(y, g):
    ms = jnp.mean(y * y, axis=-1, keepdims=True)
    return y * lax.rsqrt(ms + NORM_EPS) * g


def _norm_mm_kernel(x_ref, g_ref, w_ref, o_ref, h_ref, *, n_gelu_tiles):
    j = pl.program_id(1)

    @pl.when(j == 0)
    def _():
        h_ref[...] = _rms_rows(x_ref[...], g_ref[...]).astype(BF16)

    acc = jnp.dot(h_ref[...], w_ref[...], preferred_element_type=F32)
    if n_gelu_tiles == 0:
        o_ref[...] = acc.astype(o_ref.dtype)
    else:
        @pl.when(j < n_gelu_tiles)
        def _():
            o_ref[...] = _gelu(acc).astype(o_ref.dtype)

        @pl.when(j >= n_gelu_tiles)
        def _():
            o_ref[...] = acc.astype(o_ref.dtype)


def norm_matmul(x, g, w, layer, n_out, *, tm, tn, n_gelu_tiles=0, name):
    m, d = x.shape
    assert m % tm == 0 and n_out % tn == 0 and w.shape[1] == d
    return pl.pallas_call(
        functools.partial(_norm_mm_kernel, n_gelu_tiles=n_gelu_tiles),
        grid=(m // tm, n_out // tn),
        in_specs=[
            pl.BlockSpec((tm, d), lambda i, j: (i, 0)),
            pl.BlockSpec((1, d), lambda i, j: (0, 0)),
            pl.BlockSpec((None, d, tn), lambda i, j: (layer, 0, j)),
        ],
        out_specs=pl.BlockSpec((tm, tn), lambda i, j: (i, j)),
        out_shape=jax.ShapeDtypeStruct((m, n_out), F32),
        scratch_shapes=[pltpu.VMEM((tm, d), BF16)],
        compiler_params=_cparams(("arbitrary", "arbitrary")),
        name=name,
    )(x, g.reshape(1, d), w)


def _norm_swiglu_kernel(x_ref, g_ref, wg_ref, wu_ref, o_ref, h_ref):
    j = pl.program_id(1)

    @pl.when(j == 0)
    def _():
        h_ref[...] = _rms_rows(x_ref[...], g_ref[...]).astype(BF16)

    h = h_ref[...]
    gate = jnp.dot(h, wg_ref[...], preferred_element_type=F32)
    up = jnp.dot(h, wu_ref[...], preferred_element_type=F32)
    o_ref[...] = (gate * _sigmoid(gate) * up).astype(o_ref.dtype)


def norm_swiglu_up(x, g, w_gate, w_up, layer, *, tm, tn):
    m, d = x.shape
    n = w_gate.shape[2]
    assert m % tm == 0 and n % tn == 0
    return pl.pallas_call(
        _norm_swiglu_kernel,
        grid=(m // tm, n // tn),
        in_specs=[
            pl.BlockSpec((tm, d), lambda i, j: (i, 0)),
            pl.BlockSpec((1, d), lambda i, j: (0, 0)),
            pl.BlockSpec((None, d, tn), lambda i, j: (layer, 0, j)),
            pl.BlockSpec((None, d, tn), lambda i, j: (layer, 0, j)),
        ],
        out_specs=pl.BlockSpec((tm, tn), lambda i, j: (i, j)),
        out_shape=jax.ShapeDtypeStruct((m, n), BF16),
        scratch_shapes=[pltpu.VMEM((tm, d), BF16)],
        compiler_params=_cparams(("arbitrary", "arbitrary")),
        name="ffn_up",
    )(x, g.reshape(1, d), w_gate, w_up)


def _mm_norm_res_kernel(a_ref, w_ref, x_ref, g_ref, o_ref, acc_ref):
    k = pl.program_id(1)

    @pl.when(k == 0)
    def _():
        acc_ref[...] = jnp.zeros_like(acc_ref)

    acc_ref[...] += jnp.dot(a_ref[...], w_ref[...], preferred_element_type=F32)

    @pl.when(k == pl.num_programs(1) - 1)
    def _():
        o_ref[...] = x_ref[...] + _rms_rows(acc_ref[...], g_ref[...])


def matmul_norm_residual(a, w, layer, x, g, *, tm, tk, name):
    m, kdim = a.shape
    d = w.shape[2]
    assert m % tm == 0 and kdim % tk == 0 and w.shape[1] == kdim
    return pl.pallas_call(
        _mm_norm_res_kernel,
        grid=(m // tm, kdim // tk),
        in_specs=[
            pl.BlockSpec((tm, tk), lambda i, k: (i, k)),
            pl.BlockSpec((None, tk, d), lambda i, k: (layer, k, 0)),
            pl.BlockSpec((tm, d), lambda i, k: (i, 0)),
            pl.BlockSpec((1, d), lambda i, k: (0, 0)),
        ],
        out_specs=pl.BlockSpec((tm, d), lambda i, k: (i, 0)),
        out_shape=jax.ShapeDtypeStruct((m, d), F32),
        scratch_shapes=[pltpu.VMEM((tm, d), F32)],
        compiler_params=_cparams(("arbitrary", "arbitrary")),
        name=name,
    )(a, w, x, g.reshape(1, d))


def _out_proj_kernel(*refs, n_lhs):
    a_refs = refs[:n_lhs]
    w_ref, x_ref, g_ref, o_ref = refs[n_lhs:]
    y = None
    k0 = 0
    for a_ref in a_refs:
        kw = a_ref.shape[1]
        part = jnp.dot(a_ref[...], w_ref[k0:k0 + kw, :], preferred_element_type=F32)
        y = part if y is None else y + part
        k0 += kw
    o_ref[...] = x_ref[...] + _rms_rows(y, g_ref[...])


def out_proj_norm_residual(lhs, w, layer, x, g, *, tm, name):
    m = x.shape[0]
    kdim, d = w.shape[1], w.shape[2]
    assert sum(a.shape[1] for a in lhs) == kdim and m % tm == 0
    return pl.pallas_call(
        functools.partial(_out_proj_kernel, n_lhs=len(lhs)),
        grid=(m // tm,),
        in_specs=[pl.BlockSpec((tm, a.shape[1]), lambda i: (i, 0)) for a in lhs] + [
            pl.BlockSpec((None, kdim, d), lambda i: (layer, 0, 0)),
            pl.BlockSpec((tm, d), lambda i: (i, 0)),
            pl.BlockSpec((1, d), lambda i: (0, 0)),
        ],
        out_specs=pl.BlockSpec((tm, d), lambda i: (i, 0)),
        out_shape=jax.ShapeDtypeStruct((m, d), F32),
        compiler_params=_cparams(("arbitrary",)),
        name=name,
    )(*lhs, w, x, g.reshape(1, d))


def _block_ones(n, seg):
    r = lax.broadcasted_iota(jnp.int32, (n, n), 0) // seg
    c = lax.broadcasted_iota(jnp.int32, (n, n), 1) // seg
    return (r == c).astype(BF16)


def _seg_sum(x, ones):
    outs = []
    for j in range(x.shape[1] // LANE):
        xj = x[:, j * LANE:(j + 1) * LANE]
        hi = xj.astype(BF16)
        r1 = xj - hi.astype(F32)
        mid = r1.astype(BF16)
        lo = (r1 - mid.astype(F32)).astype(BF16)
        s = (jnp.dot(hi, ones, preferred_element_type=F32)
             + jnp.dot(mid, ones, preferred_element_type=F32)
             + jnp.dot(lo, ones, preferred_element_type=F32))
        outs.append(s)
    return jnp.concatenate(outs, axis=1) if len(outs) > 1 else outs[0]


def _s5_disc_kernel(lr_ref, li_ref, ldt_ref, bre_ref, bim_ref,
                    are_ref, aim_ref, bbre_ref, bbim_ref):
    lr = lr_ref[...]
    li = li_ref[...]
    dt = jnp.exp(ldt_ref[...])
    mag = jnp.exp(lr * dt)
    a_re = mag * jnp.cos(li * dt)
    a_im = mag * jnp.sin(li * dt)
    den = lr * lr + li * li
    nr = a_re - 1.0
    ni = a_im
    gam_re = (nr * lr + ni * li) / den
    gam_im = (ni * lr - nr * li) / den
    are_ref[...] = a_re
    aim_ref[...] = a_im
    for c in range(bre_ref.shape[0]):
        br = bre_ref[c]
        bi = bim_ref[c]
        bbre_ref[c] = gam_re * br - gam_im * bi
        bbim_ref[c] = gam_re * bi + gam_im * br


def s5_discretise(lam_re, lam_im, log_dt, b_re, b_im):
    g, p, c = b_re.shape
    ldt = jnp.broadcast_to(log_dt[:, None], (g, p))
    b_re_t = jnp.transpose(b_re, (2, 0, 1))
    b_im_t = jnp.transpose(b_im, (2, 0, 1))
    gp = jax.ShapeDtypeStruct((g, p), F32)
    cgp = jax.ShapeDtypeStruct((c, g, p), F32)
    return pl.pallas_call(
        _s5_disc_kernel,
        out_shape=(gp, gp, cgp, cgp),
        name="s5_discretise",
    )(lam_re, lam_im, ldt, b_re_t, b_im_t)


def _cmul(ar, ai, br, bi):
    return ar * br - ai * bi, ar * bi + ai * br


def _s5_scan_kernel(u_ref, bre_ref, bim_ref, cre_ref, cim_ref, are_ref, aim_ref, d_ref,
                    o_ref, sre_ref, sim_ref, bur_ref, bui_ref, *, tc):
    t_idx = pl.program_id(1)
    n_tiles = u_ref.shape[1] // LANE
    ml = bre_ref.shape[2]
    nblk = tc // SUBLANE

    @pl.when(t_idx == 0)
    def _():
        sre_ref[...] = jnp.zeros_like(sre_ref)
        sim_ref[...] = jnp.zeros_like(sim_ref)

    row = lax.broadcasted_iota(jnp.int32, (SUBLANE, ml), 0)

    for j in range(n_tiles):
        lanes = slice(j * ml, (j + 1) * ml)
        uj = u_ref[:, j * LANE:(j + 1) * LANE]
        ujb = uj.astype(BF16)
        bur_ref[...] = jnp.dot(ujb, bre_ref[j], preferred_element_type=F32)
        bui_ref[...] = jnp.dot(ujb, bim_ref[j], preferred_element_type=F32)

        a1r = jnp.broadcast_to(are_ref[:, lanes], (SUBLANE, ml))
        a1i = jnp.broadcast_to(aim_ref[:, lanes], (SUBLANE, ml))
        a2r, a2i = _cmul(a1r, a1i, a1r, a1i)
        a4r, a4i = _cmul(a2r, a2i, a2r, a2i)
        pr, pi = a1r, a1i
        cr, ci = a1r, a1i
        for r in range(1, SUBLANE):
            cr, ci = _cmul(cr, ci, a1r, a1i)
            pr = jnp.where(row >= r, cr, pr)
            pi = jnp.where(row >= r, ci, pi)
        zero = jnp.zeros((SUBLANE, ml), F32)
        coef = []
        for d, (xr, xi) in ((1, (a1r, a1i)), (2, (a2r, a2i)), (4, (a4r, a4i))):
            coef.append((d, jnp.where(row >= d, xr, zero), jnp.where(row >= d, xi, zero)))

        def blk(b, carry):
            cre, cim = carry
            r0 = pl.multiple_of(b * SUBLANE, SUBLANE)
            sr = bur_ref[pl.ds(r0, SUBLANE), :]
            si = bui_ref[pl.ds(r0, SUBLANE), :]
            for d, kr, ki in coef:
                rr = pltpu.roll(sr, d, axis=0)
                ri = pltpu.roll(si, d, axis=0)
                sr, si = sr + kr * rr - ki * ri, si + kr * ri + ki * rr
            cbr = jnp.broadcast_to(cre, (SUBLANE, ml))
            cbi = jnp.broadcast_to(cim, (SUBLANE, ml))
            sr = sr + pr * cbr - pi * cbi
            si = si + pr * cbi + pi * cbr
            bur_ref[pl.ds(r0, SUBLANE), :] = sr
            bui_ref[pl.ds(r0, SUBLANE), :] = si
            return sr[SUBLANE - 1:SUBLANE, :], si[SUBLANE - 1:SUBLANE, :]

        cre, cim = lax.fori_loop(0, nblk, blk, (sre_ref[:, lanes], sim_ref[:, lanes]))
        sre_ref[:, lanes] = cre
        sim_ref[:, lanes] = cim

        y = (jnp.dot(bur_ref[...].astype(BF16), cre_ref[j], preferred_element_type=F32)
             - jnp.dot(bui_ref[...].astype(BF16), cim_ref[j], preferred_element_type=F32))
        y = y + d_ref[:, j * LANE:(j + 1) * LANE] * uj
        o_ref[:, j * LANE:(j + 1) * LANE] = _gelu(y)


def s5_scan(p_main, bblk_re, bblk_im, cblk_re, cblk_im, a_re, a_im, d_skip, *, batch, tc):
    m = p_main.shape[0]
    t_len = m // batch
    n_tiles, _, ml = bblk_re.shape
    width = n_tiles * LANE
    modes = n_tiles * ml
    nt = t_len // tc
    assert t_len % tc == 0
    return pl.pallas_call(
        functools.partial(_s5_scan_kernel, tc=tc),
        grid=(batch, nt),
        in_specs=[
            pl.BlockSpec((tc, width), lambda b, t: (b * nt + t, 0)),
            pl.BlockSpec((n_tiles, LANE, ml), lambda b, t: (0, 0, 0)),
            pl.BlockSpec((n_tiles, LANE, ml), lambda b, t: (0, 0, 0)),
            pl.BlockSpec((n_tiles, ml, LANE), lambda b, t: (0, 0, 0)),
            pl.BlockSpec((n_tiles, ml, LANE), lambda b, t: (0, 0, 0)),
            pl.BlockSpec((1, modes), lambda b, t: (0, 0)),
            pl.BlockSpec((1, modes), lambda b, t: (0, 0)),
            pl.BlockSpec((1, width), lambda b, t: (0, 0)),
        ],
        out_specs=pl.BlockSpec((tc, width), lambda b, t: (b * nt + t, 0)),
        out_shape=jax.ShapeDtypeStruct((m, width), F32),
        scratch_shapes=[
            pltpu.VMEM((1, modes), F32), pltpu.VMEM((1, modes), F32),
            pltpu.VMEM((tc, ml), F32), pltpu.VMEM((tc, ml), F32),
        ],
        compiler_params=_cparams(("arbitrary", "arbitrary")),
        name="s5_scan",
    )(p_main, bblk_re, bblk_im, cblk_re, cblk_im, a_re, a_im, d_skip)


def _s5_glu_kernel(y_ref, yt_ref, w_ref, o_ref):
    z = jnp.dot(y_ref[...].astype(BF16), w_ref[...], preferred_element_type=F32)
    o_ref[...] = (yt_ref[...] * _sigmoid(z)).astype(o_ref.dtype)


def s5_glu(y, w, layer, *, tm, tn):
    m, d = y.shape
    return pl.pallas_call(
        _s5_glu_kernel,
        grid=(m // tm, d // tn),
        in_specs=[
            pl.BlockSpec((tm, d), lambda i, j: (i, 0)),
            pl.BlockSpec((tm, tn), lambda i, j: (i, j)),
            pl.BlockSpec((None, d, tn), lambda i, j: (layer, 0, j)),
        ],
        out_specs=pl.BlockSpec((tm, tn), lambda i, j: (i, j)),
        out_shape=jax.ShapeDtypeStruct((m, d), BF16),
        compiler_params=_cparams(("arbitrary", "arbitrary")),
        name="s5_glu",
    )(y, y, w)


def _shift_mix(z, prev_row, mu):
    row = lax.broadcasted_iota(jnp.int32, z.shape, 0)
    zs = jnp.where(row == 0, jnp.broadcast_to(prev_row, z.shape), pltpu.roll(z, 1, axis=0))
    return z + (zs - z) * mu


def _rwkv_prep_kernel(r_ref, k_ref, v_ref, lr_ref, rp_ref, kp_ref, vp_ref, lrp_ref,
                      mur_ref, muk_ref, muv_ref, mulr_ref,
                      w0_ref, w2_ref, a0_ref, a2_ref, g2_ref, kk_ref, ka_ref, rk_ref,
                      ro_ref, lw_ref, ko_ref, vo_ref, ao_ref, bo_ref, bon_ref, go_ref,
                      *, tiles_per_batch):
    i = pl.program_id(0)
    first = (i % tiles_per_batch) == 0
    last = SUBLANE - 1

    def prev(ref):
        return jnp.where(first, 0.0, ref[last:last + 1, :])

    r = _shift_mix(r_ref[...], prev(rp_ref), mur_ref[...])
    k = _shift_mix(k_ref[...], prev(kp_ref), muk_ref[...])
    v = _shift_mix(v_ref[...], prev(vp_ref), muv_ref[...])
    lr = _shift_mix(lr_ref[...], prev(lrp_ref), mulr_ref[...])

    wl = w0_ref[...] + jnp.dot(jnp.tanh(lr).astype(BF16), w2_ref[...].astype(BF16),
                               preferred_element_type=F32)
    w = -_softplus(-wl) - 0.5
    lw_ref[...] = -jnp.exp(w)
    a = _sigmoid(a0_ref[...] + jnp.dot(lr.astype(BF16), a2_ref[...].astype(BF16),
                                       preferred_element_type=F32))
    go_ref[...] = jnp.dot(_sigmoid(lr).astype(BF16), g2_ref[...].astype(BF16),
                          preferred_element_type=F32)

    ones = _block_ones(LANE, RWKV_HEAD)
    kk = k * kk_ref[...]
    kk = kk * lax.rsqrt(jnp.maximum(_seg_sum(kk * kk, ones), 1e-24))
    k = k * (1.0 + (a - 1.0) * ka_ref[...])
    ro_ref[...] = r
    ko_ref[...] = k
    vo_ref[...] = v
    ao_ref[...] = -kk
    bo_ref[...] = kk * a
    bon_ref[...] = _seg_sum(r * k * rk_ref[...], ones) * v


def rwkv_prep(p_main, lr, mu, w0, w2, a0, a2, g2, k_k, k_a, r_k, *, batch, tm):
    m = p_main.shape[0]
    t_len = m // batch
    hw = w0.shape[0]
    lrw = lr.shape[1]
    w_rank, a_rank, g_rank = w2.shape[0], a2.shape[0], g2.shape[0]
    nblk8 = tm // SUBLANE
    assert t_len % tm == 0 and p_main.shape[1] == 4 * hw

    def cur(c):
        return pl.BlockSpec((tm, hw), lambda i: (i, c))

    def prv(c):
        return pl.BlockSpec((SUBLANE, hw), lambda i: (jnp.maximum(i * nblk8 - 1, 0), c))

    row = lambda n: pl.BlockSpec((1, n), lambda i: (0, 0))
    mat = lambda a: pl.BlockSpec(a.shape, lambda i: (0, 0))
    used = w_rank + a_rank + g_rank
    mu_lr = jnp.pad(mu[3 * hw:], (0, lrw - used))
    w2 = jnp.pad(w2, ((0, lrw - w_rank), (0, 0)))
    a2 = jnp.pad(a2, ((w_rank, lrw - w_rank - a_rank), (0, 0)))
    g2 = jnp.pad(g2, ((w_rank + a_rank, lrw - used), (0, 0)))
    out = jax.ShapeDtypeStruct((m, hw), F32)
    return pl.pallas_call(
        functools.partial(_rwkv_prep_kernel, tiles_per_batch=t_len // tm),
        grid=(m // tm,),
        in_specs=[
            cur(1), cur(2), cur(3), pl.BlockSpec((tm, lrw), lambda i: (i, 0)),
            prv(1), prv(2), prv(3),
            pl.BlockSpec((SUBLANE, lrw), lambda i: (jnp.maximum(i * nblk8 - 1, 0), 0)),
            row(hw), row(hw), row(hw), row(lrw),
            row(hw), mat(w2), row(hw), mat(a2), mat(g2), row(hw), row(hw), row(hw),
        ],
        out_specs=[pl.BlockSpec((tm, hw), lambda i: (i, 0))] * 8,
        out_shape=(out,) * 8,
        compiler_params=_cparams(("arbitrary",)),
        name="rwkv_prep",
    )(p_main, p_main, p_main, lr, p_main, p_main, p_main, lr,
      mu[:hw].reshape(1, hw), mu[hw:2 * hw].reshape(1, hw), mu[2 * hw:3 * hw].reshape(1, hw),
      mu_lr.reshape(1, lrw),
      w0.reshape(1, hw), w2, a0.reshape(1, hw), a2, g2, k_k.reshape(1, hw),
      k_a.reshape(1, hw), r_k.reshape(1, hw))


def _mm(a, b):
    return jnp.dot(a.astype(BF16), b.astype(BF16), preferred_element_type=F32)


def _mm_nt(a, b):
    return lax.dot_general(a.astype(BF16), b.astype(BF16), (((1,), (1,)), ((), ())),
                           preferred_element_type=F32)


def _mm_tn(a, b):
    return lax.dot_general(a.astype(BF16), b.astype(BF16), (((0,), (0,)), ((), ())),
                           preferred_element_type=F32)


def _split3(x):
    hi = x.astype(BF16)
    r1 = x - hi.astype(F32)
    mid = r1.astype(BF16)
    lo = (r1 - mid.astype(F32)).astype(BF16)
    return hi, mid, lo


def _rwkv_scan_kernel(r_ref, lw_ref, k_ref, v_ref, a_ref, b_ref, y_ref,
                      z_ref, m_scr, zc_scr, ra_scr, yv_scr, ge_scr, *, n_chunks):
    c_len = RW_CHUNK
    n2 = 2 * c_len

    @pl.when(pl.program_id(2) == 0)
    def _():
        z_ref[...] = jnp.zeros_like(z_ref)

    lane = lax.broadcasted_iota(jnp.int32, (c_len, LANE), 1)
    head_a = lane < RWKV_HEAD
    ri = lax.broadcasted_iota(jnp.int32, (n2, n2), 0)
    ci = lax.broadcasted_iota(jnp.int32, (n2, n2), 1)
    same = (ri // c_len) == (ci // c_len)
    strict = same & ((ci % c_len) < (ri % c_len))
    incl = same & ((ci % c_len) <= (ri % c_len))
    eye = (ri == ci).astype(F32)
    rc = lax.broadcasted_iota(jnp.int32, (c_len, c_len), 0)
    cc = lax.broadcasted_iota(jnp.int32, (c_len, c_len), 1)
    tri = (cc <= rc).astype(BF16)

    def stack(x):
        return jnp.concatenate([jnp.where(head_a, x, 0.0), jnp.where(head_a, 0.0, x)], axis=0)

    def fold(x):
        return x[:c_len, :] + x[c_len:, :]

    chunks = range(n_chunks)
    pre = []
    for c in chunks:
        rows = slice(c * c_len, (c + 1) * c_len)
        r, lw, k, v = r_ref[rows, :], lw_ref[rows, :], k_ref[rows, :], v_ref[rows, :]
        a, b = a_ref[rows, :], b_ref[rows, :]
        cl = sum(jnp.dot(tri, part, preferred_element_type=F32) for part in _split3(lw))
        cl_end = cl[c_len - 1:c_len, :]
        g_inv = jnp.exp(-cl)
        g_tail = jnp.exp(cl_end - cl)
        r_t = r * jnp.exp(cl)
        ra_scr[c] = r_t
        ge_scr[c] = jnp.broadcast_to(jnp.exp(cl_end), (SUBLANE, LANE))
        pre.append(dict(
            a_s=stack(a * jnp.exp(cl - lw)).astype(BF16), r_s=stack(r_t).astype(BF16),
            b_s=stack(b * g_inv).astype(BF16), k_s=stack(k * g_inv).astype(BF16),
            v_s=stack(v).astype(BF16), bg_s=stack(b * g_tail).astype(BF16),
            kg_s=stack(k * g_tail).astype(BF16)))

    scs = [_mm_nt(jnp.concatenate([p["a_s"], p["r_s"]], axis=0),
                  jnp.concatenate([p["b_s"], p["k_s"]], axis=0)) for p in pre]
    l_ak = [jnp.where(strict, sc[:n2, n2:], 0.0).astype(BF16) for sc in scs]
    m_rb = [jnp.where(incl, sc[n2:, :n2], 0.0).astype(BF16) for sc in scs]
    m_rk = [jnp.where(incl, sc[n2:, n2:], 0.0).astype(BF16) for sc in scs]

    xs = [jnp.where(strict, sc[:n2, :n2], 0.0) for sc in scs]
    ts = [eye + x for x in xs]
    n = 2
    while n < c_len:
        xs = [_mm(x, x) for x in xs]
        ts = [t + _mm(t, x) for t, x in zip(ts, xs)]
        n *= 2

    lvs = [_mm(l, p["v_s"]) for l, p in zip(l_ak, pre)]
    tatv = [_mm(t, jnp.concatenate([p["a_s"], lv.astype(BF16)], axis=1)).astype(BF16)
            for t, p, lv in zip(ts, pre, lvs)]
    mtatv = [_mm(m, tv) for m, tv in zip(m_rb, tatv)]
    mkv = [_mm(m, p["v_s"]) for m, p in zip(m_rk, pre)]
    gz = [_mm_tn(tv, p["bg_s"]) for tv, p in zip(tatv, pre)]
    vk = [_mm_tn(p["v_s"], p["kg_s"]) for p in pre]
    for c in chunks:
        ra_scr[c] = ra_scr[c] + fold(mtatv[c][:, :LANE])
        yv_scr[c] = fold(mtatv[c][:, LANE:] + mkv[c])
        m_scr[c] = gz[c][:LANE, :]
        zc_scr[c] = gz[c][LANE:, :] + vk[c]

    zt = z_ref[...]
    for c in range(n_chunks):
        rows = slice(c * c_len, (c + 1) * c_len)
        y_ref[rows, :] = _mm_nt(ra_scr[c], zt) + yv_scr[c]
        zt = zt * ge_scr[c][0:1, :] + _mm(zt, m_scr[c]) + zc_scr[c]
    z_ref[...] = zt


def rwkv_scan(r, lw, k, v, a, b, *, batch, tb):
    m, hw = r.shape
    t_len = m // batch
    nt = t_len // tb
    nc = tb // RW_CHUNK
    assert t_len % tb == 0 and tb % RW_CHUNK == 0 and hw % LANE == 0
    spec = pl.BlockSpec((tb, LANE), lambda bi, p, t: (bi * nt + t, p))
    return pl.pallas_call(
        functools.partial(_rwkv_scan_kernel, n_chunks=nc),
        grid=(batch, hw // LANE, nt),
        in_specs=[spec] * 6,
        out_specs=spec,
        out_shape=jax.ShapeDtypeStruct((m, hw), F32),
        scratch_shapes=[
            pltpu.VMEM((LANE, LANE), F32),
            pltpu.VMEM((nc, LANE, LANE), F32), pltpu.VMEM((nc, LANE, LANE), F32),
            pltpu.VMEM((nc, RW_CHUNK, LANE), F32), pltpu.VMEM((nc, RW_CHUNK, LANE), F32),
            pltpu.VMEM((nc, SUBLANE, LANE), F32),
        ],
        compiler_params=_cparams(("arbitrary", "arbitrary", "arbitrary")),
        name="rwkv_scan",
    )(r, lw, k, v, a, b)


def _rwkv_post_kernel(y_ref, bon_ref, g_ref, lw_ref, lb_ref, o_ref):
    ones = _block_ones(LANE, RWKV_HEAD)
    y = y_ref[...]
    inv_n = 1.0 / RWKV_HEAD
    mu = _seg_sum(y, ones) * inv_n
    yc = y - mu
    var = _seg_sum(yc * yc, ones) * inv_n
    yn = yc * lax.rsqrt(var + GN_EPS) * lw_ref[...] + lb_ref[...]
    o_ref[...] = ((yn + bon_ref[...]) * g_ref[...]).astype(o_ref.dtype)


def rwkv_post(y, bonus, g, lnx_w, lnx_b, *, tm):
    m, hw = y.shape
    spec = pl.BlockSpec((tm, hw), lambda i: (i, 0))
    row = pl.BlockSpec((1, hw), lambda i: (0, 0))
    return pl.pallas_call(
        _rwkv_post_kernel,
        grid=(m // tm,),
        in_specs=[spec, spec, spec, row, row],
        out_specs=spec,
        out_shape=jax.ShapeDtypeStruct((m, hw), BF16),
        compiler_params=_cparams(("arbitrary",)),
        name="rwkv_post",
    )(y, bonus, g, lnx_w.reshape(1, hw), lnx_b.reshape(1, hw))


def _lru_kernel(gate_ref, xb_ref, cw_ref, cb_ref, wr_ref, br_ref, wi_ref, bi_ref, lam_ref,
                o_ref, halo_ref, h_ref, a_scr, b_scr, *, tc):
    t_idx = pl.program_id(1)
    width = xb_ref.shape[1]
    nblk = tc // SUBLANE
    n_gate_blocks = width // LRU_BLOCK

    @pl.when(t_idx == 0)
    def _():
        halo_ref[...] = jnp.zeros_like(halo_ref)
        h_ref[...] = jnp.zeros_like(h_ref)

    xb = xb_ref[...]
    ext = jnp.concatenate([halo_ref[...], xb], axis=0)
    xc = xb * cw_ref[CONV_WIDTH - 1:CONV_WIDTH, :] + cb_ref[...]
    for d in range(1, CONV_WIDTH):
        sh = pltpu.roll(ext, d, axis=0)[SUBLANE:, :]
        xc = xc + sh * cw_ref[CONV_WIDTH - 1 - d:CONV_WIDTH - d, :]
    halo_ref[...] = xb[tc - SUBLANE:, :]

    sp = _softplus(-lam_ref[...])
    row = lax.broadcasted_iota(jnp.int32, (SUBLANE, LRU_BLOCK), 0)

    for n in range(n_gate_blocks):
        lanes = slice(n * LRU_BLOCK, (n + 1) * LRU_BLOCK)
        xn = xc[:, lanes]
        xnb = xn.astype(BF16)
        gr = jnp.dot(xnb, wr_ref[n].astype(BF16), preferred_element_type=F32) + br_ref[:, lanes]
        gi = jnp.dot(xnb, wi_ref[n].astype(BF16), preferred_element_type=F32) + bi_ref[:, lanes]
        log_a = -LRU_C * _sigmoid(gr) * sp[:, lanes]
        a = jnp.exp(log_a)
        mult = jnp.sqrt(-jnp.tanh(log_a) * (a * a + 1.0))
        a_scr[...] = a
        b_scr[...] = mult * _sigmoid(gi) * xn

        def blk(bidx, carry):
            r0 = pl.multiple_of(bidx * SUBLANE, SUBLANE)
            av = a_scr[pl.ds(r0, SUBLANE), :]
            bv = b_scr[pl.ds(r0, SUBLANE), :]
            for d in (1, 2, 4):
                keep = row >= d
                ash = jnp.where(keep, pltpu.roll(av, d, axis=0), 1.0)
                bsh = jnp.where(keep, pltpu.roll(bv, d, axis=0), 0.0)
                bv = bv + av * bsh
                av = av * ash
            hv = bv + av * jnp.broadcast_to(carry, (SUBLANE, LRU_BLOCK))
            b_scr[pl.ds(r0, SUBLANE), :] = hv
            return hv[SUBLANE - 1:SUBLANE, :]

        h_last = lax.fori_loop(0, nblk, blk, h_ref[:, lanes])
        h_ref[:, lanes] = h_last
        o_ref[:, lanes] = (b_scr[...] * gate_ref[:, lanes]).astype(o_ref.dtype)


def lru_block(p_odd, conv_w, conv_b, w_r, b_r, w_i, b_i, lam, layer, *, batch, tc):
    m = p_odd.shape[0]
    width = lam.shape[0]
    t_len = m // batch
    nt = t_len // tc
    assert t_len % tc == 0 and p_odd.shape[1] == 2 * width
    row = pl.BlockSpec((1, width), lambda b, t: (0, 0))
    blkw = pl.BlockSpec((None,) + w_r.shape[1:], lambda b, t: (layer, 0, 0, 0))
    return pl.pallas_call(
        functools.partial(_lru_kernel, tc=tc),
        grid=(batch, nt),
        in_specs=[
            pl.BlockSpec((tc, width), lambda b, t: (b * nt + t, 0)),
            pl.BlockSpec((tc, width), lambda b, t: (b * nt + t, 1)),
            pl.BlockSpec((CONV_WIDTH, width), lambda b, t: (0, 0)),
            row, blkw, row, blkw, row, row,
        ],
        out_specs=pl.BlockSpec((tc, width), lambda b, t: (b * nt + t, 0)),
        out_shape=jax.ShapeDtypeStruct((m, width), BF16),
        scratch_shapes=[
            pltpu.VMEM((SUBLANE, width), F32), pltpu.VMEM((1, width), F32),
            pltpu.VMEM((tc, LRU_BLOCK), F32), pltpu.VMEM((tc, LRU_BLOCK), F32),
        ],
        compiler_params=_cparams(("arbitrary", "arbitrary")),
        name="lru_block",
    )(p_odd, p_odd, conv_w, conv_b.reshape(1, width), w_r, b_r.reshape(1, width),
      w_i, b_i.reshape(1, width), lam.reshape(1, width))


def _s5_block_weights(bb_re, bb_im, c_re, c_im):
    c, g, p = bb_re.shape
    gpt = LANE // c
    tiles = g // gpt
    eye = jnp.eye(gpt, dtype=F32)

    def b_blk(bb):
        bb = bb.reshape(c, tiles, gpt, p)
        return jnp.einsum('cjgp,gh->jgchp', bb, eye).reshape(tiles, gpt * c, gpt * p)

    def c_blk(cm):
        cm = cm.reshape(tiles, gpt, c, p)
        return jnp.einsum('jgcp,gh->jhpgc', cm, eye).reshape(tiles, gpt * p, gpt * c)

    return (b_blk(bb_re).astype(BF16), b_blk(bb_im).astype(BF16),
            c_blk(c_re).astype(BF16), c_blk(c_im).astype(BF16))


def _even_mixer(x, g_pre, g_post, batch, idx, w_in, shift_mu, lam_re, lam_im, log_dt, b_re, b_im,
                c_re, c_im, d_skip, w_glu, w0, w2, a0, a2, g2, k_k, k_a, r_k, lnx_w, lnx_b,
                w_out):
    hw = w0.shape[0]
    s5w = d_skip.shape[0]
    n_main = s5w + 3 * hw
    p_main = norm_matmul(x, g_pre, w_in, idx, n_main, tm=1024, tn=512, name="even_in_proj")
    lrw = w_in.shape[2] - n_main
    lr_pad = -(-lrw // LANE) * LANE
    w_lr = jnp.pad(w_in[idx, :, n_main:], ((0, 0), (0, lr_pad - lrw)))[None]
    lr = norm_matmul(x, g_pre, w_lr, 0, lr_pad, tm=1024, tn=lr_pad, name="even_lr_proj")

    a_re, a_im, bb_re, bb_im = s5_discretise(lam_re, lam_im, log_dt, b_re, b_im)
    bblk_re, bblk_im, cblk_re, cblk_im = _s5_block_weights(bb_re, bb_im, c_re, c_im)
    y_s5 = s5_scan(p_main, bblk_re, bblk_im, cblk_re, cblk_im,
                   a_re.reshape(1, -1), a_im.reshape(1, -1), d_skip.reshape(1, -1),
                   batch=batch, tc=256)
    y_s5 = s5_glu(y_s5, w_glu, idx, tm=1024, tn=512)

    r, lw, k, v, a, b, bonus, g = rwkv_prep(p_main, lr, shift_mu, w0, w2, a0, a2, g2,
                                            k_k, k_a, r_k.reshape(-1), batch=batch, tm=256)
    y_rw = rwkv_scan(r, lw, k, v, a, b, batch=batch, tb=512)
    y_rw = rwkv_post(y_rw, bonus, g, lnx_w, lnx_b, tm=512)

    return out_proj_norm_residual([y_s5, y_rw], w_out, idx, x, g_post, tm=512,
                                  name="even_out_proj")


def _odd_mixer(x, g_pre, g_post, batch, idx, w_in, conv_w, conv_b, w_r, b_r, w_i, b_i, lam, w_out):
    width = lam.shape[0]
    p_odd = norm_matmul(x, g_pre, w_in, idx, 2 * width, tm=1024, tn=512,
                        n_gelu_tiles=width // 512, name="odd_in_proj")
    hg = lru_block(p_odd, conv_w, conv_b, w_r, b_r, w_i, b_i, lam, idx, batch=batch, tc=256)
    return out_proj_norm_residual([hg], w_out, idx, x, g_post, tm=512, name="odd_out_proj")


def _ffn(x, g_pre, g_post, layer, w_gate, w_up, w_down):
    act = norm_swiglu_up(x, g_pre, w_gate, w_up, layer, tm=1024, tn=512)
    return matmul_norm_residual(act, w_down, layer, x, g_post, tm=1024, tk=512, name="ffn_down")


def kernel(x, ev_w_in, ev_shift_mu, s5_lam_re, s5_lam_im, s5_log_dt, s5_b_re, s5_b_im, s5_c_re, s5_c_im, s5_d, s5_w_glu, rw_w0, rw_w2, rw_a0, rw_a2, rw_g2, rw_k_k, rw_k_a, rw_r_k, rw_lnx_w, rw_lnx_b, ev_w_out, od_w_in, od_conv_w, od_conv_b, lru_w_r, lru_b_r, lru_w_i, lru_b_i, lru_lam, od_w_out, ffn_w_gate, ffn_w_up, ffn_w_down, norm_mix_pre, norm_mix_post, norm_ffn_pre, norm_ffn_post):
    batch, t_len, d = x.shape
    depth = ffn_w_gate.shape[0]
    xf = x.reshape(batch * t_len, d)
    (ev_w_in, s5_w_glu, ev_w_out, od_w_in, od_w_out, ffn_w_gate, ffn_w_up, ffn_w_down) = (
        w.astype(BF16) for w in (ev_w_in, s5_w_glu, ev_w_out, od_w_in, od_w_out,
                                 ffn_w_gate, ffn_w_up, ffn_w_down))
    for layer in range(depth):
        i = layer // 2
        if layer % 2 == 0:
            xf = _even_mixer(xf, norm_mix_pre[layer], norm_mix_post[layer], batch, i,
                             ev_w_in, ev_shift_mu[i], s5_lam_re[i], s5_lam_im[i],
                             s5_log_dt[i], s5_b_re[i], s5_b_im[i], s5_c_re[i], s5_c_im[i],
                             s5_d[i], s5_w_glu, rw_w0[i], rw_w2[i], rw_a0[i], rw_a2[i],
                             rw_g2[i], rw_k_k[i], rw_k_a[i], rw_r_k[i], rw_lnx_w[i],
                             rw_lnx_b[i], ev_w_out)
        else:
            xf = _odd_mixer(xf, norm_mix_pre[layer], norm_mix_post[layer], batch, i,
                            od_w_in, od_conv_w[i], od_conv_b[i], lru_w_r, lru_b_r[i],
                            lru_w_i, lru_b_i[i], lru_lam[i], od_w_out)
        xf = _ffn(xf, norm_ffn_pre[layer], norm_ffn_post[layer], layer,
                  ffn_w_gate, ffn_w_up, ffn_w_down)
    return xf.reshape(batch, t_len, d)
```

```python
import functools
import math

import jax
import jax.numpy as jnp
from jax import lax
from jax.experimental import pallas as pl
from jax.experimental.pallas import tpu as pltpu

F32 = jnp.float32
BF16 = jnp.bfloat16

NORM_EPS = 1e-6
GN_EPS = 64e-5
LRU_C = 8.0
S5_GROUP = 16
S5_STATE = 64
RWKV_HEAD = 64
CONV_WIDTH = 4
LRU_BLOCK = 256

LANE = 128
SUBLANE = 8
VMEM_LIMIT = 56 * 1024 * 1024

RW_CHUNK = 64
RW_PAIRS = 2
S5_SCAN_LANES = 1024
LRU_SCAN_LANES = 512


def _cparams(sem):
    return pltpu.CompilerParams(dimension_semantics=sem, vmem_limit_bytes=VMEM_LIMIT)


def _gelu(x):
    c = math.sqrt(2.0 / math.pi)
    return 0.5 * x * (1.0 + jnp.tanh(c * (x + 0.044715 * (x * x * x))))


def _sigmoid(x):
    return 1.0 / (1.0 + jnp.exp(-x))


def _softplus(x):
    return jnp.maximum(x, 0.0) + jnp.log(1.0 + jnp.exp(-jnp.abs(x)))


def _rms_rows(y, g):
    ms = jnp.mean(y * y, axis=-1, keepdims=True)
    return y * lax.rsqrt(ms + NORM_EPS) * g


def _norm_mm_kernel(x_ref, g_ref, w_ref, o_ref, h_ref, *, n_gelu_tiles):
    j = pl.program_id(1)

    @pl.when(j == 0)
    def _():
        h_ref[...] = _rms_rows(x_ref[...], g_ref[...]).astype(BF16)

    acc = jnp.dot(h_ref[...], w_ref[...], preferred_element_type=F32)
    if n_gelu_tiles == 0:
        o_ref[...] = acc.astype(o_ref.dtype)
    else:
        @pl.when(j < n_gelu_tiles)
        def _():
            o_ref[...] = _gelu(acc).astype(o_ref.dtype)

        @pl.when(j >= n_gelu_tiles)
        def _():
            o_ref[...] = acc.astype(o_ref.dtype)


def norm_matmul(x, g, w, layer, n_out, *, tm, tn, out_dtype, n_gelu_tiles=0, name):
    m, d = x.shape
    assert m % tm == 0 and n_out % tn == 0 and w.shape[1] == d
    return pl.pallas_call(
        functools.partial(_norm_mm_kernel, n_gelu_tiles=n_gelu_tiles),
        grid=(m // tm, n_out // tn),
        in_specs=[
            pl.BlockSpec((tm, d), lambda i, j: (i, 0)),
            pl.BlockSpec((1, d), lambda i, j: (0, 0)),
            pl.BlockSpec((None, d, tn), lambda i, j: (layer, 0, j)),
        ],
        out_specs=pl.BlockSpec((tm, tn), lambda i, j: (i, j)),
        out_shape=jax.ShapeDtypeStruct((m, n_out), out_dtype),
        scratch_shapes=[pltpu.VMEM((tm, d), BF16)],
        compiler_params=_cparams(("arbitrary", "arbitrary")),
        name=name,
    )(x, g.reshape(1, d), w)


def _norm_swiglu_kernel(x_ref, g_ref, wg_ref, wu_ref, o_ref, h_ref):
    j = pl.program_id(1)

    @pl.when(j == 0)
    def _():
        h_ref[...] = _rms_rows(x_ref[...], g_ref[...]).astype(BF16)

    h = h_ref[...]
    gate = jnp.dot(h, wg_ref[...].astype(BF16), preferred_element_type=F32)
    up = jnp.dot(h, wu_ref[...].astype(BF16), preferred_element_type=F32)
    o_ref[...] = (gate * _sigmoid(gate) * up).astype(o_ref.dtype)


def norm_swiglu_up(x, g, w_gate, w_up, layer, *, tm, tn):
    m, d = x.shape
    n = w_gate.shape[2]
    assert m % tm == 0 and n % tn == 0
    return pl.pallas_call(
        _norm_swiglu_kernel,
        grid=(m // tm, n // tn),
        in_specs=[
            pl.BlockSpec((tm, d), lambda i, j: (i, 0)),
            pl.BlockSpec((1, d), lambda i, j: (0, 0)),
            pl.BlockSpec((None, d, tn), lambda i, j: (layer, 0, j)),
            pl.BlockSpec((None, d, tn), lambda i, j: (layer, 0, j)),
        ],
        out_specs=pl.BlockSpec((tm, tn), lambda i, j: (i, j)),
        out_shape=jax.ShapeDtypeStruct((m, n), BF16),
        scratch_shapes=[pltpu.VMEM((tm, d), BF16)],
        compiler_params=_cparams(("arbitrary", "arbitrary")),
        name="ffn_up",
    )(x, g.reshape(1, d), w_gate, w_up)


def _mm_norm_res_kernel(a_ref, w_ref, x_ref, g_ref, o_ref):
    k = pl.program_id(1)

    @pl.when(k == 0)
    def _():
        o_ref[...] = jnp.zeros_like(o_ref)

    o_ref[...] += jnp.dot(a_ref[...], w_ref[...].astype(BF16), preferred_element_type=F32)

    @pl.when(k == pl.num_programs(1) - 1)
    def _():
        o_ref[...] = x_ref[...] + _rms_rows(o_ref[...], g_ref[...])


def matmul_norm_residual(a, w, layer, x, g, *, tm, tk, name):
    m, kdim = a.shape
    d = w.shape[2]
    assert m % tm == 0 and kdim % tk == 0 and w.shape[1] == kdim
    return pl.pallas_call(
        _mm_norm_res_kernel,
        grid=(m // tm, kdim // tk),
        in_specs=[
            pl.BlockSpec((tm, tk), lambda i, k: (i, k)),
            pl.BlockSpec((None, tk, d), lambda i, k: (layer, k, 0)),
            pl.BlockSpec((tm, d), lambda i, k: (i, 0)),
            pl.BlockSpec((1, d), lambda i, k: (0, 0)),
        ],
        out_specs=pl.BlockSpec((tm, d), lambda i, k: (i, 0)),
        out_shape=jax.ShapeDtypeStruct((m, d), F32),
        compiler_params=_cparams(("arbitrary", "arbitrary")),
        name=name,
    )(a, w, x, g.reshape(1, d))


def _out_proj_kernel(*refs, n_lhs):
    a_refs = refs[:n_lhs]
    w_ref, x_ref, g_ref, o_ref = refs[n_lhs:]
    y = None
    k0 = 0
    for a_ref in a_refs:
        kw = a_ref.shape[1]
        part = jnp.dot(a_ref[...], w_ref[k0:k0 + kw, :], preferred_element_type=F32)
        y = part if y is None else y + part
        k0 += kw
    o_ref[...] = x_ref[...] + _rms_rows(y, g_ref[...])


def out_proj_norm_residual(lhs, w, layer, x, g, *, tm, name):
    m = x.shape[0]
    kdim, d = w.shape[1], w.shape[2]
    assert sum(a.shape[1] for a in lhs) == kdim and m % tm == 0
    return pl.pallas_call(
        functools.partial(_out_proj_kernel, n_lhs=len(lhs)),
        grid=(m // tm,),
        in_specs=[pl.BlockSpec((tm, a.shape[1]), lambda i: (i, 0)) for a in lhs] + [
            pl.BlockSpec((None, kdim, d), lambda i: (layer, 0, 0)),
            pl.BlockSpec((tm, d), lambda i: (i, 0)),
            pl.BlockSpec((1, d), lambda i: (0, 0)),
        ],
        out_specs=pl.BlockSpec((tm, d), lambda i: (i, 0)),
        out_shape=jax.ShapeDtypeStruct((m, d), F32),
        compiler_params=_cparams(("arbitrary",)),
        name=name,
    )(*lhs, w, x, g.reshape(1, d))


def _block_ones(n, seg):
    r = lax.broadcasted_iota(jnp.int32, (n, n), 0) // seg
    c = lax.broadcasted_iota(jnp.int32, (n, n), 1) // seg
    return (r == c).astype(BF16)


def _seg_sum(x, ones):
    outs = []
    for j in range(x.shape[1] // LANE):
        xj = x[:, j * LANE:(j + 1) * LANE]
        hi = xj.astype(BF16)
        r1 = xj - hi.astype(F32)
        mid = r1.astype(BF16)
        lo = (r1 - mid.astype(F32)).astype(BF16)
        s = (jnp.dot(hi, ones, preferred_element_type=F32)
             + jnp.dot(mid, ones, preferred_element_type=F32)
             + jnp.dot(lo, ones, preferred_element_type=F32))
        outs.append(s)
    return jnp.concatenate(outs, axis=1) if len(outs) > 1 else outs[0]


def _s5_disc_kernel(lr_ref, li_ref, ldt_ref, bre_ref, bim_ref,
                    are_ref, aim_ref, bbre_ref, bbim_ref):
    lr = lr_ref[...]
    li = li_ref[...]
    dt = jnp.exp(ldt_ref[...])
    mag = jnp.exp(lr * dt)
    a_re = mag * jnp.cos(li * dt)
    a_im = mag * jnp.sin(li * dt)
    den = lr * lr + li * li
    nr = a_re - 1.0
    ni = a_im
    gam_re = (nr * lr + ni * li) / den
    gam_im = (ni * lr - nr * li) / den
    are_ref[...] = a_re
    aim_ref[...] = a_im
    for c in range(bre_ref.shape[0]):
        br = bre_ref[c]
        bi = bim_ref[c]
        bbre_ref[c] = gam_re * br - gam_im * bi
        bbim_ref[c] = gam_re * bi + gam_im * br


def s5_discretise(lam_re, lam_im, log_dt, b_re, b_im):
    g, p, c = b_re.shape
    ldt = jnp.broadcast_to(log_dt[:, None], (g, p))
    b_re_t = jnp.transpose(b_re, (2, 0, 1))
    b_im_t = jnp.transpose(b_im, (2, 0, 1))
    gp = jax.ShapeDtypeStruct((g, p), F32)
    cgp = jax.ShapeDtypeStruct((c, g, p), F32)
    return pl.pallas_call(
        _s5_disc_kernel,
        out_shape=(gp, gp, cgp, cgp),
        name="s5_discretise",
    )(lam_re, lam_im, ldt, b_re_t, b_im_t)


def _cmul(ar, ai, br, bi):
    return ar * br - ai * bi, ar * bi + ai * br


def _s5_scan_kernel(u_ref, bre_ref, bim_ref, cre_ref, cim_ref, are_ref, aim_ref, d_ref,
                    o_ref, car_ref, cai_ref, bur_ref, bui_ref, tab_ref, *, tc):
    t_idx = pl.program_id(1)
    n_tiles = u_ref.shape[1] // LANE
    ml = bre_ref.shape[2]
    modes = n_tiles * ml
    nblk = tc // SUBLANE
    grp = S5_SCAN_LANES
    row = lax.broadcasted_iota(jnp.int32, (SUBLANE, grp), 0)

    @pl.when(t_idx == 0)
    def _():
        car_ref[...] = jnp.zeros_like(car_ref)
        cai_ref[...] = jnp.zeros_like(cai_ref)
        for g in range(modes // grp):
            lanes = slice(g * grp, (g + 1) * grp)
            a1r = jnp.broadcast_to(are_ref[:, lanes], (SUBLANE, grp))
            a1i = jnp.broadcast_to(aim_ref[:, lanes], (SUBLANE, grp))
            a2r, a2i = _cmul(a1r, a1i, a1r, a1i)
            a4r, a4i = _cmul(a2r, a2i, a2r, a2i)
            pr, pi = a1r, a1i
            cr, ci = a1r, a1i
            for r in range(1, SUBLANE):
                cr, ci = _cmul(cr, ci, a1r, a1i)
                pr = jnp.where(row >= r, cr, pr)
                pi = jnp.where(row >= r, ci, pi)
            for lvl, (d, xr, xi) in enumerate(((1, a1r, a1i), (2, a2r, a2i), (4, a4r, a4i))):
                tab_ref[2 * lvl, :, lanes] = jnp.where(row >= d, xr, 0.0)
                tab_ref[2 * lvl + 1, :, lanes] = jnp.where(row >= d, xi, 0.0)
            tab_ref[6, :, lanes] = pr
            tab_ref[7, :, lanes] = pi

    for j in range(n_tiles):
        uj = u_ref[:, j * LANE:(j + 1) * LANE]
        bur_ref[:, j * ml:(j + 1) * ml] = jnp.dot(uj, bre_ref[j], preferred_element_type=F32)
        bui_ref[:, j * ml:(j + 1) * ml] = jnp.dot(uj, bim_ref[j], preferred_element_type=F32)

    def blk(b, carry):
        rows = pl.ds(pl.multiple_of(b * SUBLANE, SUBLANE), SUBLANE)
        for g in range(modes // grp):
            lanes = slice(g * grp, (g + 1) * grp)
            sr = bur_ref[rows, lanes]
            si = bui_ref[rows, lanes]
            for lvl, d in enumerate((1, 2, 4)):
                kr = tab_ref[2 * lvl, :, lanes]
                ki = tab_ref[2 * lvl + 1, :, lanes]
                rr = pltpu.roll(sr, d, axis=0)
                ri = pltpu.roll(si, d, axis=0)
                sr, si = sr + kr * rr - ki * ri, si + kr * ri + ki * rr
            pr = tab_ref[6, :, lanes]
            pi = tab_ref[7, :, lanes]
            cbr = car_ref[:, lanes]
            cbi = cai_ref[:, lanes]
            sr = sr + pr * cbr - pi * cbi
            si = si + pr * cbi + pi * cbr
            bur_ref[rows, lanes] = sr
            bui_ref[rows, lanes] = si
            car_ref[:, lanes] = jnp.broadcast_to(sr[SUBLANE - 1:SUBLANE, :], (SUBLANE, grp))
            cai_ref[:, lanes] = jnp.broadcast_to(si[SUBLANE - 1:SUBLANE, :], (SUBLANE, grp))
        return carry

    lax.fori_loop(0, nblk, blk, 0)

    for j in range(n_tiles):
        lanes = slice(j * ml, (j + 1) * ml)
        y = (jnp.dot(bur_ref[:, lanes].astype(BF16), cre_ref[j], preferred_element_type=F32)
             - jnp.dot(bui_ref[:, lanes].astype(BF16), cim_ref[j], preferred_element_type=F32))
        uj = u_ref[:, j * LANE:(j + 1) * LANE].astype(F32)
        y = y + d_ref[:, j * LANE:(j + 1) * LANE] * uj
        o_ref[:, j * LANE:(j + 1) * LANE] = _gelu(y)


def s5_scan(p_main, bblk_re, bblk_im, cblk_re, cblk_im, a_re, a_im, d_skip, *, batch, tc):
    m = p_main.shape[0]
    t_len = m // batch
    n_tiles, _, ml = bblk_re.shape
    width = n_tiles * LANE
    modes = n_tiles * ml
    nt = t_len // tc
    assert t_len % tc == 0 and modes % S5_SCAN_LANES == 0 and p_main.dtype == BF16
    return pl.pallas_call(
        functools.partial(_s5_scan_kernel, tc=tc),
        grid=(batch, nt),
        in_specs=[
            pl.BlockSpec((tc, width), lambda b, t: (b * nt + t, 0)),
            pl.BlockSpec((n_tiles, LANE, ml), lambda b, t: (0, 0, 0)),
            pl.BlockSpec((n_tiles, LANE, ml), lambda b, t: (0, 0, 0)),
            pl.BlockSpec((n_tiles, ml, LANE), lambda b, t: (0, 0, 0)),
            pl.BlockSpec((n_tiles, ml, LANE), lambda b, t: (0, 0, 0)),
            pl.BlockSpec((1, modes), lambda b, t: (0, 0)),
            pl.BlockSpec((1, modes), lambda b, t: (0, 0)),
            pl.BlockSpec((1, width), lambda b, t: (0, 0)),
        ],
        out_specs=pl.BlockSpec((tc, width), lambda b, t: (b * nt + t, 0)),
        out_shape=jax.ShapeDtypeStruct((m, width), F32),
        scratch_shapes=[
            pltpu.VMEM((SUBLANE, modes), F32), pltpu.VMEM((SUBLANE, modes), F32),
            pltpu.VMEM((tc, modes), F32), pltpu.VMEM((tc, modes), F32),
            pltpu.VMEM((8, SUBLANE, modes), F32),
        ],
        compiler_params=_cparams(("arbitrary", "arbitrary")),
        name="s5_scan",
    )(p_main, bblk_re, bblk_im, cblk_re, cblk_im, a_re, a_im, d_skip)


def _s5_glu_kernel(y_ref, yt_ref, w_ref, o_ref):
    z = jnp.dot(y_ref[...].astype(BF16), w_ref[...], preferred_element_type=F32)
    o_ref[...] = (yt_ref[...] * _sigmoid(z)).astype(o_ref.dtype)


def s5_glu(y, w, layer, *, tm, tn):
    m, d = y.shape
    return pl.pallas_call(
        _s5_glu_kernel,
        grid=(m // tm, d // tn),
        in_specs=[
            pl.BlockSpec((tm, d), lambda i, j: (i, 0)),
            pl.BlockSpec((tm, tn), lambda i, j: (i, j)),
            pl.BlockSpec((None, d, tn), lambda i, j: (layer, 0, j)),
        ],
        out_specs=pl.BlockSpec((tm, tn), lambda i, j: (i, j)),
        out_shape=jax.ShapeDtypeStruct((m, d), BF16),
        compiler_params=_cparams(("arbitrary", "arbitrary")),
        name="s5_glu",
    )(y, y, w)


def _shift_mix(z, prev_row, mu):
    row = lax.broadcasted_iota(jnp.int32, z.shape, 0)
    zs = jnp.where(row == 0, jnp.broadcast_to(prev_row, z.shape), pltpu.roll(z, 1, axis=0))
    return z + (zs - z) * mu


def _rwkv_prep_kernel(r_ref, k_ref, v_ref, lr_ref, rp_ref, kp_ref, vp_ref, lrp_ref,
                      mur_ref, muk_ref, muv_ref, mulr_ref,
                      w0_ref, w2_ref, a0_ref, a2_ref, g2_ref, kk_ref, ka_ref, rk_ref,
                      ro_ref, lw_ref, ko_ref, vo_ref, ao_ref, bo_ref, bon_ref, go_ref,
                      *, tiles_per_batch):
    i = pl.program_id(0)
    first = (i % tiles_per_batch) == 0

    def prev(ref):
        last = ref.shape[0] - 1
        return jnp.where(first, 0.0, ref[last:last + 1, :].astype(F32))

    r = _shift_mix(r_ref[...].astype(F32), prev(rp_ref), mur_ref[...])
    k = _shift_mix(k_ref[...].astype(F32), prev(kp_ref), muk_ref[...])
    v = _shift_mix(v_ref[...].astype(F32), prev(vp_ref), muv_ref[...])
    lr = _shift_mix(lr_ref[...], prev(lrp_ref), mulr_ref[...])

    wl = w0_ref[...] + jnp.dot(jnp.tanh(lr).astype(BF16), w2_ref[...].astype(BF16),
                               preferred_element_type=F32)
    w = -_softplus(-wl) - 0.5
    lw_ref[...] = -jnp.exp(w)
    a = _sigmoid(a0_ref[...] + jnp.dot(lr.astype(BF16), a2_ref[...].astype(BF16),
                                       preferred_element_type=F32))
    go_ref[...] = jnp.dot(_sigmoid(lr).astype(BF16), g2_ref[...].astype(BF16),
                          preferred_element_type=F32)

    ones = _block_ones(LANE, RWKV_HEAD)
    kk = k * kk_ref[...]
    kk = kk * lax.rsqrt(jnp.maximum(_seg_sum(kk * kk, ones), 1e-24))
    k = k * (1.0 + (a - 1.0) * ka_ref[...])
    ro_ref[...] = r
    ko_ref[...] = k
    vo_ref[...] = v
    ao_ref[...] = -kk
    bo_ref[...] = kk * a
    bon_ref[...] = _seg_sum(r * k * rk_ref[...], ones) * v


def rwkv_prep(p_main, lr, mu, w0, w2, a0, a2, g2, k_k, k_a, r_k, *, batch, tm):
    m = p_main.shape[0]
    t_len = m // batch
    hw = w0.shape[0]
    lrw = lr.shape[1]
    w_rank, a_rank, g_rank = w2.shape[0], a2.shape[0], g2.shape[0]
    nblk8 = tm // SUBLANE
    halo = 2 * SUBLANE
    nblk16 = tm // halo
    assert t_len % tm == 0 and p_main.shape[1] == 4 * hw and p_main.dtype == BF16

    def cur(c):
        return pl.BlockSpec((tm, hw), lambda i: (i, c))

    def prv(c):
        return pl.BlockSpec((halo, hw), lambda i: (jnp.maximum(i * nblk16 - 1, 0), c))

    row = lambda n: pl.BlockSpec((1, n), lambda i: (0, 0))
    mat = lambda a: pl.BlockSpec(a.shape, lambda i: (0, 0))
    used = w_rank + a_rank + g_rank
    mu_lr = jnp.pad(mu[3 * hw:], (0, lrw - used))
    w2 = jnp.pad(w2, ((0, lrw - w_rank), (0, 0)))
    a2 = jnp.pad(a2, ((w_rank, lrw - w_rank - a_rank), (0, 0)))
    g2 = jnp.pad(g2, ((w_rank + a_rank, lrw - used), (0, 0)))
    out = jax.ShapeDtypeStruct((m, hw), F32)
    return pl.pallas_call(
        functools.partial(_rwkv_prep_kernel, tiles_per_batch=t_len // tm),
        grid=(m // tm,),
        in_specs=[
            cur(1), cur(2), cur(3), pl.BlockSpec((tm, lrw), lambda i: (i, 0)),
            prv(1), prv(2), prv(3),
            pl.BlockSpec((SUBLANE, lrw), lambda i: (jnp.maximum(i * nblk8 - 1, 0), 0)),
            row(hw), row(hw), row(hw), row(lrw),
            row(hw), mat(w2), row(hw), mat(a2), mat(g2), row(hw), row(hw), row(hw),
        ],
        out_specs=[pl.BlockSpec((tm, hw), lambda i: (i, 0))] * 8,
        out_shape=(out,) * 8,
        compiler_params=_cparams(("arbitrary",)),
        name="rwkv_prep",
    )(p_main, p_main, p_main, lr, p_main, p_main, p_main, lr,
      mu[:hw].reshape(1, hw), mu[hw:2 * hw].reshape(1, hw), mu[2 * hw:3 * hw].reshape(1, hw),
      mu_lr.reshape(1, lrw),
      w0.reshape(1, hw), w2, a0.reshape(1, hw), a2, g2, k_k.reshape(1, hw),
      k_a.reshape(1, hw), r_k.reshape(1, hw))


def _mm(a, b):
    return jnp.dot(a.astype(BF16), b.astype(BF16), preferred_element_type=F32)


def _mm_nt(a, b):
    return lax.dot_general(a.astype(BF16), b.astype(BF16), (((1,), (1,)), ((), ())),
                           preferred_element_type=F32)


def _mm_tn(a, b):
    return lax.dot_general(a.astype(BF16), b.astype(BF16), (((0,), (0,)), ((), ())),
                           preferred_element_type=F32)


def _split3(x):
    hi = x.astype(BF16)
    r1 = x - hi.astype(F32)
    mid = r1.astype(BF16)
    lo = (r1 - mid.astype(F32)).astype(BF16)
    return hi, mid, lo


def _rwkv_scan_kernel(r_ref, lw_ref, k_ref, v_ref, a_ref, b_ref, y_ref,
                      z_ref, m_scr, zc_scr, ra_scr, yv_scr, ge_scr, *, n_chunks):
    c_len = RW_CHUNK
    n2 = 2 * c_len

    @pl.when(pl.program_id(2) == 0)
    def _():
        z_ref[...] = jnp.zeros_like(z_ref)

    lane = lax.broadcasted_iota(jnp.int32, (c_len, LANE), 1)
    head_a = lane < RWKV_HEAD
    ri = lax.broadcasted_iota(jnp.int32, (n2, n2), 0)
    ci = lax.broadcasted_iota(jnp.int32, (n2, n2), 1)
    same = (ri // c_len) == (ci // c_len)
    strict = same & ((ci % c_len) < (ri % c_len))
    incl = same & ((ci % c_len) <= (ri % c_len))
    eye = (ri == ci).astype(F32)
    rc = lax.broadcasted_iota(jnp.int32, (c_len, c_len), 0)
    cc = lax.broadcasted_iota(jnp.int32, (c_len, c_len), 1)
    tri = (cc <= rc).astype(BF16)

    def stack(x):
        return jnp.concatenate([jnp.where(head_a, x, 0.0), jnp.where(head_a, 0.0, x)], axis=0)

    def fold(x):
        return x[:c_len, :] + x[c_len:, :]

    units = [(pp, c) for pp in range(RW_PAIRS) for c in range(n_chunks)]

    def window(pp, c):
        return slice(c * c_len, (c + 1) * c_len), slice(pp * LANE, (pp + 1) * LANE)

    pre = []
    for u, (pp, c) in enumerate(units):
        rows, lanes = window(pp, c)
        r, lw, k, v = r_ref[rows, lanes], lw_ref[rows, lanes], k_ref[rows, lanes], v_ref[rows, lanes]
        a, b = a_ref[rows, lanes], b_ref[rows, lanes]
        cl = sum(jnp.dot(tri, part, preferred_element_type=F32) for part in _split3(lw))
        cl_end = cl[c_len - 1:c_len, :]
        g_inv = jnp.exp(-cl)
        g_tail = jnp.exp(cl_end - cl)
        r_t = r * jnp.exp(cl)
        ra_scr[u] = r_t
        ge_scr[u] = jnp.broadcast_to(jnp.exp(cl_end), (SUBLANE, LANE))
        pre.append(dict(
            a_s=stack(a * jnp.exp(cl - lw)).astype(BF16), r_s=stack(r_t).astype(BF16),
            b_s=stack(b * g_inv).astype(BF16), k_s=stack(k * g_inv).astype(BF16),
            v_s=stack(v).astype(BF16), bg_s=stack(b * g_tail).astype(BF16),
            kg_s=stack(k * g_tail).astype(BF16)))

    scs = [_mm_nt(jnp.concatenate([p["a_s"], p["r_s"]], axis=0),
                  jnp.concatenate([p["b_s"], p["k_s"]], axis=0)) for p in pre]
    l_ak = [jnp.where(strict, sc[:n2, n2:], 0.0).astype(BF16) for sc in scs]
    m_rb = [jnp.where(incl, sc[n2:, :n2], 0.0).astype(BF16) for sc in scs]
    m_rk = [jnp.where(incl, sc[n2:, n2:], 0.0).astype(BF16) for sc in scs]

    xs = [jnp.where(strict, sc[:n2, :n2], 0.0) for sc in scs]
    ts = [eye + x for x in xs]
    n = 2
    while n < c_len:
        xs = [_mm(x, x) for x in xs]
        ts = [t + _mm(t, x) for t, x in zip(ts, xs)]
        n *= 2

    lvs = [_mm(l, p["v_s"]) for l, p in zip(l_ak, pre)]
    tatv = [_mm(t, jnp.concatenate([p["a_s"], lv.astype(BF16)], axis=1)).astype(BF16)
            for t, p, lv in zip(ts, pre, lvs)]
    mtatv = [_mm(m, tv) for m, tv in zip(m_rb, tatv)]
    mkv = [_mm(m, p["v_s"]) for m, p in zip(m_rk, pre)]
    gz = [_mm_tn(tv, p["bg_s"]) for tv, p in zip(tatv, pre)]
    vk = [_mm_tn(p["v_s"], p["kg_s"]) for p in pre]
    for u in range(len(units)):
        ra_scr[u] = ra_scr[u] + fold(mtatv[u][:, :LANE])
        yv_scr[u] = fold(mtatv[u][:, LANE:] + mkv[u])
        m_scr[u] = gz[u][:LANE, :]
        zc_scr[u] = gz[u][LANE:, :] + vk[u]

    zts = [z_ref[pp] for pp in range(RW_PAIRS)]
    for c in range(n_chunks):
        for pp in range(RW_PAIRS):
            u = pp * n_chunks + c
            rows, lanes = window(pp, c)
            zt = zts[pp]
            y_ref[rows, lanes] = _mm_nt(ra_scr[u], zt) + yv_scr[u]
            zts[pp] = zt * ge_scr[u][0:1, :] + _mm(zt, m_scr[u]) + zc_scr[u]
    for pp in range(RW_PAIRS):
        z_ref[pp] = zts[pp]


def rwkv_scan(r, lw, k, v, a, b, *, batch, tb):
    m, hw = r.shape
    t_len = m // batch
    nt = t_len // tb
    nc = tb // RW_CHUNK
    lanes = RW_PAIRS * LANE
    nu = RW_PAIRS * nc
    assert t_len % tb == 0 and tb % RW_CHUNK == 0 and hw % lanes == 0
    spec = pl.BlockSpec((tb, lanes), lambda bi, p, t: (bi * nt + t, p))
    return pl.pallas_call(
        functools.partial(_rwkv_scan_kernel, n_chunks=nc),
        grid=(batch, hw // lanes, nt),
        in_specs=[spec] * 6,
        out_specs=spec,
        out_shape=jax.ShapeDtypeStruct((m, hw), F32),
        scratch_shapes=[
            pltpu.VMEM((RW_PAIRS, LANE, LANE), F32),
            pltpu.VMEM((nu, LANE, LANE), F32), pltpu.VMEM((nu, LANE, LANE), F32),
            pltpu.VMEM((nu, RW_CHUNK, LANE), F32), pltpu.VMEM((nu, RW_CHUNK, LANE), F32),
            pltpu.VMEM((nu, SUBLANE, LANE), F32),
        ],
        compiler_params=_cparams(("arbitrary", "arbitrary", "arbitrary")),
        name="rwkv_scan",
    )(r, lw, k, v, a, b)


def _rwkv_post_kernel(y_ref, bon_ref, g_ref, lw_ref, lb_ref, o_ref):
    ones = _block_ones(LANE, RWKV_HEAD)
    y = y_ref[...]
    inv_n = 1.0 / RWKV_HEAD
    mu = _seg_sum(y, ones) * inv_n
    yc = y - mu
    var = _seg_sum(yc * yc, ones) * inv_n
    yn = yc * lax.rsqrt(var + GN_EPS) * lw_ref[...] + lb_ref[...]
    o_ref[...] = ((yn + bon_ref[...]) * g_ref[...]).astype(o_ref.dtype)


def rwkv_post(y, bonus, g, lnx_w, lnx_b, *, tm):
    m, hw = y.shape
    spec = pl.BlockSpec((tm, hw), lambda i: (i, 0))
    row = pl.BlockSpec((1, hw), lambda i: (0, 0))
    return pl.pallas_call(
        _rwkv_post_kernel,
        grid=(m // tm,),
        in_specs=[spec, spec, spec, row, row],
        out_specs=spec,
        out_shape=jax.ShapeDtypeStruct((m, hw), BF16),
        compiler_params=_cparams(("arbitrary",)),
        name="rwkv_post",
    )(y, bonus, g, lnx_w.reshape(1, hw), lnx_b.reshape(1, hw))


def _lru_kernel(gate_ref, xb_ref, cw_ref, cb_ref, wr_ref, br_ref, wi_ref, bi_ref, lam_ref,
                o_ref, halo_ref, h_ref, a_scr, b_scr, *, tc):
    t_idx = pl.program_id(1)
    width = xb_ref.shape[1]
    nblk = tc // SUBLANE
    n_gate_blocks = width // LRU_BLOCK

    @pl.when(t_idx == 0)
    def _():
        halo_ref[...] = jnp.zeros_like(halo_ref)
        h_ref[...] = jnp.zeros_like(h_ref)

    xb = xb_ref[...].astype(F32)
    ext = jnp.concatenate([halo_ref[...], xb], axis=0)
    xc = xb * cw_ref[CONV_WIDTH - 1:CONV_WIDTH, :] + cb_ref[...]
    for d in range(1, CONV_WIDTH):
        sh = pltpu.roll(ext, d, axis=0)[SUBLANE:, :]
        xc = xc + sh * cw_ref[CONV_WIDTH - 1 - d:CONV_WIDTH - d, :]
    halo_ref[...] = xb[tc - SUBLANE:, :]

    sp = _softplus(-lam_ref[...])

    for n in range(n_gate_blocks):
        lanes = slice(n * LRU_BLOCK, (n + 1) * LRU_BLOCK)
        xn = xc[:, lanes]
        xnb = xn.astype(BF16)
        gr = jnp.dot(xnb, wr_ref[n].astype(BF16), preferred_element_type=F32) + br_ref[:, lanes]
        gi = jnp.dot(xnb, wi_ref[n].astype(BF16), preferred_element_type=F32) + bi_ref[:, lanes]
        log_a = -LRU_C * _sigmoid(gr) * sp[:, lanes]
        a = jnp.exp(log_a)
        mult = jnp.sqrt(-jnp.tanh(log_a) * (a * a + 1.0))
        a_scr[:, lanes] = a
        b_scr[:, lanes] = mult * _sigmoid(gi) * xn

    grp = LRU_SCAN_LANES
    row = lax.broadcasted_iota(jnp.int32, (SUBLANE, grp), 0)

    def blk(bidx, carry):
        rows = pl.ds(pl.multiple_of(bidx * SUBLANE, SUBLANE), SUBLANE)
        for g in range(width // grp):
            lanes = slice(g * grp, (g + 1) * grp)
            av = a_scr[rows, lanes]
            bv = b_scr[rows, lanes]
            for d in (1, 2, 4):
                keep = row >= d
                ash = jnp.where(keep, pltpu.roll(av, d, axis=0), 1.0)
                bsh = jnp.where(keep, pltpu.roll(bv, d, axis=0), 0.0)
                bv = bv + av * bsh
                av = av * ash
            hv = bv + av * h_ref[:, lanes]
            b_scr[rows, lanes] = hv
            h_ref[:, lanes] = jnp.broadcast_to(hv[SUBLANE - 1:SUBLANE, :], (SUBLANE, grp))
        return carry

    lax.fori_loop(0, nblk, blk, 0)
    o_ref[...] = (b_scr[...] * gate_ref[...].astype(F32)).astype(o_ref.dtype)


def lru_block(p_odd, conv_w, conv_b, w_r, b_r, w_i, b_i, lam, layer, *, batch, tc):
    m = p_odd.shape[0]
    width = lam.shape[0]
    t_len = m // batch
    nt = t_len // tc
    assert t_len % tc == 0 and p_odd.shape[1] == 2 * width
    row = pl.BlockSpec((1, width), lambda b, t: (0, 0))
    blkw = pl.BlockSpec((None,) + w_r.shape[1:], lambda b, t: (layer, 0, 0, 0))
    return pl.pallas_call(
        functools.partial(_lru_kernel, tc=tc),
        grid=(batch, nt),
        in_specs=[
            pl.BlockSpec((tc, width), lambda b, t: (b * nt + t, 0)),
            pl.BlockSpec((tc, width), lambda b, t: (b * nt + t, 1)),
            pl.BlockSpec((CONV_WIDTH, width), lambda b, t: (0, 0)),
            row, blkw, row, blkw, row, row,
        ],
        out_specs=pl.BlockSpec((tc, width), lambda b, t: (b * nt + t, 0)),
        out_shape=jax.ShapeDtypeStruct((m, width), BF16),
        scratch_shapes=[
            pltpu.VMEM((SUBLANE, width), F32), pltpu.VMEM((SUBLANE, width), F32),
            pltpu.VMEM((tc, width), F32), pltpu.VMEM((tc, width), F32),
        ],
        compiler_params=_cparams(("arbitrary", "arbitrary")),
        name="lru_block",
    )(p_odd, p_odd, conv_w, conv_b.reshape(1, width), w_r, b_r.reshape(1, width),
      w_i, b_i.reshape(1, width), lam.reshape(1, width))


def _s5_block_weights(bb_re, bb_im, c_re, c_im):
    c, g, p = bb_re.shape
    gpt = LANE // c
    tiles = g // gpt
    eye = jnp.eye(gpt, dtype=F32)

    def b_blk(bb):
        bb = bb.reshape(c, tiles, gpt, p)
        return jnp.einsum('cjgp,gh->jgchp', bb, eye).reshape(tiles, gpt * c, gpt * p)

    def c_blk(cm):
        cm = cm.reshape(tiles, gpt, c, p)
        return jnp.einsum('jgcp,gh->jhpgc', cm, eye).reshape(tiles, gpt * p, gpt * c)

    return (b_blk(bb_re).astype(BF16), b_blk(bb_im).astype(BF16),
            c_blk(c_re).astype(BF16), c_blk(c_im).astype(BF16))


def _even_mixer(x, g_pre, g_post, batch, idx, w_in, shift_mu, lam_re, lam_im, log_dt, b_re, b_im,
                c_re, c_im, d_skip, w_glu, w0, w2, a0, a2, g2, k_k, k_a, r_k, lnx_w, lnx_b,
                w_out):
    hw = w0.shape[0]
    s5w = d_skip.shape[0]
    n_main = s5w + 3 * hw
    p_main = norm_matmul(x, g_pre, w_in, idx, n_main, tm=1024, tn=512, out_dtype=BF16,
                         name="even_in_proj")
    lrw = w_in.shape[2] - n_main
    lr_pad = -(-lrw // LANE) * LANE
    w_lr = jnp.pad(w_in[idx, :, n_main:], ((0, 0), (0, lr_pad - lrw)))[None]
    lr = norm_matmul(x, g_pre, w_lr, 0, lr_pad, tm=1024, tn=lr_pad, out_dtype=F32,
                     name="even_lr_proj")

    a_re, a_im, bb_re, bb_im = s5_discretise(lam_re, lam_im, log_dt, b_re, b_im)
    bblk_re, bblk_im, cblk_re, cblk_im = _s5_block_weights(bb_re, bb_im, c_re, c_im)
    y_s5 = s5_scan(p_main, bblk_re, bblk_im, cblk_re, cblk_im,
                   a_re.reshape(1, -1), a_im.reshape(1, -1), d_skip.reshape(1, -1),
                   batch=batch, tc=256)
    y_s5 = s5_glu(y_s5, w_glu, idx, tm=1024, tn=512)

    r, lw, k, v, a, b, bonus, g = rwkv_prep(p_main, lr, shift_mu, w0, w2, a0, a2, g2,
                                            k_k, k_a, r_k.reshape(-1), batch=batch, tm=256)
    y_rw = rwkv_scan(r, lw, k, v, a, b, batch=batch, tb=512)
    y_rw = rwkv_post(y_rw, bonus, g, lnx_w, lnx_b, tm=512)

    return out_proj_norm_residual([y_s5, y_rw], w_out, idx, x, g_post, tm=512,
                                  name="even_out_proj")


def _odd_mixer(x, g_pre, g_post, batch, idx, w_in, conv_w, conv_b, w_r, b_r, w_i, b_i, lam, w_out):
    width = lam.shape[0]
    p_odd = norm_matmul(x, g_pre, w_in, idx, 2 * width, tm=1024, tn=512, out_dtype=BF16,
                        n_gelu_tiles=width // 512, name="odd_in_proj")
    hg = lru_block(p_odd, conv_w, conv_b, w_r, b_r, w_i, b_i, lam, idx, batch=batch, tc=256)
    return out_proj_norm_residual([hg], w_out, idx, x, g_post, tm=512, name="odd_out_proj")


def _ffn(x, g_pre, g_post, layer, w_gate, w_up, w_down):
    act = norm_swiglu_up(x, g_pre, w_gate, w_up, layer, tm=1024, tn=512)
    return matmul_norm_residual(act, w_down, layer, x, g_post, tm=1024, tk=512, name="ffn_down")


def kernel(x, ev_w_in, ev_shift_mu, s5_lam_re, s5_lam_im, s5_log_dt, s5_b_re, s5_b_im, s5_c_re, s5_c_im, s5_d, s5_w_glu, rw_w0, rw_w2, rw_a0, rw_a2, rw_g2, rw_k_k, rw_k_a, rw_r_k, rw_lnx_w, rw_lnx_b, ev_w_out, od_w_in, od_conv_w, od_conv_b, lru_w_r, lru_b_r, lru_w_i, lru_b_i, lru_lam, od_w_out, ffn_w_gate, ffn_w_up, ffn_w_down, norm_mix_pre, norm_mix_post, norm_ffn_pre, norm_ffn_post):
    batch, t_len, d = x.shape
    depth = ffn_w_gate.shape[0]
    xf = x.reshape(batch * t_len, d)
    ev_w_in, s5_w_glu, ev_w_out, od_w_in, od_w_out = (
        w.astype(BF16) for w in (ev_w_in, s5_w_glu, ev_w_out, od_w_in, od_w_out))
    for layer in range(depth):
        i = layer // 2
        if layer % 2 == 0:
            xf = _even_mixer(xf, norm_mix_pre[layer], norm_mix_post[layer], batch, i,
                             ev_w_in, ev_shift_mu[i], s5_lam_re[i], s5_lam_im[i],
                             s5_log_dt[i], s5_b_re[i], s5_b_im[i], s5_c_re[i], s5_c_im[i],
                             s5_d[i], s5_w_glu, rw_w0[i], rw_w2[i], rw_a0[i], rw_a2[i],
                             rw_g2[i], rw_k_k[i], rw_k_a[i], rw_r_k[i], rw_lnx_w[i],
                             rw_lnx_b[i], ev_w_out)
        else:
            xf = _odd_mixer(xf, norm_mix_pre[layer], norm_mix_post[layer], batch, i,
                            od_w_in, od_conv_w[i], od_conv_b[i], lru_w_r, lru_b_r[i],
                            lru_w_i, lru_b_i[i], lru_lam[i], od_w_out)
        xf = _ffn(xf, norm_ffn_pre[layer], norm_ffn_post[layer], layer,
                  ffn_w_gate, ffn_w_up, ffn_w_down)
    return xf.reshape(batch, t_len, d)
```

```python
import functools
import math

import jax
import jax.numpy as jnp
from jax import lax
from jax.experimental import pallas as pl
from jax.experimental.pallas import tpu as pltpu

F32 = jnp.float32
BF16 = jnp.bfloat16

NORM_EPS = 1e-6
GN_EPS = 64e-5
LRU_C = 8.0
S5_GROUP = 16
S5_STATE = 64
RWKV_HEAD = 64
CONV_WIDTH = 4
LRU_BLOCK = 256

LANE = 128
SUBLANE = 8
VMEM_LIMIT = 56 * 1024 * 1024

RW_CHUNK = 64
RW_PAIRS = 2
S5_SCAN_LANES = 1024
LRU_SCAN_LANES = 512


def _cparams(sem):
    return pltpu.CompilerParams(dimension_semantics=sem, vmem_limit_bytes=VMEM_LIMIT)


def _gelu(x):
    c = math.sqrt(2.0 / math.pi)
    return 0.5 * x * (1.0 + jnp.tanh(c * (x + 0.044715 * (x * x * x))))


def _sigmoid(x):
    return 1.0 / (1.0 + jnp.exp(-x))


def _softplus(x):
    return jnp.maximum(x, 0.0) + jnp.log(1.0 + jnp.exp(-jnp.abs(x)))


def _rms_rows(y, g):
    ms = jnp.mean(y * y, axis=-1, keepdims=True)
    return y * lax.rsqrt(ms + NORM_EPS) * g


def _norm_mm_kernel(x_ref, g_ref, w_ref, o_ref, h_ref, *, n_gelu_tiles):
    j = pl.program_id(1)

    @pl.when(j == 0)
    def _():
        h_ref[...] = _rms_rows(x_ref[...], g_ref[...]).astype(BF16)

    acc = jnp.dot(h_ref[...], w_ref[...].astype(BF16), preferred_element_type=F32)
    if n_gelu_tiles == 0:
        o_ref[...] = acc.astype(o_ref.dtype)
    else:
        @pl.when(j < n_gelu_tiles)
        def _():
            o_ref[...] = _gelu(acc).astype(o_ref.dtype)

        @pl.when(j >= n_gelu_tiles)
        def _():
            o_ref[...] = acc.astype(o_ref.dtype)


def norm_matmul(x, g, w, layer, n_out, *, tm, tn, out_dtype, n_gelu_tiles=0, name):
    m, d = x.shape
    assert m % tm == 0 and n_out % tn == 0 and w.shape[1] == d
    return pl.pallas_call(
        functools.partial(_norm_mm_kernel, n_gelu_tiles=n_gelu_tiles),
        grid=(m // tm, n_out // tn),
        in_specs=[
            pl.BlockSpec((tm, d), lambda i, j: (i, 0)),
            pl.BlockSpec((1, d), lambda i, j: (0, 0)),
            pl.BlockSpec((None, d, tn), lambda i, j: (layer, 0, j)),
        ],
        out_specs=pl.BlockSpec((tm, tn), lambda i, j: (i, j)),
        out_shape=jax.ShapeDtypeStruct((m, n_out), out_dtype),
        scratch_shapes=[pltpu.VMEM((tm, d), BF16)],
        compiler_params=_cparams(("arbitrary", "arbitrary")),
        name=name,
    )(x, g.reshape(1, d), w)


def _swiglu_up_kernel(h_ref, wg_ref, wu_ref, o_ref):
    h = h_ref[...]
    gate = jnp.dot(h, wg_ref[...].astype(BF16), preferred_element_type=F32)
    up = jnp.dot(h, wu_ref[...].astype(BF16), preferred_element_type=F32)
    o_ref[...] = (gate * _sigmoid(gate) * up).astype(o_ref.dtype)


def swiglu_up(h, w_gate, w_up, layer, *, tm, tn):
    m, d = h.shape
    n = w_gate.shape[2]
    assert m % tm == 0 and n % tn == 0 and h.dtype == BF16
    return pl.pallas_call(
        _swiglu_up_kernel,
        grid=(m // tm, n // tn),
        in_specs=[
            pl.BlockSpec((tm, d), lambda i, j: (i, 0)),
            pl.BlockSpec((None, d, tn), lambda i, j: (layer, 0, j)),
            pl.BlockSpec((None, d, tn), lambda i, j: (layer, 0, j)),
        ],
        out_specs=pl.BlockSpec((tm, tn), lambda i, j: (i, j)),
        out_shape=jax.ShapeDtypeStruct((m, n), BF16),
        compiler_params=_cparams(("arbitrary", "arbitrary")),
        name="ffn_up",
    )(h, w_gate, w_up)


def _mm_norm_res_kernel(a_ref, w_ref, x_ref, g_ref, o_ref):
    k = pl.program_id(1)

    def partial_product():
        return jnp.dot(a_ref[...], w_ref[...].astype(BF16), preferred_element_type=F32)

    @pl.when(k == 0)
    def _():
        o_ref[...] = partial_product()

    @pl.when(k > 0)
    def _():
        o_ref[...] += partial_product()

    @pl.when(k == pl.num_programs(1) - 1)
    def _():
        o_ref[...] = x_ref[...] + _rms_rows(o_ref[...], g_ref[...])


def matmul_norm_residual(a, w, layer, x, g, *, tm, tk, name):
    m, kdim = a.shape
    d = w.shape[2]
    assert m % tm == 0 and kdim % tk == 0 and w.shape[1] == kdim
    return pl.pallas_call(
        _mm_norm_res_kernel,
        grid=(m // tm, kdim // tk),
        in_specs=[
            pl.BlockSpec((tm, tk), lambda i, k: (i, k)),
            pl.BlockSpec((None, tk, d), lambda i, k: (layer, k, 0)),
            pl.BlockSpec((tm, d), lambda i, k: (i, 0)),
            pl.BlockSpec((1, d), lambda i, k: (0, 0)),
        ],
        out_specs=pl.BlockSpec((tm, d), lambda i, k: (i, 0)),
        out_shape=jax.ShapeDtypeStruct((m, d), F32),
        compiler_params=_cparams(("arbitrary", "arbitrary")),
        name=name,
    )(a, w, x, g.reshape(1, d))


def _out_proj_kernel(*refs, n_lhs):
    a_refs = refs[:n_lhs]
    w_ref, x_ref, g_ref, gn_ref, o_ref, hn_ref = refs[n_lhs:]
    y = None
    k0 = 0
    for a_ref in a_refs:
        kw = a_ref.shape[1]
        part = jnp.dot(a_ref[...], w_ref[k0:k0 + kw, :], preferred_element_type=F32)
        y = part if y is None else y + part
        k0 += kw
    x_new = x_ref[...] + _rms_rows(y, g_ref[...])
    o_ref[...] = x_new
    hn_ref[...] = _rms_rows(x_new, gn_ref[...]).astype(hn_ref.dtype)


def out_proj_norm_residual(lhs, w, layer, x, g, g_next, *, tm, name):
    m = x.shape[0]
    kdim, d = w.shape[1], w.shape[2]
    assert sum(a.shape[1] for a in lhs) == kdim and m % tm == 0
    row = pl.BlockSpec((1, d), lambda i: (0, 0))
    return pl.pallas_call(
        functools.partial(_out_proj_kernel, n_lhs=len(lhs)),
        grid=(m // tm,),
        in_specs=[pl.BlockSpec((tm, a.shape[1]), lambda i: (i, 0)) for a in lhs] + [
            pl.BlockSpec((None, kdim, d), lambda i: (layer, 0, 0)),
            pl.BlockSpec((tm, d), lambda i: (i, 0)),
            row, row,
        ],
        out_specs=[pl.BlockSpec((tm, d), lambda i: (i, 0))] * 2,
        out_shape=(jax.ShapeDtypeStruct((m, d), F32), jax.ShapeDtypeStruct((m, d), BF16)),
        compiler_params=_cparams(("arbitrary",)),
        name=name,
    )(*lhs, w, x, g.reshape(1, d), g_next.reshape(1, d))


def _block_ones(n, seg):
    r = lax.broadcasted_iota(jnp.int32, (n, n), 0) // seg
    c = lax.broadcasted_iota(jnp.int32, (n, n), 1) // seg
    return (r == c).astype(BF16)


def _seg_sum(x, ones):
    outs = []
    for j in range(x.shape[1] // LANE):
        xj = x[:, j * LANE:(j + 1) * LANE]
        hi = xj.astype(BF16)
        r1 = xj - hi.astype(F32)
        mid = r1.astype(BF16)
        lo = (r1 - mid.astype(F32)).astype(BF16)
        s = (jnp.dot(hi, ones, preferred_element_type=F32)
             + jnp.dot(mid, ones, preferred_element_type=F32)
             + jnp.dot(lo, ones, preferred_element_type=F32))
        outs.append(s)
    return jnp.concatenate(outs, axis=1) if len(outs) > 1 else outs[0]


def _s5_disc_kernel(lr_ref, li_ref, ldt_ref, bre_ref, bim_ref,
                    are_ref, aim_ref, bbre_ref, bbim_ref):
    lr = lr_ref[...]
    li = li_ref[...]
    dt = jnp.exp(ldt_ref[...])
    mag = jnp.exp(lr * dt)
    a_re = mag * jnp.cos(li * dt)
    a_im = mag * jnp.sin(li * dt)
    den = lr * lr + li * li
    nr = a_re - 1.0
    ni = a_im
    gam_re = (nr * lr + ni * li) / den
    gam_im = (ni * lr - nr * li) / den
    are_ref[...] = a_re
    aim_ref[...] = a_im
    for c in range(bre_ref.shape[0]):
        br = bre_ref[c]
        bi = bim_ref[c]
        bbre_ref[c] = gam_re * br - gam_im * bi
        bbim_ref[c] = gam_re * bi + gam_im * br


def s5_discretise(lam_re, lam_im, log_dt, b_re, b_im):
    g, p, c = b_re.shape
    ldt = jnp.broadcast_to(log_dt[:, None], (g, p))
    b_re_t = jnp.transpose(b_re, (2, 0, 1))
    b_im_t = jnp.transpose(b_im, (2, 0, 1))
    gp = jax.ShapeDtypeStruct((g, p), F32)
    cgp = jax.ShapeDtypeStruct((c, g, p), F32)
    return pl.pallas_call(
        _s5_disc_kernel,
        out_shape=(gp, gp, cgp, cgp),
        name="s5_discretise",
    )(lam_re, lam_im, ldt, b_re_t, b_im_t)


def _cmul(ar, ai, br, bi):
    return ar * br - ai * bi, ar * bi + ai * br


def _s5_scan_kernel(u_ref, bre_ref, bim_ref, cre_ref, cim_ref, are_ref, aim_ref, d_ref,
                    o_ref, car_ref, cai_ref, bur_ref, bui_ref, tab_ref, *, tc):
    t_idx = pl.program_id(1)
    n_tiles = u_ref.shape[1] // LANE
    ml = bre_ref.shape[2]
    modes = n_tiles * ml
    nblk = tc // SUBLANE
    grp = S5_SCAN_LANES
    row = lax.broadcasted_iota(jnp.int32, (SUBLANE, grp), 0)

    @pl.when(t_idx == 0)
    def _():
        car_ref[...] = jnp.zeros_like(car_ref)
        cai_ref[...] = jnp.zeros_like(cai_ref)
        for g in range(modes // grp):
            lanes = slice(g * grp, (g + 1) * grp)
            a1r = jnp.broadcast_to(are_ref[:, lanes], (SUBLANE, grp))
            a1i = jnp.broadcast_to(aim_ref[:, lanes], (SUBLANE, grp))
            a2r, a2i = _cmul(a1r, a1i, a1r, a1i)
            a4r, a4i = _cmul(a2r, a2i, a2r, a2i)
            pr, pi = a1r, a1i
            cr, ci = a1r, a1i
            for r in range(1, SUBLANE):
                cr, ci = _cmul(cr, ci, a1r, a1i)
                pr = jnp.where(row >= r, cr, pr)
                pi = jnp.where(row >= r, ci, pi)
            for lvl, (d, xr, xi) in enumerate(((1, a1r, a1i), (2, a2r, a2i), (4, a4r, a4i))):
                tab_ref[2 * lvl, :, lanes] = jnp.where(row >= d, xr, 0.0)
                tab_ref[2 * lvl + 1, :, lanes] = jnp.where(row >= d, xi, 0.0)
            tab_ref[6, :, lanes] = pr
            tab_ref[7, :, lanes] = pi

    for j in range(n_tiles):
        uj = u_ref[:, j * LANE:(j + 1) * LANE]
        bur_ref[:, j * ml:(j + 1) * ml] = jnp.dot(uj, bre_ref[j], preferred_element_type=F32)
        bui_ref[:, j * ml:(j + 1) * ml] = jnp.dot(uj, bim_ref[j], preferred_element_type=F32)

    def blk(b, carry):
        rows = pl.ds(pl.multiple_of(b * SUBLANE, SUBLANE), SUBLANE)
        for g in range(modes // grp):
            lanes = slice(g * grp, (g + 1) * grp)
            sr = bur_ref[rows, lanes]
            si = bui_ref[rows, lanes]
            for lvl, d in enumerate((1, 2, 4)):
                kr = tab_ref[2 * lvl, :, lanes]
                ki = tab_ref[2 * lvl + 1, :, lanes]
                rr = pltpu.roll(sr, d, axis=0)
                ri = pltpu.roll(si, d, axis=0)
                sr, si = sr + kr * rr - ki * ri, si + kr * ri + ki * rr
            pr = tab_ref[6, :, lanes]
            pi = tab_ref[7, :, lanes]
            cbr = car_ref[:, lanes]
            cbi = cai_ref[:, lanes]
            sr = sr + pr * cbr - pi * cbi
            si = si + pr * cbi + pi * cbr
            bur_ref[rows, lanes] = sr
            bui_ref[rows, lanes] = si
            car_ref[:, lanes] = jnp.broadcast_to(sr[SUBLANE - 1:SUBLANE, :], (SUBLANE, grp))
            cai_ref[:, lanes] = jnp.broadcast_to(si[SUBLANE - 1:SUBLANE, :], (SUBLANE, grp))
        return carry

    lax.fori_loop(0, nblk, blk, 0)

    for j in range(n_tiles):
        lanes = slice(j * ml, (j + 1) * ml)
        y = (jnp.dot(bur_ref[:, lanes].astype(BF16), cre_ref[j], preferred_element_type=F32)
             - jnp.dot(bui_ref[:, lanes].astype(BF16), cim_ref[j], preferred_element_type=F32))
        uj = u_ref[:, j * LANE:(j + 1) * LANE].astype(F32)
        y = y + d_ref[:, j * LANE:(j + 1) * LANE] * uj
        o_ref[:, j * LANE:(j + 1) * LANE] = _gelu(y)


def s5_scan(p_main, bblk_re, bblk_im, cblk_re, cblk_im, a_re, a_im, d_skip, *, batch, tc):
    m = p_main.shape[0]
    t_len = m // batch
    n_tiles, _, ml = bblk_re.shape
    width = n_tiles * LANE
    modes = n_tiles * ml
    nt = t_len // tc
    assert t_len % tc == 0 and modes % S5_SCAN_LANES == 0 and p_main.dtype == BF16
    return pl.pallas_call(
        functools.partial(_s5_scan_kernel, tc=tc),
        grid=(batch, nt),
        in_specs=[
            pl.BlockSpec((tc, width), lambda b, t: (b * nt + t, 0)),
            pl.BlockSpec((n_tiles, LANE, ml), lambda b, t: (0, 0, 0)),
            pl.BlockSpec((n_tiles, LANE, ml), lambda b, t: (0, 0, 0)),
            pl.BlockSpec((n_tiles, ml, LANE), lambda b, t: (0, 0, 0)),
            pl.BlockSpec((n_tiles, ml, LANE), lambda b, t: (0, 0, 0)),
            pl.BlockSpec((1, modes), lambda b, t: (0, 0)),
            pl.BlockSpec((1, modes), lambda b, t: (0, 0)),
            pl.BlockSpec((1, width), lambda b, t: (0, 0)),
        ],
        out_specs=pl.BlockSpec((tc, width), lambda b, t: (b * nt + t, 0)),
        out_shape=jax.ShapeDtypeStruct((m, width), F32),
        scratch_shapes=[
            pltpu.VMEM((SUBLANE, modes), F32), pltpu.VMEM((SUBLANE, modes), F32),
            pltpu.VMEM((tc, modes), F32), pltpu.VMEM((tc, modes), F32),
            pltpu.VMEM((8, SUBLANE, modes), F32),
        ],
        compiler_params=_cparams(("arbitrary", "arbitrary")),
        name="s5_scan",
    )(p_main, bblk_re, bblk_im, cblk_re, cblk_im, a_re, a_im, d_skip)


def _s5_glu_kernel(y_ref, yt_ref, w_ref, o_ref):
    z = jnp.dot(y_ref[...].astype(BF16), w_ref[...], preferred_element_type=F32)
    o_ref[...] = (yt_ref[...] * _sigmoid(z)).astype(o_ref.dtype)


def s5_glu(y, w, layer, *, tm, tn):
    m, d = y.shape
    return pl.pallas_call(
        _s5_glu_kernel,
        grid=(m // tm, d // tn),
        in_specs=[
            pl.BlockSpec((tm, d), lambda i, j: (i, 0)),
            pl.BlockSpec((tm, tn), lambda i, j: (i, j)),
            pl.BlockSpec((None, d, tn), lambda i, j: (layer, 0, j)),
        ],
        out_specs=pl.BlockSpec((tm, tn), lambda i, j: (i, j)),
        out_shape=jax.ShapeDtypeStruct((m, d), BF16),
        compiler_params=_cparams(("arbitrary", "arbitrary")),
        name="s5_glu",
    )(y, y, w)


def _shift_mix(z, prev_row, mu):
    row = lax.broadcasted_iota(jnp.int32, z.shape, 0)
    zs = jnp.where(row == 0, jnp.broadcast_to(prev_row, z.shape), pltpu.roll(z, 1, axis=0))
    return z + (zs - z) * mu


def _rwkv_prep_kernel(r_ref, k_ref, v_ref, lr_ref, rp_ref, kp_ref, vp_ref, lrp_ref,
                      mur_ref, muk_ref, muv_ref, mulr_ref,
                      w0_ref, w2_ref, a0_ref, a2_ref, g2_ref, kk_ref, ka_ref, rk_ref,
                      ro_ref, lw_ref, ko_ref, vo_ref, ao_ref, bo_ref, bon_ref, go_ref,
                      *, tiles_per_batch):
    i = pl.program_id(0)
    first = (i % tiles_per_batch) == 0

    def prev(ref):
        last = ref.shape[0] - 1
        return jnp.where(first, 0.0, ref[last:last + 1, :].astype(F32))

    r = _shift_mix(r_ref[...].astype(F32), prev(rp_ref), mur_ref[...])
    k = _shift_mix(k_ref[...].astype(F32), prev(kp_ref), muk_ref[...])
    v = _shift_mix(v_ref[...].astype(F32), prev(vp_ref), muv_ref[...])
    lr = _shift_mix(lr_ref[...], prev(lrp_ref), mulr_ref[...])

    wl = w0_ref[...] + jnp.dot(jnp.tanh(lr).astype(BF16), w2_ref[...].astype(BF16),
                               preferred_element_type=F32)
    w = -_softplus(-wl) - 0.5
    lw_ref[...] = -jnp.exp(w)
    a = _sigmoid(a0_ref[...] + jnp.dot(lr.astype(BF16), a2_ref[...].astype(BF16),
                                       preferred_element_type=F32))
    go_ref[...] = jnp.dot(_sigmoid(lr).astype(BF16), g2_ref[...].astype(BF16),
                          preferred_element_type=F32).astype(go_ref.dtype)

    ones = _block_ones(LANE, RWKV_HEAD)
    kk = k * kk_ref[...]
    kk = kk * lax.rsqrt(jnp.maximum(_seg_sum(kk * kk, ones), 1e-24))
    k = k * (1.0 + (a - 1.0) * ka_ref[...])
    ro_ref[...] = r.astype(ro_ref.dtype)
    ko_ref[...] = k.astype(ko_ref.dtype)
    vo_ref[...] = v.astype(vo_ref.dtype)
    ao_ref[...] = (-kk).astype(ao_ref.dtype)
    bo_ref[...] = (kk * a).astype(bo_ref.dtype)
    bon_ref[...] = (_seg_sum(r * k * rk_ref[...], ones) * v).astype(bon_ref.dtype)


def rwkv_prep(p_main, lr, mu, w0, w2, a0, a2, g2, k_k, k_a, r_k, *, batch, tm):
    m = p_main.shape[0]
    t_len = m // batch
    hw = w0.shape[0]
    lrw = lr.shape[1]
    w_rank, a_rank, g_rank = w2.shape[0], a2.shape[0], g2.shape[0]
    nblk8 = tm // SUBLANE
    halo = 2 * SUBLANE
    nblk16 = tm // halo
    assert t_len % tm == 0 and p_main.shape[1] == 4 * hw and p_main.dtype == BF16

    def cur(c):
        return pl.BlockSpec((tm, hw), lambda i: (i, c))

    def prv(c):
        return pl.BlockSpec((halo, hw), lambda i: (jnp.maximum(i * nblk16 - 1, 0), c))

    row = lambda n: pl.BlockSpec((1, n), lambda i: (0, 0))
    mat = lambda a: pl.BlockSpec(a.shape, lambda i: (0, 0))
    used = w_rank + a_rank + g_rank
    mu_lr = jnp.pad(mu[3 * hw:], (0, lrw - used))
    w2 = jnp.pad(w2, ((0, lrw - w_rank), (0, 0)))
    a2 = jnp.pad(a2, ((w_rank, lrw - w_rank - a_rank), (0, 0)))
    g2 = jnp.pad(g2, ((w_rank + a_rank, lrw - used), (0, 0)))
    out = jax.ShapeDtypeStruct((m, hw), BF16)
    out_lw = jax.ShapeDtypeStruct((m, hw), F32)
    return pl.pallas_call(
        functools.partial(_rwkv_prep_kernel, tiles_per_batch=t_len // tm),
        grid=(m // tm,),
        in_specs=[
            cur(1), cur(2), cur(3), pl.BlockSpec((tm, lrw), lambda i: (i, 0)),
            prv(1), prv(2), prv(3),
            pl.BlockSpec((SUBLANE, lrw), lambda i: (jnp.maximum(i * nblk8 - 1, 0), 0)),
            row(hw), row(hw), row(hw), row(lrw),
            row(hw), mat(w2), row(hw), mat(a2), mat(g2), row(hw), row(hw), row(hw),
        ],
        out_specs=[pl.BlockSpec((tm, hw), lambda i: (i, 0))] * 8,
        out_shape=(out, out_lw) + (out,) * 6,
        compiler_params=_cparams(("arbitrary",)),
        name="rwkv_prep",
    )(p_main, p_main, p_main, lr, p_main, p_main, p_main, lr,
      mu[:hw].reshape(1, hw), mu[hw:2 * hw].reshape(1, hw), mu[2 * hw:3 * hw].reshape(1, hw),
      mu_lr.reshape(1, lrw),
      w0.reshape(1, hw), w2, a0.reshape(1, hw), a2, g2, k_k.reshape(1, hw),
      k_a.reshape(1, hw), r_k.reshape(1, hw))


def _mm(a, b):
    return jnp.dot(a.astype(BF16), b.astype(BF16), preferred_element_type=F32)


def _mm_nt(a, b):
    return lax.dot_general(a.astype(BF16), b.astype(BF16), (((1,), (1,)), ((), ())),
                           preferred_element_type=F32)


def _mm_tn(a, b):
    return lax.dot_general(a.astype(BF16), b.astype(BF16), (((0,), (0,)), ((), ())),
                           preferred_element_type=F32)


def _split3(x):
    hi = x.astype(BF16)
    r1 = x - hi.astype(F32)
    mid = r1.astype(BF16)
    lo = (r1 - mid.astype(F32)).astype(BF16)
    return hi, mid, lo


def _rwkv_scan_kernel(r_ref, lw_ref, k_ref, v_ref, a_ref, b_ref, y_ref,
                      z_ref, m_scr, zc_scr, ra_scr, yv_scr, ge_scr, *, n_chunks):
    c_len = RW_CHUNK
    n2 = 2 * c_len

    @pl.when(pl.program_id(2) == 0)
    def _():
        z_ref[...] = jnp.zeros_like(z_ref)

    lane = lax.broadcasted_iota(jnp.int32, (c_len, LANE), 1)
    head_a = lane < RWKV_HEAD
    ri = lax.broadcasted_iota(jnp.int32, (n2, n2), 0)
    ci = lax.broadcasted_iota(jnp.int32, (n2, n2), 1)
    same = (ri // c_len) == (ci // c_len)
    strict = same & ((ci % c_len) < (ri % c_len))
    incl = same & ((ci % c_len) <= (ri % c_len))
    eye = (ri == ci).astype(F32)
    rc = lax.broadcasted_iota(jnp.int32, (c_len, c_len), 0)
    cc = lax.broadcasted_iota(jnp.int32, (c_len, c_len), 1)
    tri = (cc <= rc).astype(BF16)

    def stack(x):
        return jnp.concatenate([jnp.where(head_a, x, 0.0), jnp.where(head_a, 0.0, x)], axis=0)

    def fold(x):
        return x[:c_len, :] + x[c_len:, :]

    units = [(pp, c) for pp in range(RW_PAIRS) for c in range(n_chunks)]

    def window(pp, c):
        return slice(c * c_len, (c + 1) * c_len), slice(pp * LANE, (pp + 1) * LANE)

    pre = []
    for u, (pp, c) in enumerate(units):
        rows, lanes = window(pp, c)
        lw = lw_ref[rows, lanes]
        r, k, v, a, b = (ref[rows, lanes].astype(F32) for ref in (r_ref, k_ref, v_ref, a_ref, b_ref))
        cl = sum(jnp.dot(tri, part, preferred_element_type=F32) for part in _split3(lw))
        cl_end = cl[c_len - 1:c_len, :]
        g_inv = jnp.exp(-cl)
        g_tail = jnp.exp(cl_end - cl)
        r_t = r * jnp.exp(cl)
        ra_scr[u] = r_t
        ge_scr[u] = jnp.broadcast_to(jnp.exp(cl_end), (SUBLANE, LANE))
        pre.append(dict(
            a_s=stack(a * jnp.exp(cl - lw)).astype(BF16), r_s=stack(r_t).astype(BF16),
            b_s=stack(b * g_inv).astype(BF16), k_s=stack(k * g_inv).astype(BF16),
            v_s=stack(v).astype(BF16), bg_s=stack(b * g_tail).astype(BF16),
            kg_s=stack(k * g_tail).astype(BF16)))

    scs = [_mm_nt(jnp.concatenate([p["a_s"], p["r_s"]], axis=0),
                  jnp.concatenate([p["b_s"], p["k_s"]], axis=0)) for p in pre]
    l_ak = [jnp.where(strict, sc[:n2, n2:], 0.0).astype(BF16) for sc in scs]
    m_rb = [jnp.where(incl, sc[n2:, :n2], 0.0).astype(BF16) for sc in scs]
    m_rk = [jnp.where(incl, sc[n2:, n2:], 0.0).astype(BF16) for sc in scs]

    xs = [jnp.where(strict, sc[:n2, :n2], 0.0) for sc in scs]
    ts = [eye + x for x in xs]
    n = 2
    while n < c_len:
        xs = [_mm(x, x) for x in xs]
        ts = [t + _mm(t, x) for t, x in zip(ts, xs)]
        n *= 2

    lvs = [_mm(l, p["v_s"]) for l, p in zip(l_ak, pre)]
    tatv = [_mm(t, jnp.concatenate([p["a_s"], lv.astype(BF16)], axis=1)).astype(BF16)
            for t, p, lv in zip(ts, pre, lvs)]
    mtatv = [_mm(m, tv) for m, tv in zip(m_rb, tatv)]
    mkv = [_mm(m, p["v_s"]) for m, p in zip(m_rk, pre)]
    gz = [_mm_tn(tv, p["bg_s"]) for tv, p in zip(tatv, pre)]
    vk = [_mm_tn(p["v_s"], p["kg_s"]) for p in pre]
    for u in range(len(units)):
        ra_scr[u] = ra_scr[u] + fold(mtatv[u][:, :LANE])
        yv_scr[u] = fold(mtatv[u][:, LANE:] + mkv[u])
        m_scr[u] = gz[u][:LANE, :]
        zc_scr[u] = gz[u][LANE:, :] + vk[u]

    zts = [z_ref[pp] for pp in range(RW_PAIRS)]
    for c in range(n_chunks):
        for pp in range(RW_PAIRS):
            u = pp * n_chunks + c
            rows, lanes = window(pp, c)
            zt = zts[pp]
            y_ref[rows, lanes] = _mm_nt(ra_scr[u], zt) + yv_scr[u]
            zts[pp] = zt * ge_scr[u][0:1, :] + _mm(zt, m_scr[u]) + zc_scr[u]
    for pp in range(RW_PAIRS):
        z_ref[pp] = zts[pp]


def rwkv_scan(r, lw, k, v, a, b, *, batch, tb):
    m, hw = r.shape
    t_len = m // batch
    nt = t_len // tb
    nc = tb // RW_CHUNK
    lanes = RW_PAIRS * LANE
    nu = RW_PAIRS * nc
    assert t_len % tb == 0 and tb % RW_CHUNK == 0 and hw % lanes == 0
    spec = pl.BlockSpec((tb, lanes), lambda bi, p, t: (bi * nt + t, p))
    return pl.pallas_call(
        functools.partial(_rwkv_scan_kernel, n_chunks=nc),
        grid=(batch, hw // lanes, nt),
        in_specs=[spec] * 6,
        out_specs=spec,
        out_shape=jax.ShapeDtypeStruct((m, hw), F32),
        scratch_shapes=[
            pltpu.VMEM((RW_PAIRS, LANE, LANE), F32),
            pltpu.VMEM((nu, LANE, LANE), F32), pltpu.VMEM((nu, LANE, LANE), F32),
            pltpu.VMEM((nu, RW_CHUNK, LANE), F32), pltpu.VMEM((nu, RW_CHUNK, LANE), F32),
            pltpu.VMEM((nu, SUBLANE, LANE), F32),
        ],
        compiler_params=_cparams(("arbitrary", "arbitrary", "arbitrary")),
        name="rwkv_scan",
    )(r, lw, k, v, a, b)


def _rwkv_post_kernel(y_ref, bon_ref, g_ref, lw_ref, lb_ref, o_ref):
    ones = _block_ones(LANE, RWKV_HEAD)
    y = y_ref[...]
    inv_n = 1.0 / RWKV_HEAD
    mu = _seg_sum(y, ones) * inv_n
    yc = y - mu
    var = _seg_sum(yc * yc, ones) * inv_n
    yn = yc * lax.rsqrt(var + GN_EPS) * lw_ref[...] + lb_ref[...]
    o_ref[...] = ((yn + bon_ref[...].astype(F32)) * g_ref[...].astype(F32)).astype(o_ref.dtype)


def rwkv_post(y, bonus, g, lnx_w, lnx_b, *, tm):
    m, hw = y.shape
    spec = pl.BlockSpec((tm, hw), lambda i: (i, 0))
    row = pl.BlockSpec((1, hw), lambda i: (0, 0))
    return pl.pallas_call(
        _rwkv_post_kernel,
        grid=(m // tm,),
        in_specs=[spec, spec, spec, row, row],
        out_specs=spec,
        out_shape=jax.ShapeDtypeStruct((m, hw), BF16),
        compiler_params=_cparams(("arbitrary",)),
        name="rwkv_post",
    )(y, bonus, g, lnx_w.reshape(1, hw), lnx_b.reshape(1, hw))


def _lru_kernel(gate_ref, xb_ref, cw_ref, cb_ref, wr_ref, br_ref, wi_ref, bi_ref, lam_ref,
                o_ref, halo_ref, h_ref, a_scr, b_scr, *, tc):
    t_idx = pl.program_id(1)
    width = xb_ref.shape[1]
    nblk = tc // SUBLANE
    n_gate_blocks = width // LRU_BLOCK

    @pl.when(t_idx == 0)
    def _():
        halo_ref[...] = jnp.zeros_like(halo_ref)
        h_ref[...] = jnp.zeros_like(h_ref)

    xb = xb_ref[...].astype(F32)
    ext = jnp.concatenate([halo_ref[...], xb], axis=0)
    xc = xb * cw_ref[CONV_WIDTH - 1:CONV_WIDTH, :] + cb_ref[...]
    for d in range(1, CONV_WIDTH):
        sh = pltpu.roll(ext, d, axis=0)[SUBLANE:, :]
        xc = xc + sh * cw_ref[CONV_WIDTH - 1 - d:CONV_WIDTH - d, :]
    halo_ref[...] = xb[tc - SUBLANE:, :]

    sp = _softplus(-lam_ref[...])

    for n in range(n_gate_blocks):
        lanes = slice(n * LRU_BLOCK, (n + 1) * LRU_BLOCK)
        xn = xc[:, lanes]
        xnb = xn.astype(BF16)
        gr = jnp.dot(xnb, wr_ref[n].astype(BF16), preferred_element_type=F32) + br_ref[:, lanes]
        gi = jnp.dot(xnb, wi_ref[n].astype(BF16), preferred_element_type=F32) + bi_ref[:, lanes]
        log_a = -LRU_C * _sigmoid(gr) * sp[:, lanes]
        a = jnp.exp(log_a)
        mult = jnp.sqrt(-jnp.tanh(log_a) * (a * a + 1.0))
        a_scr[:, lanes] = a
        b_scr[:, lanes] = mult * _sigmoid(gi) * xn

    grp = LRU_SCAN_LANES
    row = lax.broadcasted_iota(jnp.int32, (SUBLANE, grp), 0)

    def blk(bidx, carry):
        rows = pl.ds(pl.multiple_of(bidx * SUBLANE, SUBLANE), SUBLANE)
        for g in range(width // grp):
            lanes = slice(g * grp, (g + 1) * grp)
            av = a_scr[rows, lanes]
            bv = b_scr[rows, lanes]
            for d in (1, 2, 4):
                keep = row >= d
                ash = jnp.where(keep, pltpu.roll(av, d, axis=0), 1.0)
                bsh = jnp.where(keep, pltpu.roll(bv, d, axis=0), 0.0)
                bv = bv + av * bsh
                av = av * ash
            hv = bv + av * h_ref[:, lanes]
            b_scr[rows, lanes] = hv
            h_ref[:, lanes] = jnp.broadcast_to(hv[SUBLANE - 1:SUBLANE, :], (SUBLANE, grp))
        return carry

    lax.fori_loop(0, nblk, blk, 0)
    o_ref[...] = (b_scr[...] * gate_ref[...].astype(F32)).astype(o_ref.dtype)


def lru_block(p_odd, conv_w, conv_b, w_r, b_r, w_i, b_i, lam, layer, *, batch, tc):
    m = p_odd.shape[0]
    width = lam.shape[0]
    t_len = m // batch
    nt = t_len // tc
    assert t_len % tc == 0 and p_odd.shape[1] == 2 * width
    row = pl.BlockSpec((1, width), lambda b, t: (0, 0))
    blkw = pl.BlockSpec((None,) + w_r.shape[1:], lambda b, t: (layer, 0, 0, 0))
    return pl.pallas_call(
        functools.partial(_lru_kernel, tc=tc),
        grid=(batch, nt),
        in_specs=[
            pl.BlockSpec((tc, width), lambda b, t: (b * nt + t, 0)),
            pl.BlockSpec((tc, width), lambda b, t: (b * nt + t, 1)),
            pl.BlockSpec((CONV_WIDTH, width), lambda b, t: (0, 0)),
            row, blkw, row, blkw, row, row,
        ],
        out_specs=pl.BlockSpec((tc, width), lambda b, t: (b * nt + t, 0)),
        out_shape=jax.ShapeDtypeStruct((m, width), BF16),
        scratch_shapes=[
            pltpu.VMEM((SUBLANE, width), F32), pltpu.VMEM((SUBLANE, width), F32),
            pltpu.VMEM((tc, width), F32), pltpu.VMEM((tc, width), F32),
        ],
        compiler_params=_cparams(("arbitrary", "arbitrary")),
        name="lru_block",
    )(p_odd, p_odd, conv_w, conv_b.reshape(1, width), w_r, b_r.reshape(1, width),
      w_i, b_i.reshape(1, width), lam.reshape(1, width))


def _s5_block_weights(bb_re, bb_im, c_re, c_im):
    c, g, p = bb_re.shape
    gpt = LANE // c
    tiles = g // gpt
    eye = jnp.eye(gpt, dtype=F32)

    def b_blk(bb):
        bb = bb.reshape(c, tiles, gpt, p)
        return jnp.einsum('cjgp,gh->jgchp', bb, eye).reshape(tiles, gpt * c, gpt * p)

    def c_blk(cm):
        cm = cm.reshape(tiles, gpt, c, p)
        return jnp.einsum('jgcp,gh->jhpgc', cm, eye).reshape(tiles, gpt * p, gpt * c)

    return (b_blk(bb_re).astype(BF16), b_blk(bb_im).astype(BF16),
            c_blk(c_re).astype(BF16), c_blk(c_im).astype(BF16))


def _even_mixer(x, g_pre, g_post, g_ffn, batch, idx, w_in, shift_mu, lam_re, lam_im, log_dt,
                b_re, b_im, c_re, c_im, d_skip, w_glu, w0, w2, a0, a2, g2, k_k, k_a, r_k,
                lnx_w, lnx_b, w_out):
    hw = w0.shape[0]
    s5w = d_skip.shape[0]
    n_main = s5w + 3 * hw
    p_main = norm_matmul(x, g_pre, w_in, idx, n_main, tm=1024, tn=512, out_dtype=BF16,
                         name="even_in_proj")
    lrw = w_in.shape[2] - n_main
    lr_pad = -(-lrw // LANE) * LANE
    w_lr = jnp.pad(w_in[idx, :, n_main:], ((0, 0), (0, lr_pad - lrw)))[None]
    lr = norm_matmul(x, g_pre, w_lr, 0, lr_pad, tm=1024, tn=lr_pad, out_dtype=F32,
                     name="even_lr_proj")

    a_re, a_im, bb_re, bb_im = s5_discretise(lam_re, lam_im, log_dt, b_re, b_im)
    bblk_re, bblk_im, cblk_re, cblk_im = _s5_block_weights(bb_re, bb_im, c_re, c_im)
    y_s5 = s5_scan(p_main, bblk_re, bblk_im, cblk_re, cblk_im,
                   a_re.reshape(1, -1), a_im.reshape(1, -1), d_skip.reshape(1, -1),
                   batch=batch, tc=256)
    y_s5 = s5_glu(y_s5, w_glu, idx, tm=1024, tn=512)

    r, lw, k, v, a, b, bonus, g = rwkv_prep(p_main, lr, shift_mu, w0, w2, a0, a2, g2,
                                            k_k, k_a, r_k.reshape(-1), batch=batch, tm=256)
    y_rw = rwkv_scan(r, lw, k, v, a, b, batch=batch, tb=512)
    y_rw = rwkv_post(y_rw, bonus, g, lnx_w, lnx_b, tm=512)

    return out_proj_norm_residual([y_s5, y_rw], w_out, idx, x, g_post, g_ffn, tm=512,
                                  name="even_out_proj")


def _odd_mixer(x, g_pre, g_post, g_ffn, batch, idx, w_in, conv_w, conv_b, w_r, b_r, w_i, b_i,
               lam, w_out):
    width = lam.shape[0]
    p_odd = norm_matmul(x, g_pre, w_in, idx, 2 * width, tm=1024, tn=512, out_dtype=BF16,
                        n_gelu_tiles=width // 512, name="odd_in_proj")
    hg = lru_block(p_odd, conv_w, conv_b, w_r, b_r, w_i, b_i, lam, idx, batch=batch, tc=256)
    return out_proj_norm_residual([hg], w_out, idx, x, g_post, g_ffn, tm=512,
                                  name="odd_out_proj")


def _ffn(x, h, g_post, layer, w_gate, w_up, w_down):
    act = swiglu_up(h, w_gate, w_up, layer, tm=2048, tn=512)
    return matmul_norm_residual(act, w_down, layer, x, g_post, tm=1024, tk=512, name="ffn_down")


def kernel(x, ev_w_in, ev_shift_mu, s5_lam_re, s5_lam_im, s5_log_dt, s5_b_re, s5_b_im, s5_c_re, s5_c_im, s5_d, s5_w_glu, rw_w0, rw_w2, rw_a0, rw_a2, rw_g2, rw_k_k, rw_k_a, rw_r_k, rw_lnx_w, rw_lnx_b, ev_w_out, od_w_in, od_conv_w, od_conv_b, lru_w_r, lru_b_r, lru_w_i, lru_b_i, lru_lam, od_w_out, ffn_w_gate, ffn_w_up, ffn_w_down, norm_mix_pre, norm_mix_post, norm_ffn_pre, norm_ffn_post):
    batch, t_len, d = x.shape
    depth = ffn_w_gate.shape[0]
    xf = x.reshape(batch * t_len, d)
    s5_w_glu, ev_w_out, od_w_out = (w.astype(BF16) for w in (s5_w_glu, ev_w_out, od_w_out))
    for layer in range(depth):
        i = layer // 2
        if layer % 2 == 0:
            xf, hf = _even_mixer(xf, norm_mix_pre[layer], norm_mix_post[layer],
                                 norm_ffn_pre[layer], batch, i,
                                 ev_w_in, ev_shift_mu[i], s5_lam_re[i], s5_lam_im[i],
                                 s5_log_dt[i], s5_b_re[i], s5_b_im[i], s5_c_re[i], s5_c_im[i],
                                 s5_d[i], s5_w_glu, rw_w0[i], rw_w2[i], rw_a0[i], rw_a2[i],
                                 rw_g2[i], rw_k_k[i], rw_k_a[i], rw_r_k[i], rw_lnx_w[i],
                                 rw_lnx_b[i], ev_w_out)
        else:
            xf, hf = _odd_mixer(xf, norm_mix_pre[layer], norm_mix_post[layer],
                                norm_ffn_pre[layer], batch, i,
                                od_w_in, od_conv_w[i], od_conv_b[i], lru_w_r, lru_b_r[i],
                                lru_w_i, lru_b_i[i], lru_lam[i], od_w_out)
        xf = _ffn(xf, hf, norm_ffn_post[layer], layer, ffn_w_gate, ffn_w_up, ffn_w_down)
    return xf.reshape(batch, t_len, d)
```

```python
import functools
import math

import jax
import jax.numpy as jnp
from jax import lax
from jax.experimental import pallas as pl
from jax.experimental.pallas import tpu as pltpu

F32 = jnp.float32
BF16 = jnp.bfloat16

NORM_EPS = 1e-6
GN_EPS = 64e-5
LRU_C = 8.0
S5_GROUP = 16
S5_STATE = 64
RWKV_HEAD = 64
CONV_WIDTH = 4
LRU_BLOCK = 256

LANE = 128
SUBLANE = 8
VMEM_LIMIT = 56 * 1024 * 1024

RW_CHUNK = 64
RW_PAIRS = 2
S5_BLOCK = 8
LRU_SCAN_LANES = 512


def _cparams(sem):
    return pltpu.CompilerParams(dimension_semantics=sem, vmem_limit_bytes=VMEM_LIMIT)


def _gelu(x):
    c = math.sqrt(2.0 / math.pi)
    return 0.5 * x * (1.0 + jnp.tanh(c * (x + 0.044715 * (x * x * x))))


def _sigmoid(x):
    return 1.0 / (1.0 + jnp.exp(-x))


def _softplus(x):
    return jnp.maximum(x, 0.0) + jnp.log(1.0 + jnp.exp(-jnp.abs(x)))


def _rms_rows(y, g):
    ms = jnp.mean(y * y, axis=-1, keepdims=True)
    return y * lax.rsqrt(ms + NORM_EPS) * g


def _rmsnorm_kernel(x_ref, g_ref, o_ref):
    o_ref[...] = _rms_rows(x_ref[...], g_ref[...]).astype(o_ref.dtype)


def rmsnorm(x, g, *, tm):
    m, d = x.shape
    assert m % tm == 0
    return pl.pallas_call(
        _rmsnorm_kernel,
        grid=(m // tm,),
        in_specs=[pl.BlockSpec((tm, d), lambda i: (i, 0)), pl.BlockSpec((1, d), lambda i: (0, 0))],
        out_specs=pl.BlockSpec((tm, d), lambda i: (i, 0)),
        out_shape=jax.ShapeDtypeStruct((m, d), BF16),
        compiler_params=_cparams(("arbitrary",)),
        name="rmsnorm",
    )(x, g.reshape(1, d))


def _in_proj_kernel(h_ref, w_ref, o_ref, *, n_gelu_tiles):
    j = pl.program_id(1)
    acc = jnp.dot(h_ref[...], w_ref[...].astype(BF16), preferred_element_type=F32)
    if n_gelu_tiles == 0:
        o_ref[...] = acc.astype(o_ref.dtype)
    else:
        @pl.when(j < n_gelu_tiles)
        def _():
            o_ref[...] = _gelu(acc).astype(o_ref.dtype)

        @pl.when(j >= n_gelu_tiles)
        def _():
            o_ref[...] = acc.astype(o_ref.dtype)


def in_proj(h, w, layer, n_out, *, tm, tn, out_dtype, n_gelu_tiles=0, name):
    m, d = h.shape
    assert m % tm == 0 and n_out % tn == 0 and w.shape[1] == d and h.dtype == BF16
    return pl.pallas_call(
        functools.partial(_in_proj_kernel, n_gelu_tiles=n_gelu_tiles),
        grid=(m // tm, n_out // tn),
        in_specs=[
            pl.BlockSpec((tm, d), lambda i, j: (i, 0)),
            pl.BlockSpec((None, d, tn), lambda i, j: (layer, 0, j)),
        ],
        out_specs=pl.BlockSpec((tm, tn), lambda i, j: (i, j)),
        out_shape=jax.ShapeDtypeStruct((m, n_out), out_dtype),
        compiler_params=_cparams(("arbitrary", "arbitrary")),
        name=name,
    )(h, w)


def _swiglu_up_kernel(h_ref, wg_ref, wu_ref, o_ref):
    h = h_ref[...]
    gate = jnp.dot(h, wg_ref[...].astype(BF16), preferred_element_type=F32)
    up = jnp.dot(h, wu_ref[...].astype(BF16), preferred_element_type=F32)
    o_ref[...] = (gate * _sigmoid(gate) * up).astype(o_ref.dtype)


def swiglu_up(h, w_gate, w_up, layer, *, tm, tn):
    m, d = h.shape
    n = w_gate.shape[2]
    assert m % tm == 0 and n % tn == 0 and h.dtype == BF16
    return pl.pallas_call(
        _swiglu_up_kernel,
        grid=(m // tm, n // tn),
        in_specs=[
            pl.BlockSpec((tm, d), lambda i, j: (i, 0)),
            pl.BlockSpec((None, d, tn), lambda i, j: (layer, 0, j)),
            pl.BlockSpec((None, d, tn), lambda i, j: (layer, 0, j)),
        ],
        out_specs=pl.BlockSpec((tm, tn), lambda i, j: (i, j)),
        out_shape=jax.ShapeDtypeStruct((m, n), BF16),
        compiler_params=_cparams(("arbitrary", "arbitrary")),
        name="ffn_up",
    )(h, w_gate, w_up)


def _mm_norm_res_kernel(a_ref, w_ref, x_ref, g_ref, *rest, emit_next, n_chunk):
    if emit_next:
        gn_ref, o_ref, hn_ref = rest
    else:
        (o_ref,) = rest
    k = pl.program_id(1)
    d = o_ref.shape[1]

    def accumulate(first):
        a = a_ref[...]
        for n in range(0, d, n_chunk):
            cols = slice(n, n + n_chunk)
            part = jnp.dot(a, w_ref[:, cols].astype(BF16), preferred_element_type=F32)
            if first:
                o_ref[:, cols] = part
            else:
                o_ref[:, cols] += part

    @pl.when(k == 0)
    def _():
        accumulate(True)

    @pl.when(k > 0)
    def _():
        accumulate(False)

    @pl.when(k == pl.num_programs(1) - 1)
    def _():
        x_new = x_ref[...] + _rms_rows(o_ref[...], g_ref[...])
        o_ref[...] = x_new
        if emit_next:
            hn_ref[...] = _rms_rows(x_new, gn_ref[...]).astype(hn_ref.dtype)


def matmul_norm_residual(a, w, layer, x, g, g_next, *, tm, tk, n_chunk, name):
    m, kdim = a.shape
    d = w.shape[2]
    assert m % tm == 0 and kdim % tk == 0 and w.shape[1] == kdim and d % n_chunk == 0
    emit_next = g_next is not None
    row = pl.BlockSpec((1, d), lambda i, k: (0, 0))
    blk = pl.BlockSpec((tm, d), lambda i, k: (i, 0))
    f32_out = jax.ShapeDtypeStruct((m, d), F32)
    operands = [a, w, x, g.reshape(1, d)] + ([g_next.reshape(1, d)] if emit_next else [])
    return pl.pallas_call(
        functools.partial(_mm_norm_res_kernel, emit_next=emit_next, n_chunk=n_chunk),
        grid=(m // tm, kdim // tk),
        in_specs=[
            pl.BlockSpec((tm, tk), lambda i, k: (i, k)),
            pl.BlockSpec((None, tk, d), lambda i, k: (layer, k, 0)),
            blk, row,
        ] + ([row] if emit_next else []),
        out_specs=[blk, blk] if emit_next else blk,
        out_shape=(f32_out, jax.ShapeDtypeStruct((m, d), BF16)) if emit_next else f32_out,
        compiler_params=_cparams(("arbitrary", "arbitrary")),
        name=name,
    )(*operands)


def _out_proj_kernel(*refs, n_lhs):
    a_refs = refs[:n_lhs]
    w_ref, x_ref, g_ref, gn_ref, o_ref, hn_ref = refs[n_lhs:]
    y = None
    k0 = 0
    for a_ref in a_refs:
        kw = a_ref.shape[1]
        part = jnp.dot(a_ref[...], w_ref[k0:k0 + kw, :], preferred_element_type=F32)
        y = part if y is None else y + part
        k0 += kw
    x_new = x_ref[...] + _rms_rows(y, g_ref[...])
    o_ref[...] = x_new
    hn_ref[...] = _rms_rows(x_new, gn_ref[...]).astype(hn_ref.dtype)


def out_proj_norm_residual(lhs, w, layer, x, g, g_next, *, tm, name):
    m = x.shape[0]
    kdim, d = w.shape[1], w.shape[2]
    assert sum(a.shape[1] for a in lhs) == kdim and m % tm == 0
    row = pl.BlockSpec((1, d), lambda i: (0, 0))
    return pl.pallas_call(
        functools.partial(_out_proj_kernel, n_lhs=len(lhs)),
        grid=(m // tm,),
        in_specs=[pl.BlockSpec((tm, a.shape[1]), lambda i: (i, 0)) for a in lhs] + [
            pl.BlockSpec((None, kdim, d), lambda i: (layer, 0, 0)),
            pl.BlockSpec((tm, d), lambda i: (i, 0)),
            row, row,
        ],
        out_specs=[pl.BlockSpec((tm, d), lambda i: (i, 0))] * 2,
        out_shape=(jax.ShapeDtypeStruct((m, d), F32), jax.ShapeDtypeStruct((m, d), BF16)),
        compiler_params=_cparams(("arbitrary",)),
        name=name,
    )(*lhs, w, x, g.reshape(1, d), g_next.reshape(1, d))


def _block_ones(n, seg):
    r = lax.broadcasted_iota(jnp.int32, (n, n), 0) // seg
    c = lax.broadcasted_iota(jnp.int32, (n, n), 1) // seg
    return (r == c).astype(BF16)


def _seg_sum(x, ones):
    outs = []
    for j in range(x.shape[1] // LANE):
        xj = x[:, j * LANE:(j + 1) * LANE]
        hi = xj.astype(BF16)
        r1 = xj - hi.astype(F32)
        mid = r1.astype(BF16)
        lo = (r1 - mid.astype(F32)).astype(BF16)
        s = (jnp.dot(hi, ones, preferred_element_type=F32)
             + jnp.dot(mid, ones, preferred_element_type=F32)
             + jnp.dot(lo, ones, preferred_element_type=F32))
        outs.append(s)
    return jnp.concatenate(outs, axis=1) if len(outs) > 1 else outs[0]


def _s5_disc_kernel(lr_ref, li_ref, ldt_ref, bre_ref, bim_ref,
                    are_ref, aim_ref, bbre_ref, bbim_ref):
    lr = lr_ref[...]
    li = li_ref[...]
    dt = jnp.exp(ldt_ref[...])
    mag = jnp.exp(lr * dt)
    a_re = mag * jnp.cos(li * dt)
    a_im = mag * jnp.sin(li * dt)
    den = lr * lr + li * li
    nr = a_re - 1.0
    ni = a_im
    gam_re = (nr * lr + ni * li) / den
    gam_im = (ni * lr - nr * li) / den
    are_ref[...] = a_re
    aim_ref[...] = a_im
    for c in range(bre_ref.shape[0]):
        br = bre_ref[c]
        bi = bim_ref[c]
        bbre_ref[c] = gam_re * br - gam_im * bi
        bbim_ref[c] = gam_re * bi + gam_im * br


def s5_discretise(lam_re, lam_im, log_dt, b_re, b_im):
    g, p, c = b_re.shape
    ldt = jnp.broadcast_to(log_dt[:, None], (g, p))
    b_re_t = jnp.transpose(b_re, (2, 0, 1))
    b_im_t = jnp.transpose(b_im, (2, 0, 1))
    gp = jax.ShapeDtypeStruct((g, p), F32)
    cgp = jax.ShapeDtypeStruct((c, g, p), F32)
    return pl.pallas_call(
        _s5_disc_kernel,
        out_shape=(gp, gp, cgp, cgp),
        name="s5_discretise",
    )(lam_re, lam_im, ldt, b_re_t, b_im_t)


def _cmul(ar, ai, br, bi):
    return ar * br - ai * bi, ar * bi + ai * br


def _s5_tables_kernel(bre_ref, bim_ref, cre_ref, cim_ref, ar_ref, ai_ref, acr_ref, aci_ref, d_ref,
                      wk_ref, wb_ref, wc_ref, a8r_ref, a8i_ref):
    bre, bim = bre_ref[...], bim_ref[...]
    cre, cim = cre_ref[...], cim_ref[...]
    ar, ai = ar_ref[...], ai_ref[...]
    acr, aci = acr_ref[...], aci_ref[...]
    ml = bre.shape[1]
    eye = (lax.broadcasted_iota(jnp.int32, (LANE, LANE), 0)
           == lax.broadcasted_iota(jnp.int32, (LANE, LANE), 1)).astype(F32)

    pr, pi = [jnp.ones_like(ar)], [jnp.zeros_like(ai)]
    for _ in range(S5_BLOCK):
        nr, ni = _cmul(pr[-1], pi[-1], ar, ai)
        pr.append(nr)
        pi.append(ni)
    qr, qi = [acr], [aci]
    for _ in range(S5_BLOCK - 1):
        nr, ni = _cmul(qr[-1], qi[-1], acr, aci)
        qr.append(nr)
        qi.append(ni)

    hi = lax.Precision.HIGHEST
    for tau in range(S5_BLOCK):
        mre = bre * pr[tau] - bim * pi[tau]
        mim = bre * pi[tau] + bim * pr[tau]
        k = (jnp.dot(mre, cre, preferred_element_type=F32, precision=hi)
             - jnp.dot(mim, cim, preferred_element_type=F32, precision=hi))
        if tau == 0:
            k = k + eye * d_ref[...]
        wk_ref[tau * LANE:(tau + 1) * LANE, :] = k.astype(wk_ref.dtype)
        r = S5_BLOCK - 1 - tau
        wb_ref[r * LANE:(r + 1) * LANE, :ml] = mre.astype(wb_ref.dtype)
        wb_ref[r * LANE:(r + 1) * LANE, ml:] = mim.astype(wb_ref.dtype)

    for r in range(S5_BLOCK):
        dr, di = qr[r], qi[r]
        wc_ref[:ml, r * LANE:(r + 1) * LANE] = (cre * dr - cim * di).astype(wc_ref.dtype)
        wc_ref[ml:, r * LANE:(r + 1) * LANE] = (-(cre * di + cim * dr)).astype(wc_ref.dtype)

    a8r_ref[...] = pr[S5_BLOCK]
    a8i_ref[...] = pi[S5_BLOCK]


def s5_tables(bblk_re, bblk_im, cblk_re, cblk_im, a_re, a_im, d_skip):
    tiles, _, ml = bblk_re.shape
    a_col_re = jnp.broadcast_to(jnp.swapaxes(a_re, 1, 2), (tiles, ml, LANE))
    a_col_im = jnp.broadcast_to(jnp.swapaxes(a_im, 1, 2), (tiles, ml, LANE))
    kdim = S5_BLOCK * LANE
    t3 = lambda a, b: pl.BlockSpec((None, a, b), lambda j: (j, 0, 0))
    return pl.pallas_call(
        _s5_tables_kernel,
        grid=(tiles,),
        in_specs=[t3(LANE, ml), t3(LANE, ml), t3(ml, LANE), t3(ml, LANE), t3(1, ml), t3(1, ml),
                  t3(ml, LANE), t3(ml, LANE), t3(1, LANE)],
        out_specs=[t3(kdim, LANE), t3(kdim, 2 * ml), t3(2 * ml, kdim), t3(1, ml), t3(1, ml)],
        out_shape=(jax.ShapeDtypeStruct((tiles, kdim, LANE), BF16),
                   jax.ShapeDtypeStruct((tiles, kdim, 2 * ml), BF16),
                   jax.ShapeDtypeStruct((tiles, 2 * ml, kdim), BF16),
                   jax.ShapeDtypeStruct((tiles, 1, ml), F32),
                   jax.ShapeDtypeStruct((tiles, 1, ml), F32)),
        compiler_params=_cparams(("arbitrary",)),
        name="s5_tables",
    )(bblk_re, bblk_im, cblk_re, cblk_im, a_re, a_im, a_col_re, a_col_im, d_skip)


def _s5_mix_kernel(u_ref, wk_ref, wb_ref, wc_ref, a8r_ref, a8i_ref, o_ref,
                   car_ref, cai_ref, tab_ref, uf_ref, xr_ref, xi_ref, y_ref, *, tc):
    t_idx = pl.program_id(2)
    ml = a8r_ref.shape[1]
    nb = tc // S5_BLOCK
    row = lax.broadcasted_iota(jnp.int32, (SUBLANE, ml), 0)

    @pl.when(t_idx == 0)
    def _():
        car_ref[...] = jnp.zeros_like(car_ref)
        cai_ref[...] = jnp.zeros_like(cai_ref)
        a1r = jnp.broadcast_to(a8r_ref[...], (SUBLANE, ml))
        a1i = jnp.broadcast_to(a8i_ref[...], (SUBLANE, ml))
        a2r, a2i = _cmul(a1r, a1i, a1r, a1i)
        a4r, a4i = _cmul(a2r, a2i, a2r, a2i)
        pr, pi = a1r, a1i
        cr, ci = a1r, a1i
        for r in range(1, SUBLANE):
            cr, ci = _cmul(cr, ci, a1r, a1i)
            pr = jnp.where(row >= r, cr, pr)
            pi = jnp.where(row >= r, ci, pi)
        for lvl, (d, xr, xi) in enumerate(((1, a1r, a1i), (2, a2r, a2i), (4, a4r, a4i))):
            tab_ref[2 * lvl] = jnp.where(row >= d, xr, 0.0)
            tab_ref[2 * lvl + 1] = jnp.where(row >= d, xi, 0.0)
        tab_ref[6] = pr
        tab_ref[7] = pi

    uf = u_ref[...].astype(F32)
    uf_ref[...] = uf

    pos = lax.broadcasted_iota(jnp.int32, (tc, LANE), 0) % S5_BLOCK
    shifted = [uf.astype(BF16)]
    for tau in range(1, S5_BLOCK):
        shifted.append(jnp.where(pos >= tau, pltpu.roll(uf, tau, axis=0), 0.0).astype(BF16))
    y_ref[...] = jnp.dot(jnp.concatenate(shifted, axis=1), wk_ref[...], preferred_element_type=F32)

    strided = [uf_ref[pl.ds(r, nb, stride=S5_BLOCK), :].astype(BF16) for r in range(S5_BLOCK)]
    x = jnp.dot(jnp.concatenate(strided, axis=1), wb_ref[...], preferred_element_type=F32)
    xr_ref[...] = x[:, :ml]
    xi_ref[...] = x[:, ml:]

    def blk(b, carry):
        rows = pl.ds(pl.multiple_of(b * SUBLANE, SUBLANE), SUBLANE)
        sr = xr_ref[rows, :]
        si = xi_ref[rows, :]
        for lvl, d in enumerate((1, 2, 4)):
            kr = tab_ref[2 * lvl]
            ki = tab_ref[2 * lvl + 1]
            rr = pltpu.roll(sr, d, axis=0)
            ri = pltpu.roll(si, d, axis=0)
            sr, si = sr + kr * rr - ki * ri, si + kr * ri + ki * rr
        pr = tab_ref[6]
        pi = tab_ref[7]
        cbr = car_ref[...]
        cbi = cai_ref[...]
        sr = sr + pr * cbr - pi * cbi
        si = si + pr * cbi + pi * cbr
        xr_ref[rows, :] = jnp.where(row == 0, cbr, pltpu.roll(sr, 1, axis=0))
        xi_ref[rows, :] = jnp.where(row == 0, cbi, pltpu.roll(si, 1, axis=0))
        car_ref[...] = jnp.broadcast_to(sr[SUBLANE - 1:SUBLANE, :], (SUBLANE, ml))
        cai_ref[...] = jnp.broadcast_to(si[SUBLANE - 1:SUBLANE, :], (SUBLANE, ml))
        return carry

    lax.fori_loop(0, nb // SUBLANE, blk, 0)

    s_in = jnp.concatenate([xr_ref[...], xi_ref[...]], axis=1).astype(BF16)
    z = jnp.dot(s_in, wc_ref[...], preferred_element_type=F32)
    for r in range(S5_BLOCK):
        rows = pl.ds(r, nb, stride=S5_BLOCK)
        y_ref[rows, :] = y_ref[rows, :] + z[:, r * LANE:(r + 1) * LANE]
    o_ref[...] = _gelu(y_ref[...])


def s5_mix(p_main, wk, wb, wc, a8r, a8i, *, batch, tc):
    m = p_main.shape[0]
    t_len = m // batch
    tiles, kdim, _ = wk.shape
    ml = a8r.shape[2]
    nt = t_len // tc
    nb = tc // S5_BLOCK
    assert t_len % tc == 0 and nb % SUBLANE == 0 and p_main.dtype == BF16
    t3 = lambda a, b: pl.BlockSpec((None, a, b), lambda bi, j, t: (j, 0, 0))
    io = pl.BlockSpec((tc, LANE), lambda bi, j, t: (bi * nt + t, j))
    return pl.pallas_call(
        functools.partial(_s5_mix_kernel, tc=tc),
        grid=(batch, tiles, nt),
        in_specs=[io, t3(kdim, LANE), t3(kdim, 2 * ml), t3(2 * ml, kdim), t3(1, ml), t3(1, ml)],
        out_specs=io,
        out_shape=jax.ShapeDtypeStruct((m, tiles * LANE), F32),
        scratch_shapes=[
            pltpu.VMEM((SUBLANE, ml), F32), pltpu.VMEM((SUBLANE, ml), F32),
            pltpu.VMEM((8, SUBLANE, ml), F32),
            pltpu.VMEM((tc, LANE), F32),
            pltpu.VMEM((nb, ml), F32), pltpu.VMEM((nb, ml), F32),
            pltpu.VMEM((tc, LANE), F32),
        ],
        compiler_params=_cparams(("arbitrary", "arbitrary", "arbitrary")),
        name="s5_mix",
    )(p_main, wk, wb, wc, a8r, a8i)


def _s5_glu_kernel(y_ref, yt_ref, w_ref, o_ref):
    z = jnp.dot(y_ref[...].astype(BF16), w_ref[...], preferred_element_type=F32)
    o_ref[...] = (yt_ref[...] * _sigmoid(z)).astype(o_ref.dtype)


def s5_glu(y, w, layer, *, tm, tn):
    m, d = y.shape
    return pl.pallas_call(
        _s5_glu_kernel,
        grid=(m // tm, d // tn),
        in_specs=[
            pl.BlockSpec((tm, d), lambda i, j: (i, 0)),
            pl.BlockSpec((tm, tn), lambda i, j: (i, j)),
            pl.BlockSpec((None, d, tn), lambda i, j: (layer, 0, j)),
        ],
        out_specs=pl.BlockSpec((tm, tn), lambda i, j: (i, j)),
        out_shape=jax.ShapeDtypeStruct((m, d), BF16),
        compiler_params=_cparams(("arbitrary", "arbitrary")),
        name="s5_glu",
    )(y, y, w)


def _shift_mix(z, prev_row, mu):
    row = lax.broadcasted_iota(jnp.int32, z.shape, 0)
    zs = jnp.where(row == 0, jnp.broadcast_to(prev_row, z.shape), pltpu.roll(z, 1, axis=0))
    return z + (zs - z) * mu


def _rwkv_prep_kernel(r_ref, k_ref, v_ref, lr_ref, rp_ref, kp_ref, vp_ref, lrp_ref,
                      mur_ref, muk_ref, muv_ref, mulr_ref,
                      w0_ref, w2_ref, a0_ref, a2_ref, g2_ref, kk_ref, ka_ref, rk_ref,
                      ro_ref, lw_ref, ko_ref, vo_ref, ao_ref, bo_ref, bon_ref, go_ref,
                      *, tiles_per_batch):
    i = pl.program_id(0)
    first = (i % tiles_per_batch) == 0

    def prev(ref):
        last = ref.shape[0] - 1
        return jnp.where(first, 0.0, ref[last:last + 1, :].astype(F32))

    r = _shift_mix(r_ref[...].astype(F32), prev(rp_ref), mur_ref[...])
    k = _shift_mix(k_ref[...].astype(F32), prev(kp_ref), muk_ref[...])
    v = _shift_mix(v_ref[...].astype(F32), prev(vp_ref), muv_ref[...])
    lr = _shift_mix(lr_ref[...], prev(lrp_ref), mulr_ref[...])

    wl = w0_ref[...] + jnp.dot(jnp.tanh(lr).astype(BF16), w2_ref[...].astype(BF16),
                               preferred_element_type=F32)
    w = -_softplus(-wl) - 0.5
    lw_ref[...] = -jnp.exp(w)
    a = _sigmoid(a0_ref[...] + jnp.dot(lr.astype(BF16), a2_ref[...].astype(BF16),
                                       preferred_element_type=F32))
    go_ref[...] = jnp.dot(_sigmoid(lr).astype(BF16), g2_ref[...].astype(BF16),
                          preferred_element_type=F32).astype(go_ref.dtype)

    ones = _block_ones(LANE, RWKV_HEAD)
    kk = k * kk_ref[...]
    kk = kk * lax.rsqrt(jnp.maximum(_seg_sum(kk * kk, ones), 1e-24))
    k = k * (1.0 + (a - 1.0) * ka_ref[...])
    ro_ref[...] = r.astype(ro_ref.dtype)
    ko_ref[...] = k.astype(ko_ref.dtype)
    vo_ref[...] = v.astype(vo_ref.dtype)
    ao_ref[...] = (-kk).astype(ao_ref.dtype)
    bo_ref[...] = (kk * a).astype(bo_ref.dtype)
    bon_ref[...] = (_seg_sum(r * k * rk_ref[...], ones) * v).astype(bon_ref.dtype)


def rwkv_prep(p_main, lr, mu, w0, w2, a0, a2, g2, k_k, k_a, r_k, *, batch, tm):
    m = p_main.shape[0]
    t_len = m // batch
    hw = w0.shape[0]
    lrw = lr.shape[1]
    w_rank, a_rank, g_rank = w2.shape[0], a2.shape[0], g2.shape[0]
    nblk8 = tm // SUBLANE
    halo = 2 * SUBLANE
    nblk16 = tm // halo
    assert t_len % tm == 0 and p_main.shape[1] == 4 * hw and p_main.dtype == BF16

    def cur(c):
        return pl.BlockSpec((tm, hw), lambda i: (i, c))

    def prv(c):
        return pl.BlockSpec((halo, hw), lambda i: (jnp.maximum(i * nblk16 - 1, 0), c))

    row = lambda n: pl.BlockSpec((1, n), lambda i: (0, 0))
    mat = lambda a: pl.BlockSpec(a.shape, lambda i: (0, 0))
    used = w_rank + a_rank + g_rank
    mu_lr = jnp.pad(mu[3 * hw:], (0, lrw - used))
    w2 = jnp.pad(w2, ((0, lrw - w_rank), (0, 0)))
    a2 = jnp.pad(a2, ((w_rank, lrw - w_rank - a_rank), (0, 0)))
    g2 = jnp.pad(g2, ((w_rank + a_rank, lrw - used), (0, 0)))
    out = jax.ShapeDtypeStruct((m, hw), BF16)
    out_lw = jax.ShapeDtypeStruct((m, hw), F32)
    return pl.pallas_call(
        functools.partial(_rwkv_prep_kernel, tiles_per_batch=t_len // tm),
        grid=(m // tm,),
        in_specs=[
            cur(1), cur(2), cur(3), pl.BlockSpec((tm, lrw), lambda i: (i, 0)),
            prv(1), prv(2), prv(3),
            pl.BlockSpec((SUBLANE, lrw), lambda i: (jnp.maximum(i * nblk8 - 1, 0), 0)),
            row(hw), row(hw), row(hw), row(lrw),
            row(hw), mat(w2), row(hw), mat(a2), mat(g2), row(hw), row(hw), row(hw),
        ],
        out_specs=[pl.BlockSpec((tm, hw), lambda i: (i, 0))] * 8,
        out_shape=(out, out_lw) + (out,) * 6,
        compiler_params=_cparams(("arbitrary",)),
        name="rwkv_prep",
    )(p_main, p_main, p_main, lr, p_main, p_main, p_main, lr,
      mu[:hw].reshape(1, hw), mu[hw:2 * hw].reshape(1, hw), mu[2 * hw:3 * hw].reshape(1, hw),
      mu_lr.reshape(1, lrw),
      w0.reshape(1, hw), w2, a0.reshape(1, hw), a2, g2, k_k.reshape(1, hw),
      k_a.reshape(1, hw), r_k.reshape(1, hw))


def _mm(a, b):
    return jnp.dot(a.astype(BF16), b.astype(BF16), preferred_element_type=F32)


def _mm_nt(a, b):
    return lax.dot_general(a.astype(BF16), b.astype(BF16), (((1,), (1,)), ((), ())),
                           preferred_element_type=F32)


def _mm_tn(a, b):
    return lax.dot_general(a.astype(BF16), b.astype(BF16), (((0,), (0,)), ((), ())),
                           preferred_element_type=F32)


def _split3(x):
    hi = x.astype(BF16)
    r1 = x - hi.astype(F32)
    mid = r1.astype(BF16)
    lo = (r1 - mid.astype(F32)).astype(BF16)
    return hi, mid, lo


def _rwkv_scan_kernel(r_ref, lw_ref, k_ref, v_ref, a_ref, b_ref, y_ref,
                      z_ref, m_scr, zc_scr, ra_scr, yv_scr, ge_scr, *, n_chunks):
    c_len = RW_CHUNK
    n2 = 2 * c_len

    @pl.when(pl.program_id(2) == 0)
    def _():
        z_ref[...] = jnp.zeros_like(z_ref)

    lane = lax.broadcasted_iota(jnp.int32, (c_len, LANE), 1)
    head_a = lane < RWKV_HEAD
    ri = lax.broadcasted_iota(jnp.int32, (n2, n2), 0)
    ci = lax.broadcasted_iota(jnp.int32, (n2, n2), 1)
    same = (ri // c_len) == (ci // c_len)
    strict = same & ((ci % c_len) < (ri % c_len))
    incl = same & ((ci % c_len) <= (ri % c_len))
    eye = (ri == ci).astype(F32)
    rc = lax.broadcasted_iota(jnp.int32, (c_len, c_len), 0)
    cc = lax.broadcasted_iota(jnp.int32, (c_len, c_len), 1)
    tri = (cc <= rc).astype(BF16)

    def stack(x):
        return jnp.concatenate([jnp.where(head_a, x, 0.0), jnp.where(head_a, 0.0, x)], axis=0)

    def fold(x):
        return x[:c_len, :] + x[c_len:, :]

    units = [(pp, c) for pp in range(RW_PAIRS) for c in range(n_chunks)]

    def window(pp, c):
        return slice(c * c_len, (c + 1) * c_len), slice(pp * LANE, (pp + 1) * LANE)

    pre = []
    for u, (pp, c) in enumerate(units):
        rows, lanes = window(pp, c)
        lw = lw_ref[rows, lanes]
        r, k, v, a, b = (ref[rows, lanes].astype(F32) for ref in (r_ref, k_ref, v_ref, a_ref, b_ref))
        cl = sum(jnp.dot(tri, part, preferred_element_type=F32) for part in _split3(lw))
        cl_end = cl[c_len - 1:c_len, :]
        g_inv = jnp.exp(-cl)
        g_tail = jnp.exp(cl_end - cl)
        r_t = r * jnp.exp(cl)
        ra_scr[u] = r_t
        ge_scr[u] = jnp.broadcast_to(jnp.exp(cl_end), (SUBLANE, LANE))
        pre.append(dict(
            a_s=stack(a * jnp.exp(cl - lw)).astype(BF16), r_s=stack(r_t).astype(BF16),
            b_s=stack(b * g_inv).astype(BF16), k_s=stack(k * g_inv).astype(BF16),
            v_s=stack(v).astype(BF16), bg_s=stack(b * g_tail).astype(BF16),
            kg_s=stack(k * g_tail).astype(BF16)))

    scs = [_mm_nt(jnp.concatenate([p["a_s"], p["r_s"]], axis=0),
                  jnp.concatenate([p["b_s"], p["k_s"]], axis=0)) for p in pre]
    l_ak = [jnp.where(strict, sc[:n2, n2:], 0.0).astype(BF16) for sc in scs]
    m_rb = [jnp.where(incl, sc[n2:, :n2], 0.0).astype(BF16) for sc in scs]
    m_rk = [jnp.where(incl, sc[n2:, n2:], 0.0).astype(BF16) for sc in scs]

    xs = [jnp.where(strict, sc[:n2, :n2], 0.0) for sc in scs]
    ts = [eye + x for x in xs]
    n = 2
    while n < c_len:
        xs = [_mm(x, x) for x in xs]
        ts = [t + _mm(t, x) for t, x in zip(ts, xs)]
        n *= 2

    lvs = [_mm(l, p["v_s"]) for l, p in zip(l_ak, pre)]
    tatv = [_mm(t, jnp.concatenate([p["a_s"], lv.astype(BF16)], axis=1)).astype(BF16)
            for t, p, lv in zip(ts, pre, lvs)]
    mtatv = [_mm(m, tv) for m, tv in zip(m_rb, tatv)]
    mkv = [_mm(m, p["v_s"]) for m, p in zip(m_rk, pre)]
    gz = [_mm_tn(tv, p["bg_s"]) for tv, p in zip(tatv, pre)]
    vk = [_mm_tn(p["v_s"], p["kg_s"]) for p in pre]
    for u in range(len(units)):
        ra_scr[u] = ra_scr[u] + fold(mtatv[u][:, :LANE])
        yv_scr[u] = fold(mtatv[u][:, LANE:] + mkv[u])
        m_scr[u] = gz[u][:LANE, :]
        zc_scr[u] = gz[u][LANE:, :] + vk[u]

    zts = [z_ref[pp] for pp in range(RW_PAIRS)]
    for c in range(n_chunks):
        for pp in range(RW_PAIRS):
            u = pp * n_chunks + c
            rows, lanes = window(pp, c)
            zt = zts[pp]
            y_ref[rows, lanes] = _mm_nt(ra_scr[u], zt) + yv_scr[u]
            zts[pp] = zt * ge_scr[u][0:1, :] + _mm(zt, m_scr[u]) + zc_scr[u]
    for pp in range(RW_PAIRS):
        z_ref[pp] = zts[pp]


def rwkv_scan(r, lw, k, v, a, b, *, batch, tb):
    m, hw = r.shape
    t_len = m // batch
    nt = t_len // tb
    nc = tb // RW_CHUNK
    lanes = RW_PAIRS * LANE
    nu = RW_PAIRS * nc
    assert t_len % tb == 0 and tb % RW_CHUNK == 0 and hw % lanes == 0
    spec = pl.BlockSpec((tb, lanes), lambda bi, p, t: (bi * nt + t, p))
    return pl.pallas_call(
        functools.partial(_rwkv_scan_kernel, n_chunks=nc),
        grid=(batch, hw // lanes, nt),
        in_specs=[spec] * 6,
        out_specs=spec,
        out_shape=jax.ShapeDtypeStruct((m, hw), F32),
        scratch_shapes=[
            pltpu.VMEM((RW_PAIRS, LANE, LANE), F32),
            pltpu.VMEM((nu, LANE, LANE), F32), pltpu.VMEM((nu, LANE, LANE), F32),
            pltpu.VMEM((nu, RW_CHUNK, LANE), F32), pltpu.VMEM((nu, RW_CHUNK, LANE), F32),
            pltpu.VMEM((nu, SUBLANE, LANE), F32),
        ],
        compiler_params=_cparams(("arbitrary", "arbitrary", "arbitrary")),
        name="rwkv_scan",
    )(r, lw, k, v, a, b)


def _rwkv_post_kernel(y_ref, bon_ref, g_ref, lw_ref, lb_ref, o_ref):
    ones = _block_ones(LANE, RWKV_HEAD)
    y = y_ref[...]
    inv_n = 1.0 / RWKV_HEAD
    mu = _seg_sum(y, ones) * inv_n
    yc = y - mu
    var = _seg_sum(yc * yc, ones) * inv_n
    yn = yc * lax.rsqrt(var + GN_EPS) * lw_ref[...] + lb_ref[...]
    o_ref[...] = ((yn + bon_ref[...].astype(F32)) * g_ref[...].astype(F32)).astype(o_ref.dtype)


def rwkv_post(y, bonus, g, lnx_w, lnx_b, *, tm):
    m, hw = y.shape
    spec = pl.BlockSpec((tm, hw), lambda i: (i, 0))
    row = pl.BlockSpec((1, hw), lambda i: (0, 0))
    return pl.pallas_call(
        _rwkv_post_kernel,
        grid=(m // tm,),
        in_specs=[spec, spec, spec, row, row],
        out_specs=spec,
        out_shape=jax.ShapeDtypeStruct((m, hw), BF16),
        compiler_params=_cparams(("arbitrary",)),
        name="rwkv_post",
    )(y, bonus, g, lnx_w.reshape(1, hw), lnx_b.reshape(1, hw))


def _lru_kernel(gate_ref, xb_ref, cw_ref, cb_ref, wr_ref, br_ref, wi_ref, bi_ref, lam_ref,
                o_ref, halo_ref, h_ref, a_scr, b_scr, *, tc):
    t_idx = pl.program_id(1)
    width = xb_ref.shape[1]
    nblk = tc // SUBLANE
    n_gate_blocks = width // LRU_BLOCK

    @pl.when(t_idx == 0)
    def _():
        halo_ref[...] = jnp.zeros_like(halo_ref)
        h_ref[...] = jnp.zeros_like(h_ref)

    xb = xb_ref[...].astype(F32)
    ext = jnp.concatenate([halo_ref[...], xb], axis=0)
    xc = xb * cw_ref[CONV_WIDTH - 1:CONV_WIDTH, :] + cb_ref[...]
    for d in range(1, CONV_WIDTH):
        sh = pltpu.roll(ext, d, axis=0)[SUBLANE:, :]
        xc = xc + sh * cw_ref[CONV_WIDTH - 1 - d:CONV_WIDTH - d, :]
    halo_ref[...] = xb[tc - SUBLANE:, :]

    sp = _softplus(-lam_ref[...])

    for n in range(n_gate_blocks):
        lanes = slice(n * LRU_BLOCK, (n + 1) * LRU_BLOCK)
        xn = xc[:, lanes]
        xnb = xn.astype(BF16)
        gr = jnp.dot(xnb, wr_ref[n].astype(BF16), preferred_element_type=F32) + br_ref[:, lanes]
        gi = jnp.dot(xnb, wi_ref[n].astype(BF16), preferred_element_type=F32) + bi_ref[:, lanes]
        log_a = -LRU_C * _sigmoid(gr) * sp[:, lanes]
        a = jnp.exp(log_a)
        mult = jnp.sqrt(-jnp.tanh(log_a) * (a * a + 1.0))
        a_scr[:, lanes] = a
        b_scr[:, lanes] = mult * _sigmoid(gi) * xn

    grp = LRU_SCAN_LANES
    row = lax.broadcasted_iota(jnp.int32, (SUBLANE, grp), 0)

    def blk(bidx, carry):
        rows = pl.ds(pl.multiple_of(bidx * SUBLANE, SUBLANE), SUBLANE)
        for g in range(width // grp):
            lanes = slice(g * grp, (g + 1) * grp)
            av = a_scr[rows, lanes]
            bv = b_scr[rows, lanes]
            for d in (1, 2, 4):
                keep = row >= d
                ash = jnp.where(keep, pltpu.roll(av, d, axis=0), 1.0)
                bsh = jnp.where(keep, pltpu.roll(bv, d, axis=0), 0.0)
                bv = bv + av * bsh
                av = av * ash
            hv = bv + av * h_ref[:, lanes]
            b_scr[rows, lanes] = hv
            h_ref[:, lanes] = jnp.broadcast_to(hv[SUBLANE - 1:SUBLANE, :], (SUBLANE, grp))
        return carry

    lax.fori_loop(0, nblk, blk, 0)
    o_ref[...] = (b_scr[...] * gate_ref[...].astype(F32)).astype(o_ref.dtype)


def lru_block(p_odd, conv_w, conv_b, w_r, b_r, w_i, b_i, lam, layer, *, batch, tc):
    m = p_odd.shape[0]
    width = lam.shape[0]
    t_len = m // batch
    nt = t_len // tc
    assert t_len % tc == 0 and p_odd.shape[1] == 2 * width
    row = pl.BlockSpec((1, width), lambda b, t: (0, 0))
    blkw = pl.BlockSpec((None,) + w_r.shape[1:], lambda b, t: (layer, 0, 0, 0))
    return pl.pallas_call(
        functools.partial(_lru_kernel, tc=tc),
        grid=(batch, nt),
        in_specs=[
            pl.BlockSpec((tc, width), lambda b, t: (b * nt + t, 0)),
            pl.BlockSpec((tc, width), lambda b, t: (b * nt + t, 1)),
            pl.BlockSpec((CONV_WIDTH, width), lambda b, t: (0, 0)),
            row, blkw, row, blkw, row, row,
        ],
        out_specs=pl.BlockSpec((tc, width), lambda b, t: (b * nt + t, 0)),
        out_shape=jax.ShapeDtypeStruct((m, width), BF16),
        scratch_shapes=[
            pltpu.VMEM((SUBLANE, width), F32), pltpu.VMEM((SUBLANE, width), F32),
            pltpu.VMEM((tc, width), F32), pltpu.VMEM((tc, width), F32),
        ],
        compiler_params=_cparams(("arbitrary", "arbitrary")),
        name="lru_block",
    )(p_odd, p_odd, conv_w, conv_b.reshape(1, width), w_r, b_r.reshape(1, width),
      w_i, b_i.reshape(1, width), lam.reshape(1, width))


def _s5_block_weights(bb_re, bb_im, c_re, c_im):
    c, g, p = bb_re.shape
    gpt = LANE // c
    tiles = g // gpt
    eye = jnp.eye(gpt, dtype=F32)

    def b_blk(bb):
        bb = bb.reshape(c, tiles, gpt, p)
        return jnp.einsum('cjgp,gh->jgchp', bb, eye).reshape(tiles, gpt * c, gpt * p)

    def c_blk(cm):
        cm = cm.reshape(tiles, gpt, c, p)
        return jnp.einsum('jgcp,gh->jhpgc', cm, eye).reshape(tiles, gpt * p, gpt * c)

    return b_blk(bb_re), b_blk(bb_im), c_blk(c_re), c_blk(c_im)


def _even_mixer(x, h, g_post, g_ffn, batch, idx, w_in, shift_mu, lam_re, lam_im, log_dt,
                b_re, b_im, c_re, c_im, d_skip, w_glu, w0, w2, a0, a2, g2, k_k, k_a, r_k,
                lnx_w, lnx_b, w_out):
    hw = w0.shape[0]
    s5w = d_skip.shape[0]
    n_main = s5w + 3 * hw
    p_main = in_proj(h, w_in, idx, n_main, tm=2048, tn=512, out_dtype=BF16, name="even_in_proj")
    lrw = w_in.shape[2] - n_main
    lr_pad = -(-lrw // LANE) * LANE
    w_lr = jnp.pad(w_in[idx, :, n_main:], ((0, 0), (0, lr_pad - lrw)))[None]
    lr = in_proj(h, w_lr, 0, lr_pad, tm=2048, tn=lr_pad, out_dtype=F32, name="even_lr_proj")

    a_re, a_im, bb_re, bb_im = s5_discretise(lam_re, lam_im, log_dt, b_re, b_im)
    bblk_re, bblk_im, cblk_re, cblk_im = _s5_block_weights(bb_re, bb_im, c_re, c_im)
    tiles = bblk_re.shape[0]
    wk, wb, wc, a8r, a8i = s5_tables(bblk_re, bblk_im, cblk_re, cblk_im,
                                     a_re.reshape(tiles, 1, -1), a_im.reshape(tiles, 1, -1),
                                     d_skip.reshape(tiles, 1, LANE))
    y_s5 = s5_mix(p_main, wk, wb, wc, a8r, a8i, batch=batch, tc=2048)
    y_s5 = s5_glu(y_s5, w_glu, idx, tm=1024, tn=512)

    r, lw, k, v, a, b, bonus, g = rwkv_prep(p_main, lr, shift_mu, w0, w2, a0, a2, g2,
                                            k_k, k_a, r_k.reshape(-1), batch=batch, tm=256)
    y_rw = rwkv_scan(r, lw, k, v, a, b, batch=batch, tb=512)
    y_rw = rwkv_post(y_rw, bonus, g, lnx_w, lnx_b, tm=512)

    return out_proj_norm_residual([y_s5, y_rw], w_out, idx, x, g_post, g_ffn, tm=512,
                                  name="even_out_proj")


def _odd_mixer(x, h, g_post, g_ffn, batch, idx, w_in, conv_w, conv_b, w_r, b_r, w_i, b_i,
               lam, w_out):
    width = lam.shape[0]
    p_odd = in_proj(h, w_in, idx, 2 * width, tm=2048, tn=512, out_dtype=BF16,
                    n_gelu_tiles=width // 512, name="odd_in_proj")
    hg = lru_block(p_odd, conv_w, conv_b, w_r, b_r, w_i, b_i, lam, idx, batch=batch, tc=256)
    return out_proj_norm_residual([hg], w_out, idx, x, g_post, g_ffn, tm=512,
                                  name="odd_out_proj")


def _ffn(x, h, g_post, g_next, layer, w_gate, w_up, w_down):
    act = swiglu_up(h, w_gate, w_up, layer, tm=2048, tn=512)
    return matmul_norm_residual(act, w_down, layer, x, g_post, g_next, tm=1024, tk=512,
                                n_chunk=512, name="ffn_down")


def kernel(x, ev_w_in, ev_shift_mu, s5_lam_re, s5_lam_im, s5_log_dt, s5_b_re, s5_b_im, s5_c_re, s5_c_im, s5_d, s5_w_glu, rw_w0, rw_w2, rw_a0, rw_a2, rw_g2, rw_k_k, rw_k_a, rw_r_k, rw_lnx_w, rw_lnx_b, ev_w_out, od_w_in, od_conv_w, od_conv_b, lru_w_r, lru_b_r, lru_w_i, lru_b_i, lru_lam, od_w_out, ffn_w_gate, ffn_w_up, ffn_w_down, norm_mix_pre, norm_mix_post, norm_ffn_pre, norm_ffn_post):
    batch, t_len, d = x.shape
    depth = ffn_w_gate.shape[0]
    xf = x.reshape(batch * t_len, d)
    s5_w_glu, ev_w_out, od_w_out = (w.astype(BF16) for w in (s5_w_glu, ev_w_out, od_w_out))
    hm = rmsnorm(xf, norm_mix_pre[0], tm=512)
    for layer in range(depth):
        i = layer // 2
        if layer % 2 == 0:
            xf, hf = _even_mixer(xf, hm, norm_mix_post[layer],
                                 norm_ffn_pre[layer], batch, i,
                                 ev_w_in, ev_shift_mu[i], s5_lam_re[i], s5_lam_im[i],
                                 s5_log_dt[i], s5_b_re[i], s5_b_im[i], s5_c_re[i], s5_c_im[i],
                                 s5_d[i], s5_w_glu, rw_w0[i], rw_w2[i], rw_a0[i], rw_a2[i],
                                 rw_g2[i], rw_k_k[i], rw_k_a[i], rw_r_k[i], rw_lnx_w[i],
                                 rw_lnx_b[i], ev_w_out)
        else:
            xf, hf = _odd_mixer(xf, hm, norm_mix_post[layer],
                                norm_ffn_pre[layer], batch, i,
                                od_w_in, od_conv_w[i], od_conv_b[i], lru_w_r, lru_b_r[i],
                                lru_w_i, lru_b_i[i], lru_lam[i], od_w_out)
        if layer + 1 < depth:
            xf, hm = _ffn(xf, hf, norm_ffn_post[layer], norm_mix_pre[layer + 1], layer,
                          ffn_w_gate, ffn_w_up, ffn_w_down)
        else:
            xf = _ffn(xf, hf, norm_ffn_post[layer], None, layer,
                      ffn_w_gate, ffn_w_up, ffn_w_down)
    return xf.reshape(batch, t_len, d)
```

```python
import functools
import math

import jax
import jax.numpy as jnp
from jax import lax
from jax.experimental import pallas as pl
from jax.experimental.pallas import tpu as pltpu

F32 = jnp.float32
BF16 = jnp.bfloat16

NORM_EPS = 1e-6
GN_EPS = 64e-5
LRU_C = 8.0
RWKV_HEAD = 64
CONV_WIDTH = 4
LRU_BLOCK = 256

LANE = 128
SUBLANE = 8
VMEM_LIMIT = 56 * 1024 * 1024

RW_CHUNK = 64
RW_PAIRS = 2
S5_BLOCK = 8
LRU_SCAN_LANES = 512


def _cparams(sem):
    return pltpu.CompilerParams(dimension_semantics=sem, vmem_limit_bytes=VMEM_LIMIT)


def _gelu(x):
    c = math.sqrt(2.0 / math.pi)
    return 0.5 * x * (1.0 + jnp.tanh(c * (x + 0.044715 * (x * x * x))))


def _sigmoid(x):
    return 1.0 / (1.0 + jnp.exp(-x))


def _softplus(x):
    return jnp.maximum(x, 0.0) + jnp.log(1.0 + jnp.exp(-jnp.abs(x)))


def _rms_rows(y, g):
    ms = jnp.mean(y * y, axis=-1, keepdims=True)
    return y * lax.rsqrt(ms + NORM_EPS) * g


def _rmsnorm_kernel(x_ref, g_ref, o_ref):
    o_ref[...] = _rms_rows(x_ref[...], g_ref[...]).astype(o_ref.dtype)


def rmsnorm(x, g, *, tm):
    m, d = x.shape
    assert m % tm == 0
    return pl.pallas_call(
        _rmsnorm_kernel,
        grid=(m // tm,),
        in_specs=[pl.BlockSpec((tm, d), lambda i: (i, 0)), pl.BlockSpec((1, d), lambda i: (0, 0))],
        out_specs=pl.BlockSpec((tm, d), lambda i: (i, 0)),
        out_shape=jax.ShapeDtypeStruct((m, d), BF16),
        compiler_params=_cparams(("arbitrary",)),
        name="rmsnorm",
    )(x, g.reshape(1, d))


def _in_proj_kernel(h_ref, w_ref, o_ref, *, n_gelu_tiles):
    j = pl.program_id(1)
    acc = jnp.dot(h_ref[...], w_ref[...].astype(BF16), preferred_element_type=F32)
    if n_gelu_tiles == 0:
        o_ref[...] = acc.astype(o_ref.dtype)
    else:
        @pl.when(j < n_gelu_tiles)
        def _():
            o_ref[...] = _gelu(acc).astype(o_ref.dtype)

        @pl.when(j >= n_gelu_tiles)
        def _():
            o_ref[...] = acc.astype(o_ref.dtype)


def in_proj(h, w, layer, n_out, *, tm, tn, out_dtype, n_gelu_tiles=0, name):
    m, d = h.shape
    assert m % tm == 0 and n_out % tn == 0 and w.shape[1] == d and h.dtype == BF16
    return pl.pallas_call(
        functools.partial(_in_proj_kernel, n_gelu_tiles=n_gelu_tiles),
        grid=(m // tm, n_out // tn),
        in_specs=[
            pl.BlockSpec((tm, d), lambda i, j: (i, 0)),
            pl.BlockSpec((None, d, tn), lambda i, j: (layer, 0, j)),
        ],
        out_specs=pl.BlockSpec((tm, tn), lambda i, j: (i, j)),
        out_shape=jax.ShapeDtypeStruct((m, n_out), out_dtype),
        compiler_params=_cparams(("arbitrary", "arbitrary")),
        name=name,
    )(h, w)


def _swiglu_up_kernel(h_ref, wg_ref, wu_ref, o_ref):
    h = h_ref[...]
    gate = jnp.dot(h, wg_ref[...].astype(BF16), preferred_element_type=F32)
    up = jnp.dot(h, wu_ref[...].astype(BF16), preferred_element_type=F32)
    o_ref[...] = (gate * _sigmoid(gate) * up).astype(o_ref.dtype)


def swiglu_up(h, w_gate, w_up, layer, *, tm, tn):
    m, d = h.shape
    n = w_gate.shape[2]
    assert m % tm == 0 and n % tn == 0 and h.dtype == BF16
    return pl.pallas_call(
        _swiglu_up_kernel,
        grid=(m // tm, n // tn),
        in_specs=[
            pl.BlockSpec((tm, d), lambda i, j: (i, 0)),
            pl.BlockSpec((None, d, tn), lambda i, j: (layer, 0, j)),
            pl.BlockSpec((None, d, tn), lambda i, j: (layer, 0, j)),
        ],
        out_specs=pl.BlockSpec((tm, tn), lambda i, j: (i, j)),
        out_shape=jax.ShapeDtypeStruct((m, n), BF16),
        compiler_params=_cparams(("arbitrary", "arbitrary")),
        name="ffn_up",
    )(h, w_gate, w_up)


def _mm_norm_res_kernel(a_ref, w_ref, x_ref, g_ref, *rest, emit_next, n_chunk):
    if emit_next:
        gn_ref, o_ref, hn_ref = rest
    else:
        (o_ref,) = rest
    k = pl.program_id(1)
    d = o_ref.shape[1]

    def accumulate(first):
        a = a_ref[...]
        for n in range(0, d, n_chunk):
            cols = slice(n, n + n_chunk)
            part = jnp.dot(a, w_ref[:, cols].astype(BF16), preferred_element_type=F32)
            if first:
                o_ref[:, cols] = part
            else:
                o_ref[:, cols] += part

    @pl.when(k == 0)
    def _():
        accumulate(True)

    @pl.when(k > 0)
    def _():
        accumulate(False)

    @pl.when(k == pl.num_programs(1) - 1)
    def _():
        x_new = x_ref[...] + _rms_rows(o_ref[...], g_ref[...])
        o_ref[...] = x_new
        if emit_next:
            hn_ref[...] = _rms_rows(x_new, gn_ref[...]).astype(hn_ref.dtype)


def matmul_norm_residual(a, w, layer, x, g, g_next, *, tm, tk, n_chunk, name):
    m, kdim = a.shape
    d = w.shape[2]
    assert m % tm == 0 and kdim % tk == 0 and w.shape[1] == kdim and d % n_chunk == 0
    emit_next = g_next is not None
    row = pl.BlockSpec((1, d), lambda i, k: (0, 0))
    blk = pl.BlockSpec((tm, d), lambda i, k: (i, 0))
    f32_out = jax.ShapeDtypeStruct((m, d), F32)
    operands = [a, w, x, g.reshape(1, d)] + ([g_next.reshape(1, d)] if emit_next else [])
    return pl.pallas_call(
        functools.partial(_mm_norm_res_kernel, emit_next=emit_next, n_chunk=n_chunk),
        grid=(m // tm, kdim // tk),
        in_specs=[
            pl.BlockSpec((tm, tk), lambda i, k: (i, k)),
            pl.BlockSpec((None, tk, d), lambda i, k: (layer, k, 0)),
            blk, row,
        ] + ([row] if emit_next else []),
        out_specs=[blk, blk] if emit_next else blk,
        out_shape=(f32_out, jax.ShapeDtypeStruct((m, d), BF16)) if emit_next else f32_out,
        compiler_params=_cparams(("arbitrary", "arbitrary")),
        name=name,
    )(*operands)


def _out_proj_kernel(*refs, n_lhs):
    a_refs = refs[:n_lhs]
    w_ref, x_ref, g_ref, gn_ref, o_ref, hn_ref = refs[n_lhs:]
    y = None
    k0 = 0
    for a_ref in a_refs:
        kw = a_ref.shape[1]
        part = jnp.dot(a_ref[...], w_ref[k0:k0 + kw, :], preferred_element_type=F32)
        y = part if y is None else y + part
        k0 += kw
    x_new = x_ref[...] + _rms_rows(y, g_ref[...])
    o_ref[...] = x_new
    hn_ref[...] = _rms_rows(x_new, gn_ref[...]).astype(hn_ref.dtype)


def out_proj_norm_residual(lhs, w, layer, x, g, g_next, *, tm, name):
    m = x.shape[0]
    kdim, d = w.shape[1], w.shape[2]
    assert sum(a.shape[1] for a in lhs) == kdim and m % tm == 0
    row = pl.BlockSpec((1, d), lambda i: (0, 0))
    return pl.pallas_call(
        functools.partial(_out_proj_kernel, n_lhs=len(lhs)),
        grid=(m // tm,),
        in_specs=[pl.BlockSpec((tm, a.shape[1]), lambda i: (i, 0)) for a in lhs] + [
            pl.BlockSpec((None, kdim, d), lambda i: (layer, 0, 0)),
            pl.BlockSpec((tm, d), lambda i: (i, 0)),
            row, row,
        ],
        out_specs=[pl.BlockSpec((tm, d), lambda i: (i, 0))] * 2,
        out_shape=(jax.ShapeDtypeStruct((m, d), F32), jax.ShapeDtypeStruct((m, d), BF16)),
        compiler_params=_cparams(("arbitrary",)),
        name=name,
    )(*lhs, w, x, g.reshape(1, d), g_next.reshape(1, d))


def _block_ones(n, seg):
    r = lax.broadcasted_iota(jnp.int32, (n, n), 0) // seg
    c = lax.broadcasted_iota(jnp.int32, (n, n), 1) // seg
    return (r == c).astype(BF16)


def _seg_sum(x, ones):
    outs = []
    for j in range(x.shape[1] // LANE):
        xj = x[:, j * LANE:(j + 1) * LANE]
        hi = xj.astype(BF16)
        lo = (xj - hi.astype(F32)).astype(BF16)
        s = (jnp.dot(hi, ones, preferred_element_type=F32)
             + jnp.dot(lo, ones, preferred_element_type=F32))
        outs.append(s)
    return jnp.concatenate(outs, axis=1) if len(outs) > 1 else outs[0]


def _s5_disc_kernel(lr_ref, li_ref, ldt_ref, bre_ref, bim_ref,
                    are_ref, aim_ref, bbre_ref, bbim_ref):
    lr = lr_ref[...]
    li = li_ref[...]
    dt = jnp.exp(ldt_ref[...])
    mag = jnp.exp(lr * dt)
    a_re = mag * jnp.cos(li * dt)
    a_im = mag * jnp.sin(li * dt)
    den = lr * lr + li * li
    nr = a_re - 1.0
    ni = a_im
    gam_re = (nr * lr + ni * li) / den
    gam_im = (ni * lr - nr * li) / den
    are_ref[...] = a_re
    aim_ref[...] = a_im
    for c in range(bre_ref.shape[0]):
        br = bre_ref[c]
        bi = bim_ref[c]
        bbre_ref[c] = gam_re * br - gam_im * bi
        bbim_ref[c] = gam_re * bi + gam_im * br


def s5_discretise(lam_re, lam_im, log_dt, b_re, b_im):
    g, p, c = b_re.shape
    ldt = jnp.broadcast_to(log_dt[:, None], (g, p))
    b_re_t = jnp.transpose(b_re, (2, 0, 1))
    b_im_t = jnp.transpose(b_im, (2, 0, 1))
    gp = jax.ShapeDtypeStruct((g, p), F32)
    cgp = jax.ShapeDtypeStruct((c, g, p), F32)
    return pl.pallas_call(
        _s5_disc_kernel,
        out_shape=(gp, gp, cgp, cgp),
        name="s5_discretise",
    )(lam_re, lam_im, ldt, b_re_t, b_im_t)


def _cmul(ar, ai, br, bi):
    return ar * br - ai * bi, ar * bi + ai * br


def _s5_tables_kernel(bre_ref, bim_ref, cre_ref, cim_ref, ar_ref, ai_ref, acr_ref, aci_ref, d_ref,
                      wk_ref, wb_ref, wc_ref, a8r_ref, a8i_ref):
    bre, bim = bre_ref[...], bim_ref[...]
    cre, cim = cre_ref[...], cim_ref[...]
    ar, ai = ar_ref[...], ai_ref[...]
    acr, aci = acr_ref[...], aci_ref[...]
    ml = bre.shape[1]
    eye = (lax.broadcasted_iota(jnp.int32, (LANE, LANE), 0)
           == lax.broadcasted_iota(jnp.int32, (LANE, LANE), 1)).astype(F32)

    pr, pi = [jnp.ones_like(ar)], [jnp.zeros_like(ai)]
    for _ in range(S5_BLOCK):
        nr, ni = _cmul(pr[-1], pi[-1], ar, ai)
        pr.append(nr)
        pi.append(ni)
    qr, qi = [acr], [aci]
    for _ in range(S5_BLOCK - 1):
        nr, ni = _cmul(qr[-1], qi[-1], acr, aci)
        qr.append(nr)
        qi.append(ni)

    hi = lax.Precision.HIGHEST
    for tau in range(S5_BLOCK):
        mre = bre * pr[tau] - bim * pi[tau]
        mim = bre * pi[tau] + bim * pr[tau]
        k = (jnp.dot(mre, cre, preferred_element_type=F32, precision=hi)
             - jnp.dot(mim, cim, preferred_element_type=F32, precision=hi))
        if tau == 0:
            k = k + eye * d_ref[...]
        wk_ref[tau * LANE:(tau + 1) * LANE, :] = k.astype(wk_ref.dtype)
        r = S5_BLOCK - 1 - tau
        wb_ref[r * LANE:(r + 1) * LANE, :ml] = mre.astype(wb_ref.dtype)
        wb_ref[r * LANE:(r + 1) * LANE, ml:] = mim.astype(wb_ref.dtype)

    for r in range(S5_BLOCK):
        dr, di = qr[r], qi[r]
        wc_ref[:ml, r * LANE:(r + 1) * LANE] = (cre * dr - cim * di).astype(wc_ref.dtype)
        wc_ref[ml:, r * LANE:(r + 1) * LANE] = (-(cre * di + cim * dr)).astype(wc_ref.dtype)

    a8r_ref[...] = pr[S5_BLOCK]
    a8i_ref[...] = pi[S5_BLOCK]


def s5_tables(bblk_re, bblk_im, cblk_re, cblk_im, a_re, a_im, d_skip):
    tiles, _, ml = bblk_re.shape
    a_col_re = jnp.broadcast_to(jnp.swapaxes(a_re, 1, 2), (tiles, ml, LANE))
    a_col_im = jnp.broadcast_to(jnp.swapaxes(a_im, 1, 2), (tiles, ml, LANE))
    kdim = S5_BLOCK * LANE
    t3 = lambda a, b: pl.BlockSpec((None, a, b), lambda j: (j, 0, 0))
    return pl.pallas_call(
        _s5_tables_kernel,
        grid=(tiles,),
        in_specs=[t3(LANE, ml), t3(LANE, ml), t3(ml, LANE), t3(ml, LANE), t3(1, ml), t3(1, ml),
                  t3(ml, LANE), t3(ml, LANE), t3(1, LANE)],
        out_specs=[t3(kdim, LANE), t3(kdim, 2 * ml), t3(2 * ml, kdim), t3(1, ml), t3(1, ml)],
        out_shape=(jax.ShapeDtypeStruct((tiles, kdim, LANE), BF16),
                   jax.ShapeDtypeStruct((tiles, kdim, 2 * ml), BF16),
                   jax.ShapeDtypeStruct((tiles, 2 * ml, kdim), BF16),
                   jax.ShapeDtypeStruct((tiles, 1, ml), F32),
                   jax.ShapeDtypeStruct((tiles, 1, ml), F32)),
        compiler_params=_cparams(("arbitrary",)),
        name="s5_tables",
    )(bblk_re, bblk_im, cblk_re, cblk_im, a_re, a_im, a_col_re, a_col_im, d_skip)


def _s5_mix_kernel(u_ref, wk_ref, wb_ref, wc_ref, a8r_ref, a8i_ref, o_ref,
                   car_ref, cai_ref, tab_ref, uf_ref, xr_ref, xi_ref, y_ref, *, tc):
    t_idx = pl.program_id(2)
    ml = a8r_ref.shape[1]
    nb = tc // S5_BLOCK
    row = lax.broadcasted_iota(jnp.int32, (SUBLANE, ml), 0)

    @pl.when(t_idx == 0)
    def _():
        car_ref[...] = jnp.zeros_like(car_ref)
        cai_ref[...] = jnp.zeros_like(cai_ref)
        a1r = jnp.broadcast_to(a8r_ref[...], (SUBLANE, ml))
        a1i = jnp.broadcast_to(a8i_ref[...], (SUBLANE, ml))
        a2r, a2i = _cmul(a1r, a1i, a1r, a1i)
        a4r, a4i = _cmul(a2r, a2i, a2r, a2i)
        pr, pi = a1r, a1i
        cr, ci = a1r, a1i
        for r in range(1, SUBLANE):
            cr, ci = _cmul(cr, ci, a1r, a1i)
            pr = jnp.where(row >= r, cr, pr)
            pi = jnp.where(row >= r, ci, pi)
        for lvl, (d, xr, xi) in enumerate(((1, a1r, a1i), (2, a2r, a2i), (4, a4r, a4i))):
            tab_ref[2 * lvl] = jnp.where(row >= d, xr, 0.0)
            tab_ref[2 * lvl + 1] = jnp.where(row >= d, xi, 0.0)
        tab_ref[6] = pr
        tab_ref[7] = pi

    uf = u_ref[...].astype(F32)
    uf_ref[...] = uf

    pos = lax.broadcasted_iota(jnp.int32, (tc, LANE), 0) % S5_BLOCK
    shifted = [uf.astype(BF16)]
    for tau in range(1, S5_BLOCK):
        shifted.append(jnp.where(pos >= tau, pltpu.roll(uf, tau, axis=0), 0.0).astype(BF16))
    y_ref[...] = jnp.dot(jnp.concatenate(shifted, axis=1), wk_ref[...], preferred_element_type=F32)

    strided = [uf_ref[pl.ds(r, nb, stride=S5_BLOCK), :].astype(BF16) for r in range(S5_BLOCK)]
    x = jnp.dot(jnp.concatenate(strided, axis=1), wb_ref[...], preferred_element_type=F32)
    xr_ref[...] = x[:, :ml]
    xi_ref[...] = x[:, ml:]

    def blk(b, carry):
        rows = pl.ds(pl.multiple_of(b * SUBLANE, SUBLANE), SUBLANE)
        sr = xr_ref[rows, :]
        si = xi_ref[rows, :]
        for lvl, d in enumerate((1, 2, 4)):
            kr = tab_ref[2 * lvl]
            ki = tab_ref[2 * lvl + 1]
            rr = pltpu.roll(sr, d, axis=0)
            ri = pltpu.roll(si, d, axis=0)
            sr, si = sr + kr * rr - ki * ri, si + kr * ri + ki * rr
        pr = tab_ref[6]
        pi = tab_ref[7]
        cbr = car_ref[...]
        cbi = cai_ref[...]
        sr = sr + pr * cbr - pi * cbi
        si = si + pr * cbi + pi * cbr
        xr_ref[rows, :] = jnp.where(row == 0, cbr, pltpu.roll(sr, 1, axis=0))
        xi_ref[rows, :] = jnp.where(row == 0, cbi, pltpu.roll(si, 1, axis=0))
        car_ref[...] = jnp.broadcast_to(sr[SUBLANE - 1:SUBLANE, :], (SUBLANE, ml))
        cai_ref[...] = jnp.broadcast_to(si[SUBLANE - 1:SUBLANE, :], (SUBLANE, ml))
        return carry

    lax.fori_loop(0, nb // SUBLANE, blk, 0)

    s_in = jnp.concatenate([xr_ref[...], xi_ref[...]], axis=1).astype(BF16)
    z = jnp.dot(s_in, wc_ref[...], preferred_element_type=F32)
    for r in range(S5_BLOCK):
        rows = pl.ds(r, nb, stride=S5_BLOCK)
        y_ref[rows, :] = y_ref[rows, :] + z[:, r * LANE:(r + 1) * LANE]
    o_ref[...] = _gelu(y_ref[...])


def s5_mix(p_main, wk, wb, wc, a8r, a8i, *, batch, tc):
    m = p_main.shape[0]
    t_len = m // batch
    tiles, kdim, _ = wk.shape
    ml = a8r.shape[2]
    nt = t_len // tc
    nb = tc // S5_BLOCK
    assert t_len % tc == 0 and nb % SUBLANE == 0 and p_main.dtype == BF16
    t3 = lambda a, b: pl.BlockSpec((None, a, b), lambda bi, j, t: (j, 0, 0))
    io = pl.BlockSpec((tc, LANE), lambda bi, j, t: (bi * nt + t, j))
    return pl.pallas_call(
        functools.partial(_s5_mix_kernel, tc=tc),
        grid=(batch, tiles, nt),
        in_specs=[io, t3(kdim, LANE), t3(kdim, 2 * ml), t3(2 * ml, kdim), t3(1, ml), t3(1, ml)],
        out_specs=io,
        out_shape=jax.ShapeDtypeStruct((m, tiles * LANE), F32),
        scratch_shapes=[
            pltpu.VMEM((SUBLANE, ml), F32), pltpu.VMEM((SUBLANE, ml), F32),
            pltpu.VMEM((8, SUBLANE, ml), F32),
            pltpu.VMEM((tc, LANE), F32),
            pltpu.VMEM((nb, ml), F32), pltpu.VMEM((nb, ml), F32),
            pltpu.VMEM((tc, LANE), F32),
        ],
        compiler_params=_cparams(("arbitrary", "arbitrary", "arbitrary")),
        name="s5_mix",
    )(p_main, wk, wb, wc, a8r, a8i)


def _shift_mix(z, prev_row, mu):
    row = lax.broadcasted_iota(jnp.int32, z.shape, 0)
    zs = jnp.where(row == 0, jnp.broadcast_to(prev_row, z.shape), pltpu.roll(z, 1, axis=0))
    return z + (zs - z) * mu


def _rwkv_prep_kernel(r_ref, k_ref, v_ref, lr_ref, rp_ref, kp_ref, vp_ref, lrp_ref,
                      mur_ref, muk_ref, muv_ref, mulr_ref,
                      w0_ref, w2_ref, a0_ref, a2_ref, g2_ref, kk_ref, ka_ref, rk_ref,
                      ro_ref, lw_ref, ko_ref, vo_ref, ao_ref, bo_ref, bon_ref, go_ref,
                      *, tiles_per_batch):
    i = pl.program_id(0)
    first = (i % tiles_per_batch) == 0

    def prev(ref):
        last = ref.shape[0] - 1
        return jnp.where(first, 0.0, ref[last:last + 1, :].astype(F32))

    r = _shift_mix(r_ref[...].astype(F32), prev(rp_ref), mur_ref[...])
    k = _shift_mix(k_ref[...].astype(F32), prev(kp_ref), muk_ref[...])
    v = _shift_mix(v_ref[...].astype(F32), prev(vp_ref), muv_ref[...])
    lr = _shift_mix(lr_ref[...], prev(lrp_ref), mulr_ref[...])

    wl = w0_ref[...] + jnp.dot(jnp.tanh(lr).astype(BF16), w2_ref[...].astype(BF16),
                               preferred_element_type=F32)
    w = -_softplus(-wl) - 0.5
    lw_ref[...] = -jnp.exp(w)
    a = _sigmoid(a0_ref[...] + jnp.dot(lr.astype(BF16), a2_ref[...].astype(BF16),
                                       preferred_element_type=F32))
    go_ref[...] = jnp.dot(_sigmoid(lr).astype(BF16), g2_ref[...].astype(BF16),
                          preferred_element_type=F32).astype(go_ref.dtype)

    ones = _block_ones(LANE, RWKV_HEAD)
    kk = k * kk_ref[...]
    kk = kk * lax.rsqrt(jnp.maximum(_seg_sum(kk * kk, ones), 1e-24))
    k = k * (1.0 + (a - 1.0) * ka_ref[...])
    ro_ref[...] = r.astype(ro_ref.dtype)
    ko_ref[...] = k.astype(ko_ref.dtype)
    vo_ref[...] = v.astype(vo_ref.dtype)
    ao_ref[...] = (-kk).astype(ao_ref.dtype)
    bo_ref[...] = (kk * a).astype(bo_ref.dtype)
    bon_ref[...] = (_seg_sum(r * k * rk_ref[...], ones) * v).astype(bon_ref.dtype)


def rwkv_prep(p_main, lr, mu, w0, w2, a0, a2, g2, k_k, k_a, r_k, *, batch, tm):
    m = p_main.shape[0]
    t_len = m // batch
    hw = w0.shape[0]
    lrw = lr.shape[1]
    w_rank, a_rank, g_rank = w2.shape[0], a2.shape[0], g2.shape[0]
    nblk8 = tm // SUBLANE
    halo = 2 * SUBLANE
    nblk16 = tm // halo
    assert t_len % tm == 0 and p_main.shape[1] == 4 * hw and p_main.dtype == BF16

    def cur(c):
        return pl.BlockSpec((tm, hw), lambda i: (i, c))

    def prv(c):
        return pl.BlockSpec((halo, hw), lambda i: (jnp.maximum(i * nblk16 - 1, 0), c))

    row = lambda n: pl.BlockSpec((1, n), lambda i: (0, 0))
    mat = lambda a: pl.BlockSpec(a.shape, lambda i: (0, 0))
    used = w_rank + a_rank + g_rank
    mu_lr = jnp.pad(mu[3 * hw:], (0, lrw - used))
    w2 = jnp.pad(w2, ((0, lrw - w_rank), (0, 0)))
    a2 = jnp.pad(a2, ((w_rank, lrw - w_rank - a_rank), (0, 0)))
    g2 = jnp.pad(g2, ((w_rank + a_rank, lrw - used), (0, 0)))
    out = jax.ShapeDtypeStruct((m, hw), BF16)
    out_lw = jax.ShapeDtypeStruct((m, hw), F32)
    return pl.pallas_call(
        functools.partial(_rwkv_prep_kernel, tiles_per_batch=t_len // tm),
        grid=(m // tm,),
        in_specs=[
            cur(1), cur(2), cur(3), pl.BlockSpec((tm, lrw), lambda i: (i, 0)),
            prv(1), prv(2), prv(3),
            pl.BlockSpec((SUBLANE, lrw), lambda i: (jnp.maximum(i * nblk8 - 1, 0), 0)),
            row(hw), row(hw), row(hw), row(lrw),
            row(hw), mat(w2), row(hw), mat(a2), mat(g2), row(hw), row(hw), row(hw),
        ],
        out_specs=[pl.BlockSpec((tm, hw), lambda i: (i, 0))] * 8,
        out_shape=(out, out_lw) + (out,) * 6,
        compiler_params=_cparams(("arbitrary",)),
        name="rwkv_prep",
    )(p_main, p_main, p_main, lr, p_main, p_main, p_main, lr,
      mu[:hw].reshape(1, hw), mu[hw:2 * hw].reshape(1, hw), mu[2 * hw:3 * hw].reshape(1, hw),
      mu_lr.reshape(1, lrw),
      w0.reshape(1, hw), w2, a0.reshape(1, hw), a2, g2, k_k.reshape(1, hw),
      k_a.reshape(1, hw), r_k.reshape(1, hw))


def _mm(a, b):
    return jnp.dot(a.astype(BF16), b.astype(BF16), preferred_element_type=F32)


def _mm_nt(a, b):
    return lax.dot_general(a.astype(BF16), b.astype(BF16), (((1,), (1,)), ((), ())),
                           preferred_element_type=F32)


def _mm_tn(a, b):
    return lax.dot_general(a.astype(BF16), b.astype(BF16), (((0,), (0,)), ((), ())),
                           preferred_element_type=F32)


def _split3(x):
    hi = x.astype(BF16)
    r1 = x - hi.astype(F32)
    mid = r1.astype(BF16)
    lo = (r1 - mid.astype(F32)).astype(BF16)
    return hi, mid, lo


def _rwkv_scan_kernel(r_ref, lw_ref, k_ref, v_ref, a_ref, b_ref, y_ref,
                      z_ref, m_scr, zc_scr, ra_scr, yv_scr, ge_scr, *, n_chunks):
    c_len = RW_CHUNK
    n2 = 2 * c_len

    @pl.when(pl.program_id(2) == 0)
    def _():
        z_ref[...] = jnp.zeros_like(z_ref)

    lane = lax.broadcasted_iota(jnp.int32, (c_len, LANE), 1)
    head_a = lane < RWKV_HEAD
    ri = lax.broadcasted_iota(jnp.int32, (n2, n2), 0)
    ci = lax.broadcasted_iota(jnp.int32, (n2, n2), 1)
    same = (ri // c_len) == (ci // c_len)
    strict = same & ((ci % c_len) < (ri % c_len))
    incl = same & ((ci % c_len) <= (ri % c_len))
    eye = (ri == ci).astype(F32)
    rc = lax.broadcasted_iota(jnp.int32, (c_len, c_len), 0)
    cc = lax.broadcasted_iota(jnp.int32, (c_len, c_len), 1)
    tri = (cc <= rc).astype(BF16)

    def stack(x):
        return jnp.concatenate([jnp.where(head_a, x, 0.0), jnp.where(head_a, 0.0, x)], axis=0)

    def fold(x):
        return x[:c_len, :] + x[c_len:, :]

    units = [(pp, c) for pp in range(RW_PAIRS) for c in range(n_chunks)]

    def window(pp, c):
        return slice(c * c_len, (c + 1) * c_len), slice(pp * LANE, (pp + 1) * LANE)

    pre = []
    for u, (pp, c) in enumerate(units):
        rows, lanes = window(pp, c)
        lw = lw_ref[rows, lanes]
        r, k, v, a, b = (ref[rows, lanes].astype(F32) for ref in (r_ref, k_ref, v_ref, a_ref, b_ref))
        cl = sum(jnp.dot(tri, part, preferred_element_type=F32) for part in _split3(lw))
        cl_end = cl[c_len - 1:c_len, :]
        g_inv = jnp.exp(-cl)
        g_tail = jnp.exp(cl_end - cl)
        r_t = r * jnp.exp(cl)
        ra_scr[u] = r_t
        ge_scr[u] = jnp.broadcast_to(jnp.exp(cl_end), (SUBLANE, LANE))
        pre.append(dict(
            a_s=stack(a * jnp.exp(cl - lw)).astype(BF16), r_s=stack(r_t).astype(BF16),
            b_s=stack(b * g_inv).astype(BF16), k_s=stack(k * g_inv).astype(BF16),
            v_s=stack(v).astype(BF16), bg_s=stack(b * g_tail).astype(BF16),
            kg_s=stack(k * g_tail).astype(BF16)))

    scs = [_mm_nt(jnp.concatenate([p["a_s"], p["r_s"]], axis=0),
                  jnp.concatenate([p["b_s"], p["k_s"]], axis=0)) for p in pre]
    l_ak = [jnp.where(strict, sc[:n2, n2:], 0.0).astype(BF16) for sc in scs]
    m_rb = [jnp.where(incl, sc[n2:, :n2], 0.0).astype(BF16) for sc in scs]
    m_rk = [jnp.where(incl, sc[n2:, n2:], 0.0).astype(BF16) for sc in scs]

    xs = [jnp.where(strict, sc[:n2, :n2], 0.0) for sc in scs]
    ts = [eye + x for x in xs]
    n = 2
    while n < c_len:
        xs = [_mm(x, x) for x in xs]
        ts = [t + _mm(t, x) for t, x in zip(ts, xs)]
        n *= 2

    lvs = [_mm(l, p["v_s"]) for l, p in zip(l_ak, pre)]
    tatv = [_mm(t, jnp.concatenate([p["a_s"], lv.astype(BF16)], axis=1)).astype(BF16)
            for t, p, lv in zip(ts, pre, lvs)]
    mtatv = [_mm(m, tv) for m, tv in zip(m_rb, tatv)]
    mkv = [_mm(m, p["v_s"]) for m, p in zip(m_rk, pre)]
    gz = [_mm_tn(tv, p["bg_s"]) for tv, p in zip(tatv, pre)]
    vk = [_mm_tn(p["v_s"], p["kg_s"]) for p in pre]
    for u in range(len(units)):
        ra_scr[u] = ra_scr[u] + fold(mtatv[u][:, :LANE])
        yv_scr[u] = fold(mtatv[u][:, LANE:] + mkv[u])
        m_scr[u] = gz[u][:LANE, :]
        zc_scr[u] = gz[u][LANE:, :] + vk[u]

    zts = [z_ref[pp] for pp in range(RW_PAIRS)]
    for c in range(n_chunks):
        for pp in range(RW_PAIRS):
            u = pp * n_chunks + c
            rows, lanes = window(pp, c)
            zt = zts[pp]
            y_ref[rows, lanes] = _mm_nt(ra_scr[u], zt) + yv_scr[u]
            zts[pp] = zt * ge_scr[u][0:1, :] + _mm(zt, m_scr[u]) + zc_scr[u]
    for pp in range(RW_PAIRS):
        z_ref[pp] = zts[pp]


def rwkv_scan(r, lw, k, v, a, b, *, batch, tb):
    m, hw = r.shape
    t_len = m // batch
    nt = t_len // tb
    nc = tb // RW_CHUNK
    lanes = RW_PAIRS * LANE
    nu = RW_PAIRS * nc
    assert t_len % tb == 0 and tb % RW_CHUNK == 0 and hw % lanes == 0
    spec = pl.BlockSpec((tb, lanes), lambda bi, p, t: (bi * nt + t, p))
    return pl.pallas_call(
        functools.partial(_rwkv_scan_kernel, n_chunks=nc),
        grid=(batch, hw // lanes, nt),
        in_specs=[spec] * 6,
        out_specs=spec,
        out_shape=jax.ShapeDtypeStruct((m, hw), F32),
        scratch_shapes=[
            pltpu.VMEM((RW_PAIRS, LANE, LANE), F32),
            pltpu.VMEM((nu, LANE, LANE), F32), pltpu.VMEM((nu, LANE, LANE), F32),
            pltpu.VMEM((nu, RW_CHUNK, LANE), F32), pltpu.VMEM((nu, RW_CHUNK, LANE), F32),
            pltpu.VMEM((nu, SUBLANE, LANE), F32),
        ],
        compiler_params=_cparams(("arbitrary", "arbitrary", "arbitrary")),
        name="rwkv_scan",
    )(r, lw, k, v, a, b)


def _even_out_kernel(ys_ref, wglu_ref, yr_ref, bon_ref, gt_ref, lw_ref, lb_ref, w_ref, x_ref,
                     g_ref, gn_ref, o_ref, hn_ref):
    ys = ys_ref[...]
    z = jnp.dot(ys.astype(BF16), wglu_ref[...], preferred_element_type=F32)
    a_s5 = (ys * _sigmoid(z)).astype(BF16)

    ones = _block_ones(LANE, RWKV_HEAD)
    y = yr_ref[...]
    inv_n = 1.0 / RWKV_HEAD
    mu = _seg_sum(y, ones) * inv_n
    yc = y - mu
    var = _seg_sum(yc * yc, ones) * inv_n
    yn = yc * lax.rsqrt(var + GN_EPS) * lw_ref[...] + lb_ref[...]
    a_rw = ((yn + bon_ref[...].astype(F32)) * gt_ref[...].astype(F32)).astype(BF16)

    half = a_s5.shape[1]
    out = (jnp.dot(a_s5, w_ref[:half, :], preferred_element_type=F32)
           + jnp.dot(a_rw, w_ref[half:, :], preferred_element_type=F32))
    x_new = x_ref[...] + _rms_rows(out, g_ref[...])
    o_ref[...] = x_new
    hn_ref[...] = _rms_rows(x_new, gn_ref[...]).astype(hn_ref.dtype)


def even_out_proj(y_s5, w_glu, y_rw, bonus, gate, lnx_w, lnx_b, w_out, layer, x, g, g_next, *, tm):
    m, hw = y_rw.shape
    s5w = y_s5.shape[1]
    d = w_out.shape[2]
    assert w_out.shape[1] == s5w + hw and m % tm == 0
    blk = lambda n: pl.BlockSpec((tm, n), lambda i: (i, 0))
    row = lambda n: pl.BlockSpec((1, n), lambda i: (0, 0))
    return pl.pallas_call(
        _even_out_kernel,
        grid=(m // tm,),
        in_specs=[
            blk(s5w), pl.BlockSpec((None, s5w, s5w), lambda i: (layer, 0, 0)),
            blk(hw), blk(hw), blk(hw), row(hw), row(hw),
            pl.BlockSpec((None, s5w + hw, d), lambda i: (layer, 0, 0)),
            blk(d), row(d), row(d),
        ],
        out_specs=[blk(d), blk(d)],
        out_shape=(jax.ShapeDtypeStruct((m, d), F32), jax.ShapeDtypeStruct((m, d), BF16)),
        compiler_params=_cparams(("arbitrary",)),
        name="even_out_proj",
    )(y_s5, w_glu, y_rw, bonus, gate, lnx_w.reshape(1, hw), lnx_b.reshape(1, hw), w_out, x,
      g.reshape(1, d), g_next.reshape(1, d))


def _lru_kernel(gate_ref, xb_ref, cw_ref, cb_ref, wr_ref, br_ref, wi_ref, bi_ref, lam_ref,
                o_ref, halo_ref, h_ref, a_scr, b_scr, *, tc):
    t_idx = pl.program_id(1)
    width = xb_ref.shape[1]
    nblk = tc // SUBLANE
    n_gate_blocks = width // LRU_BLOCK

    @pl.when(t_idx == 0)
    def _():
        halo_ref[...] = jnp.zeros_like(halo_ref)
        h_ref[...] = jnp.zeros_like(h_ref)

    xb = xb_ref[...].astype(F32)
    ext = jnp.concatenate([halo_ref[...], xb], axis=0)
    xc = xb * cw_ref[CONV_WIDTH - 1:CONV_WIDTH, :] + cb_ref[...]
    for d in range(1, CONV_WIDTH):
        sh = pltpu.roll(ext, d, axis=0)[SUBLANE:, :]
        xc = xc + sh * cw_ref[CONV_WIDTH - 1 - d:CONV_WIDTH - d, :]
    halo_ref[...] = xb[tc - SUBLANE:, :]

    sp = _softplus(-lam_ref[...])

    for n in range(n_gate_blocks):
        lanes = slice(n * LRU_BLOCK, (n + 1) * LRU_BLOCK)
        xn = xc[:, lanes]
        xnb = xn.astype(BF16)
        gr = jnp.dot(xnb, wr_ref[n].astype(BF16), preferred_element_type=F32) + br_ref[:, lanes]
        gi = jnp.dot(xnb, wi_ref[n].astype(BF16), preferred_element_type=F32) + bi_ref[:, lanes]
        log_a = -LRU_C * _sigmoid(gr) * sp[:, lanes]
        a = jnp.exp(log_a)
        mult = jnp.sqrt(-jnp.tanh(log_a) * (a * a + 1.0))
        a_scr[:, lanes] = a
        b_scr[:, lanes] = mult * _sigmoid(gi) * xn

    grp = LRU_SCAN_LANES
    row = lax.broadcasted_iota(jnp.int32, (SUBLANE, grp), 0)

    def blk(bidx, carry):
        rows = pl.ds(pl.multiple_of(bidx * SUBLANE, SUBLANE), SUBLANE)
        for g in range(width // grp):
            lanes = slice(g * grp, (g + 1) * grp)
            av = a_scr[rows, lanes]
            bv = b_scr[rows, lanes]
            for d in (1, 2, 4):
                keep = row >= d
                ash = jnp.where(keep, pltpu.roll(av, d, axis=0), 1.0)
                bsh = jnp.where(keep, pltpu.roll(bv, d, axis=0), 0.0)
                bv = bv + av * bsh
                av = av * ash
            hv = bv + av * h_ref[:, lanes]
            b_scr[rows, lanes] = hv
            h_ref[:, lanes] = jnp.broadcast_to(hv[SUBLANE - 1:SUBLANE, :], (SUBLANE, grp))
        return carry

    lax.fori_loop(0, nblk, blk, 0)
    o_ref[...] = (b_scr[...] * gate_ref[...].astype(F32)).astype(o_ref.dtype)


def lru_block(p_odd, conv_w, conv_b, w_r, b_r, w_i, b_i, lam, layer, *, batch, tc):
    m = p_odd.shape[0]
    width = lam.shape[0]
    t_len = m // batch
    nt = t_len // tc
    assert t_len % tc == 0 and p_odd.shape[1] == 2 * width
    row = pl.BlockSpec((1, width), lambda b, t: (0, 0))
    blkw = pl.BlockSpec((None,) + w_r.shape[1:], lambda b, t: (layer, 0, 0, 0))
    return pl.pallas_call(
        functools.partial(_lru_kernel, tc=tc),
        grid=(batch, nt),
        in_specs=[
            pl.BlockSpec((tc, width), lambda b, t: (b * nt + t, 0)),
            pl.BlockSpec((tc, width), lambda b, t: (b * nt + t, 1)),
            pl.BlockSpec((CONV_WIDTH, width), lambda b, t: (0, 0)),
            row, blkw, row, blkw, row, row,
        ],
        out_specs=pl.BlockSpec((tc, width), lambda b, t: (b * nt + t, 0)),
        out_shape=jax.ShapeDtypeStruct((m, width), BF16),
        scratch_shapes=[
            pltpu.VMEM((SUBLANE, width), F32), pltpu.VMEM((SUBLANE, width), F32),
            pltpu.VMEM((tc, width), F32), pltpu.VMEM((tc, width), F32),
        ],
        compiler_params=_cparams(("arbitrary", "arbitrary")),
        name="lru_block",
    )(p_odd, p_odd, conv_w, conv_b.reshape(1, width), w_r, b_r.reshape(1, width),
      w_i, b_i.reshape(1, width), lam.reshape(1, width))


def _s5_block_weights(bb_re, bb_im, c_re, c_im):
    c, g, p = bb_re.shape
    gpt = LANE // c
    tiles = g // gpt
    eye = jnp.eye(gpt, dtype=F32)

    def b_blk(bb):
        bb = bb.reshape(c, tiles, gpt, p)
        return jnp.einsum('cjgp,gh->jgchp', bb, eye).reshape(tiles, gpt * c, gpt * p)

    def c_blk(cm):
        cm = cm.reshape(tiles, gpt, c, p)
        return jnp.einsum('jgcp,gh->jhpgc', cm, eye).reshape(tiles, gpt * p, gpt * c)

    return b_blk(bb_re), b_blk(bb_im), c_blk(c_re), c_blk(c_im)


def _even_mixer(x, h, g_post, g_ffn, batch, idx, w_in, shift_mu, lam_re, lam_im, log_dt,
                b_re, b_im, c_re, c_im, d_skip, w_glu, w0, w2, a0, a2, g2, k_k, k_a, r_k,
                lnx_w, lnx_b, w_out):
    hw = w0.shape[0]
    s5w = d_skip.shape[0]
    n_main = s5w + 3 * hw
    p_main = in_proj(h, w_in, idx, n_main, tm=2048, tn=512, out_dtype=BF16, name="even_in_proj")
    lrw = w_in.shape[2] - n_main
    lr_pad = -(-lrw // LANE) * LANE
    w_lr = jnp.pad(w_in[idx, :, n_main:], ((0, 0), (0, lr_pad - lrw)))[None]
    lr = in_proj(h, w_lr, 0, lr_pad, tm=2048, tn=lr_pad, out_dtype=F32, name="even_lr_proj")

    a_re, a_im, bb_re, bb_im = s5_discretise(lam_re, lam_im, log_dt, b_re, b_im)
    bblk_re, bblk_im, cblk_re, cblk_im = _s5_block_weights(bb_re, bb_im, c_re, c_im)
    tiles = bblk_re.shape[0]
    wk, wb, wc, a8r, a8i = s5_tables(bblk_re, bblk_im, cblk_re, cblk_im,
                                     a_re.reshape(tiles, 1, -1), a_im.reshape(tiles, 1, -1),
                                     d_skip.reshape(tiles, 1, LANE))
    y_s5 = s5_mix(p_main, wk, wb, wc, a8r, a8i, batch=batch, tc=2048)

    r, lw, k, v, a, b, bonus, g = rwkv_prep(p_main, lr, shift_mu, w0, w2, a0, a2, g2,
                                            k_k, k_a, r_k.reshape(-1), batch=batch, tm=256)
    y_rw = rwkv_scan(r, lw, k, v, a, b, batch=batch, tb=512)

    return even_out_proj(y_s5, w_glu, y_rw, bonus, g, lnx_w, lnx_b, w_out, idx, x, g_post, g_ffn,
                         tm=256)


def _odd_mixer(x, h, g_post, g_ffn, batch, idx, w_in, conv_w, conv_b, w_r, b_r, w_i, b_i,
               lam, w_out):
    width = lam.shape[0]
    p_odd = in_proj(h, w_in, idx, 2 * width, tm=2048, tn=512, out_dtype=BF16,
                    n_gelu_tiles=width // 512, name="odd_in_proj")
    hg = lru_block(p_odd, conv_w, conv_b, w_r, b_r, w_i, b_i, lam, idx, batch=batch, tc=256)
    return out_proj_norm_residual([hg], w_out, idx, x, g_post, g_ffn, tm=512,
                                  name="odd_out_proj")


def _ffn(x, h, g_post, g_next, layer, w_gate, w_up, w_down):
    act = swiglu_up(h, w_gate, w_up, layer, tm=2048, tn=512)
    return matmul_norm_residual(act, w_down, layer, x, g_post, g_next, tm=1024, tk=512,
                                n_chunk=512, name="ffn_down")


def kernel(x, ev_w_in, ev_shift_mu, s5_lam_re, s5_lam_im, s5_log_dt, s5_b_re, s5_b_im, s5_c_re, s5_c_im, s5_d, s5_w_glu, rw_w0, rw_w2, rw_a0, rw_a2, rw_g2, rw_k_k, rw_k_a, rw_r_k, rw_lnx_w, rw_lnx_b, ev_w_out, od_w_in, od_conv_w, od_conv_b, lru_w_r, lru_b_r, lru_w_i, lru_b_i, lru_lam, od_w_out, ffn_w_gate, ffn_w_up, ffn_w_down, norm_mix_pre, norm_mix_post, norm_ffn_pre, norm_ffn_post):
    batch, t_len, d = x.shape
    depth = ffn_w_gate.shape[0]
    xf = x.reshape(batch * t_len, d)
    s5_w_glu, ev_w_out, od_w_out = (w.astype(BF16) for w in (s5_w_glu, ev_w_out, od_w_out))
    hm = rmsnorm(xf, norm_mix_pre[0], tm=512)
    for layer in range(depth):
        i = layer // 2
        if layer % 2 == 0:
            xf, hf = _even_mixer(xf, hm, norm_mix_post[layer],
                                 norm_ffn_pre[layer], batch, i,
                                 ev_w_in, ev_shift_mu[i], s5_lam_re[i], s5_lam_im[i],
                                 s5_log_dt[i], s5_b_re[i], s5_b_im[i], s5_c_re[i], s5_c_im[i],
                                 s5_d[i], s5_w_glu, rw_w0[i], rw_w2[i], rw_a0[i], rw_a2[i],
                                 rw_g2[i], rw_k_k[i], rw_k_a[i], rw_r_k[i], rw_lnx_w[i],
                                 rw_lnx_b[i], ev_w_out)
        else:
            xf, hf = _odd_mixer(xf, hm, norm_mix_post[layer],
                                norm_ffn_pre[layer], batch, i,
                                od_w_in, od_conv_w[i], od_conv_b[i], lru_w_r, lru_b_r[i],
                                lru_w_i, lru_b_i[i], lru_lam[i], od_w_out)
        if layer + 1 < depth:
            xf, hm = _ffn(xf, hf, norm_ffn_post[layer], norm_mix_pre[layer + 1], layer,
                          ffn_w_gate, ffn_w_up, ffn_w_down)
        else:
            xf = _ffn(xf, hf, norm_ffn_post[layer], None, layer,
                      ffn_w_gate, ffn_w_up, ffn_w_down)
    return xf.reshape(batch, t_len, d)
```

```python
import functools
import math

import jax
import jax.numpy as jnp
from jax import lax
from jax.experimental import pallas as pl
from jax.experimental.pallas import tpu as pltpu

F32 = jnp.float32
BF16 = jnp.bfloat16

NORM_EPS = 1e-6
GN_EPS = 64e-5
LRU_C = 8.0
RWKV_HEAD = 64
CONV_WIDTH = 4
LRU_BLOCK = 256

LANE = 128
SUBLANE = 8
VMEM_LIMIT = 56 * 1024 * 1024

RW_CHUNK = 64
RW_PAIRS = 2
S5_BLOCK = 8
LRU_SCAN_LANES = 512
FFN_CAST_TILES = 2


def _cparams(sem):
    return pltpu.CompilerParams(dimension_semantics=sem, vmem_limit_bytes=VMEM_LIMIT)


def _gelu(x):
    c = math.sqrt(2.0 / math.pi)
    return 0.5 * x * (1.0 + jnp.tanh(c * (x + 0.044715 * (x * x * x))))


def _sigmoid(x):
    return 1.0 / (1.0 + jnp.exp(-x))


def _softplus(x):
    return jnp.maximum(x, 0.0) + jnp.log(1.0 + jnp.exp(-jnp.abs(x)))


def _rms_rows(y, g):
    ms = jnp.mean(y * y, axis=-1, keepdims=True)
    return y * lax.rsqrt(ms + NORM_EPS) * g


def _rmsnorm_kernel(x_ref, g_ref, o_ref):
    o_ref[...] = _rms_rows(x_ref[...], g_ref[...]).astype(o_ref.dtype)


def rmsnorm(x, g, *, tm):
    m, d = x.shape
    assert m % tm == 0
    return pl.pallas_call(
        _rmsnorm_kernel,
        grid=(m // tm,),
        in_specs=[pl.BlockSpec((tm, d), lambda i: (i, 0)), pl.BlockSpec((1, d), lambda i: (0, 0))],
        out_specs=pl.BlockSpec((tm, d), lambda i: (i, 0)),
        out_shape=jax.ShapeDtypeStruct((m, d), BF16),
        compiler_params=_cparams(("arbitrary",)),
        name="rmsnorm",
    )(x, g.reshape(1, d))


def _in_proj_kernel(h_ref, w_ref, o_ref, *, n_gelu_tiles):
    j = pl.program_id(1)
    acc = jnp.dot(h_ref[...], w_ref[...].astype(BF16), preferred_element_type=F32)
    if n_gelu_tiles == 0:
        o_ref[...] = acc.astype(o_ref.dtype)
    else:
        @pl.when(j < n_gelu_tiles)
        def _():
            o_ref[...] = _gelu(acc).astype(o_ref.dtype)

        @pl.when(j >= n_gelu_tiles)
        def _():
            o_ref[...] = acc.astype(o_ref.dtype)


def in_proj(h, w, layer, n_out, *, tm, tn, out_dtype, n_gelu_tiles=0, name):
    m, d = h.shape
    assert m % tm == 0 and n_out % tn == 0 and w.shape[1] == d and h.dtype == BF16
    return pl.pallas_call(
        functools.partial(_in_proj_kernel, n_gelu_tiles=n_gelu_tiles),
        grid=(m // tm, n_out // tn),
        in_specs=[
            pl.BlockSpec((tm, d), lambda i, j: (i, 0)),
            pl.BlockSpec((None, d, tn), lambda i, j: (layer, 0, j)),
        ],
        out_specs=pl.BlockSpec((tm, tn), lambda i, j: (i, j)),
        out_shape=jax.ShapeDtypeStruct((m, n_out), out_dtype),
        compiler_params=_cparams(("arbitrary", "arbitrary")),
        name=name,
    )(h, w)


def _swiglu_up_kernel(h_ref, wg_ref, wu_ref, wd_ref, o_ref, wdb_ref):
    h = h_ref[...]
    gate = jnp.dot(h, wg_ref[...].astype(BF16), preferred_element_type=F32)
    up = jnp.dot(h, wu_ref[...].astype(BF16), preferred_element_type=F32)
    o_ref[...] = (gate * _sigmoid(gate) * up).astype(o_ref.dtype)

    @pl.when(pl.program_id(0) < FFN_CAST_TILES)
    def _():
        wdb_ref[...] = wd_ref[...].astype(wdb_ref.dtype)


def swiglu_up(h, w_gate, w_up, w_down, layer, *, tm, tn):
    m, d = h.shape
    n = w_gate.shape[2]
    nj = n // tn
    rows = tn // FFN_CAST_TILES
    last = n // rows - 1
    assert m % tm == 0 and n % tn == 0 and tn % FFN_CAST_TILES == 0 and h.dtype == BF16
    assert m // tm >= FFN_CAST_TILES and w_down.shape[1:] == (n, d)

    def cast_block(i, j):
        return jnp.where(i < FFN_CAST_TILES, FFN_CAST_TILES * j + i, last)

    return pl.pallas_call(
        _swiglu_up_kernel,
        grid=(m // tm, nj),
        in_specs=[
            pl.BlockSpec((tm, d), lambda i, j: (i, 0)),
            pl.BlockSpec((None, d, tn), lambda i, j: (layer, 0, j)),
            pl.BlockSpec((None, d, tn), lambda i, j: (layer, 0, j)),
            pl.BlockSpec((None, rows, d), lambda i, j: (layer, cast_block(i, j), 0)),
        ],
        out_specs=[pl.BlockSpec((tm, tn), lambda i, j: (i, j)),
                   pl.BlockSpec((rows, d), lambda i, j: (cast_block(i, j), 0))],
        out_shape=(jax.ShapeDtypeStruct((m, n), BF16), jax.ShapeDtypeStruct((n, d), BF16)),
        compiler_params=_cparams(("arbitrary", "arbitrary")),
        name="ffn_up",
    )(h, w_gate, w_up, w_down)


def _mm_norm_res_kernel(a_ref, w_ref, x_ref, g_ref, *rest, emit_next, n_chunk):
    if emit_next:
        gn_ref, o_ref, hn_ref = rest
    else:
        (o_ref,) = rest
    k = pl.program_id(1)
    d = o_ref.shape[1]

    def accumulate(first):
        a = a_ref[...]
        for n in range(0, d, n_chunk):
            cols = slice(n, n + n_chunk)
            part = jnp.dot(a, w_ref[:, cols], preferred_element_type=F32)
            if first:
                o_ref[:, cols] = part
            else:
                o_ref[:, cols] += part

    @pl.when(k == 0)
    def _():
        accumulate(True)

    @pl.when(k > 0)
    def _():
        accumulate(False)

    @pl.when(k == pl.num_programs(1) - 1)
    def _():
        x_new = x_ref[...] + _rms_rows(o_ref[...], g_ref[...])
        o_ref[...] = x_new
        if emit_next:
            hn_ref[...] = _rms_rows(x_new, gn_ref[...]).astype(hn_ref.dtype)


def matmul_norm_residual(a, w, layer, x, g, g_next, *, tm, tk, n_chunk, name):
    m, kdim = a.shape
    d = w.shape[2]
    assert m % tm == 0 and kdim % tk == 0 and w.shape[1] == kdim and d % n_chunk == 0
    emit_next = g_next is not None
    row = pl.BlockSpec((1, d), lambda i, k: (0, 0))
    blk = pl.BlockSpec((tm, d), lambda i, k: (i, 0))
    f32_out = jax.ShapeDtypeStruct((m, d), F32)
    operands = [a, w, x, g.reshape(1, d)] + ([g_next.reshape(1, d)] if emit_next else [])
    return pl.pallas_call(
        functools.partial(_mm_norm_res_kernel, emit_next=emit_next, n_chunk=n_chunk),
        grid=(m // tm, kdim // tk),
        in_specs=[
            pl.BlockSpec((tm, tk), lambda i, k: (i, k)),
            pl.BlockSpec((None, tk, d), lambda i, k: (layer, k, 0)),
            blk, row,
        ] + ([row] if emit_next else []),
        out_specs=[blk, blk] if emit_next else blk,
        out_shape=(f32_out, jax.ShapeDtypeStruct((m, d), BF16)) if emit_next else f32_out,
        compiler_params=_cparams(("arbitrary", "arbitrary")),
        name=name,
    )(*operands)


def _out_proj_kernel(*refs, n_lhs):
    a_refs = refs[:n_lhs]
    w_ref, x_ref, g_ref, gn_ref, o_ref, hn_ref = refs[n_lhs:]
    y = None
    k0 = 0
    for a_ref in a_refs:
        kw = a_ref.shape[1]
        part = jnp.dot(a_ref[...], w_ref[k0:k0 + kw, :], preferred_element_type=F32)
        y = part if y is None else y + part
        k0 += kw
    x_new = x_ref[...] + _rms_rows(y, g_ref[...])
    o_ref[...] = x_new
    hn_ref[...] = _rms_rows(x_new, gn_ref[...]).astype(hn_ref.dtype)


def out_proj_norm_residual(lhs, w, layer, x, g, g_next, *, tm, name):
    m = x.shape[0]
    kdim, d = w.shape[1], w.shape[2]
    assert sum(a.shape[1] for a in lhs) == kdim and m % tm == 0
    row = pl.BlockSpec((1, d), lambda i: (0, 0))
    return pl.pallas_call(
        functools.partial(_out_proj_kernel, n_lhs=len(lhs)),
        grid=(m // tm,),
        in_specs=[pl.BlockSpec((tm, a.shape[1]), lambda i: (i, 0)) for a in lhs] + [
            pl.BlockSpec((None, kdim, d), lambda i: (layer, 0, 0)),
            pl.BlockSpec((tm, d), lambda i: (i, 0)),
            row, row,
        ],
        out_specs=[pl.BlockSpec((tm, d), lambda i: (i, 0))] * 2,
        out_shape=(jax.ShapeDtypeStruct((m, d), F32), jax.ShapeDtypeStruct((m, d), BF16)),
        compiler_params=_cparams(("arbitrary",)),
        name=name,
    )(*lhs, w, x, g.reshape(1, d), g_next.reshape(1, d))


def _block_ones(n, seg):
    r = lax.broadcasted_iota(jnp.int32, (n, n), 0) // seg
    c = lax.broadcasted_iota(jnp.int32, (n, n), 1) // seg
    return (r == c).astype(BF16)


def _seg_sum(x, ones):
    outs = []
    for j in range(x.shape[1] // LANE):
        xj = x[:, j * LANE:(j + 1) * LANE]
        hi = xj.astype(BF16)
        lo = (xj - hi.astype(F32)).astype(BF16)
        s = (jnp.dot(hi, ones, preferred_element_type=F32)
             + jnp.dot(lo, ones, preferred_element_type=F32))
        outs.append(s)
    return jnp.concatenate(outs, axis=1) if len(outs) > 1 else outs[0]


def _s5_disc_kernel(lr_ref, li_ref, ldt_ref, bre_ref, bim_ref,
                    are_ref, aim_ref, bbre_ref, bbim_ref):
    lr = lr_ref[...]
    li = li_ref[...]
    dt = jnp.exp(ldt_ref[...])
    mag = jnp.exp(lr * dt)
    a_re = mag * jnp.cos(li * dt)
    a_im = mag * jnp.sin(li * dt)
    den = lr * lr + li * li
    nr = a_re - 1.0
    ni = a_im
    gam_re = (nr * lr + ni * li) / den
    gam_im = (ni * lr - nr * li) / den
    are_ref[...] = a_re
    aim_ref[...] = a_im
    for c in range(bre_ref.shape[0]):
        br = bre_ref[c]
        bi = bim_ref[c]
        bbre_ref[c] = gam_re * br - gam_im * bi
        bbim_ref[c] = gam_re * bi + gam_im * br


def s5_discretise(lam_re, lam_im, log_dt, b_re, b_im):
    g, p, c = b_re.shape
    ldt = jnp.broadcast_to(log_dt[:, None], (g, p))
    b_re_t = jnp.transpose(b_re, (2, 0, 1))
    b_im_t = jnp.transpose(b_im, (2, 0, 1))
    gp = jax.ShapeDtypeStruct((g, p), F32)
    cgp = jax.ShapeDtypeStruct((c, g, p), F32)
    return pl.pallas_call(
        _s5_disc_kernel,
        out_shape=(gp, gp, cgp, cgp),
        name="s5_discretise",
    )(lam_re, lam_im, ldt, b_re_t, b_im_t)


def _cmul(ar, ai, br, bi):
    return ar * br - ai * bi, ar * bi + ai * br


def _s5_tables_kernel(bre_ref, bim_ref, cre_ref, cim_ref, ar_ref, ai_ref, acr_ref, aci_ref, d_ref,
                      wk_ref, wb_ref, wc_ref, a8r_ref, a8i_ref):
    bre, bim = bre_ref[...], bim_ref[...]
    cre, cim = cre_ref[...], cim_ref[...]
    ar, ai = ar_ref[...], ai_ref[...]
    acr, aci = acr_ref[...], aci_ref[...]
    ml = bre.shape[1]
    eye = (lax.broadcasted_iota(jnp.int32, (LANE, LANE), 0)
           == lax.broadcasted_iota(jnp.int32, (LANE, LANE), 1)).astype(F32)

    pr, pi = [jnp.ones_like(ar)], [jnp.zeros_like(ai)]
    for _ in range(S5_BLOCK):
        nr, ni = _cmul(pr[-1], pi[-1], ar, ai)
        pr.append(nr)
        pi.append(ni)
    qr, qi = [acr], [aci]
    for _ in range(S5_BLOCK - 1):
        nr, ni = _cmul(qr[-1], qi[-1], acr, aci)
        qr.append(nr)
        qi.append(ni)

    hi = lax.Precision.HIGHEST
    for tau in range(S5_BLOCK):
        mre = bre * pr[tau] - bim * pi[tau]
        mim = bre * pi[tau] + bim * pr[tau]
        k = (jnp.dot(mre, cre, preferred_element_type=F32, precision=hi)
             - jnp.dot(mim, cim, preferred_element_type=F32, precision=hi))
        if tau == 0:
            k = k + eye * d_ref[...]
        wk_ref[tau * LANE:(tau + 1) * LANE, :] = k.astype(wk_ref.dtype)
        r = S5_BLOCK - 1 - tau
        wb_ref[r * LANE:(r + 1) * LANE, :ml] = mre.astype(wb_ref.dtype)
        wb_ref[r * LANE:(r + 1) * LANE, ml:] = mim.astype(wb_ref.dtype)

    for r in range(S5_BLOCK):
        dr, di = qr[r], qi[r]
        wc_ref[:ml, r * LANE:(r + 1) * LANE] = (cre * dr - cim * di).astype(wc_ref.dtype)
        wc_ref[ml:, r * LANE:(r + 1) * LANE] = (-(cre * di + cim * dr)).astype(wc_ref.dtype)

    a8r_ref[...] = pr[S5_BLOCK]
    a8i_ref[...] = pi[S5_BLOCK]


def s5_tables(bblk_re, bblk_im, cblk_re, cblk_im, a_re, a_im, d_skip):
    tiles, _, ml = bblk_re.shape
    a_col_re = jnp.broadcast_to(jnp.swapaxes(a_re, 1, 2), (tiles, ml, LANE))
    a_col_im = jnp.broadcast_to(jnp.swapaxes(a_im, 1, 2), (tiles, ml, LANE))
    kdim = S5_BLOCK * LANE
    t3 = lambda a, b: pl.BlockSpec((None, a, b), lambda j: (j, 0, 0))
    return pl.pallas_call(
        _s5_tables_kernel,
        grid=(tiles,),
        in_specs=[t3(LANE, ml), t3(LANE, ml), t3(ml, LANE), t3(ml, LANE), t3(1, ml), t3(1, ml),
                  t3(ml, LANE), t3(ml, LANE), t3(1, LANE)],
        out_specs=[t3(kdim, LANE), t3(kdim, 2 * ml), t3(2 * ml, kdim), t3(1, ml), t3(1, ml)],
        out_shape=(jax.ShapeDtypeStruct((tiles, kdim, LANE), BF16),
                   jax.ShapeDtypeStruct((tiles, kdim, 2 * ml), BF16),
                   jax.ShapeDtypeStruct((tiles, 2 * ml, kdim), BF16),
                   jax.ShapeDtypeStruct((tiles, 1, ml), F32),
                   jax.ShapeDtypeStruct((tiles, 1, ml), F32)),
        compiler_params=_cparams(("arbitrary",)),
        name="s5_tables",
    )(bblk_re, bblk_im, cblk_re, cblk_im, a_re, a_im, a_col_re, a_col_im, d_skip)


def _s5_mix_kernel(u_ref, wk_ref, wb_ref, wc_ref, a8r_ref, a8i_ref, o_ref,
                   car_ref, cai_ref, tab_ref, uf_ref, xr_ref, xi_ref, y_ref, *, tc):
    t_idx = pl.program_id(2)
    ml = a8r_ref.shape[1]
    nb = tc // S5_BLOCK
    row = lax.broadcasted_iota(jnp.int32, (SUBLANE, ml), 0)

    @pl.when(t_idx == 0)
    def _():
        car_ref[...] = jnp.zeros_like(car_ref)
        cai_ref[...] = jnp.zeros_like(cai_ref)
        a1r = jnp.broadcast_to(a8r_ref[...], (SUBLANE, ml))
        a1i = jnp.broadcast_to(a8i_ref[...], (SUBLANE, ml))
        a2r, a2i = _cmul(a1r, a1i, a1r, a1i)
        a4r, a4i = _cmul(a2r, a2i, a2r, a2i)
        pr, pi = a1r, a1i
        cr, ci = a1r, a1i
        for r in range(1, SUBLANE):
            cr, ci = _cmul(cr, ci, a1r, a1i)
            pr = jnp.where(row >= r, cr, pr)
            pi = jnp.where(row >= r, ci, pi)
        for lvl, (d, xr, xi) in enumerate(((1, a1r, a1i), (2, a2r, a2i), (4, a4r, a4i))):
            tab_ref[2 * lvl] = jnp.where(row >= d, xr, 0.0)
            tab_ref[2 * lvl + 1] = jnp.where(row >= d, xi, 0.0)
        tab_ref[6] = pr
        tab_ref[7] = pi

    uf = u_ref[...].astype(F32)
    uf_ref[...] = uf

    pos = lax.broadcasted_iota(jnp.int32, (tc, LANE), 0) % S5_BLOCK
    shifted = [uf.astype(BF16)]
    for tau in range(1, S5_BLOCK):
        shifted.append(jnp.where(pos >= tau, pltpu.roll(uf, tau, axis=0), 0.0).astype(BF16))
    y_ref[...] = jnp.dot(jnp.concatenate(shifted, axis=1), wk_ref[...], preferred_element_type=F32)

    strided = [uf_ref[pl.ds(r, nb, stride=S5_BLOCK), :].astype(BF16) for r in range(S5_BLOCK)]
    x = jnp.dot(jnp.concatenate(strided, axis=1), wb_ref[...], preferred_element_type=F32)
    xr_ref[...] = x[:, :ml]
    xi_ref[...] = x[:, ml:]

    def blk(b, carry):
        rows = pl.ds(pl.multiple_of(b * SUBLANE, SUBLANE), SUBLANE)
        sr = xr_ref[rows, :]
        si = xi_ref[rows, :]
        for lvl, d in enumerate((1, 2, 4)):
            kr = tab_ref[2 * lvl]
            ki = tab_ref[2 * lvl + 1]
            rr = pltpu.roll(sr, d, axis=0)
            ri = pltpu.roll(si, d, axis=0)
            sr, si = sr + kr * rr - ki * ri, si + kr * ri + ki * rr
        pr = tab_ref[6]
        pi = tab_ref[7]
        cbr = car_ref[...]
        cbi = cai_ref[...]
        sr = sr + pr * cbr - pi * cbi
        si = si + pr * cbi + pi * cbr
        xr_ref[rows, :] = jnp.where(row == 0, cbr, pltpu.roll(sr, 1, axis=0))
        xi_ref[rows, :] = jnp.where(row == 0, cbi, pltpu.roll(si, 1, axis=0))
        car_ref[...] = jnp.broadcast_to(sr[SUBLANE - 1:SUBLANE, :], (SUBLANE, ml))
        cai_ref[...] = jnp.broadcast_to(si[SUBLANE - 1:SUBLANE, :], (SUBLANE, ml))
        return carry

    lax.fori_loop(0, nb // SUBLANE, blk, 0)

    s_in = jnp.concatenate([xr_ref[...], xi_ref[...]], axis=1).astype(BF16)
    z = jnp.dot(s_in, wc_ref[...], preferred_element_type=F32)
    for r in range(S5_BLOCK):
        rows = pl.ds(r, nb, stride=S5_BLOCK)
        y_ref[rows, :] = y_ref[rows, :] + z[:, r * LANE:(r + 1) * LANE]
    o_ref[...] = _gelu(y_ref[...])


def s5_mix(p_main, wk, wb, wc, a8r, a8i, *, batch, tc):
    m = p_main.shape[0]
    t_len = m // batch
    tiles, kdim, _ = wk.shape
    ml = a8r.shape[2]
    nt = t_len // tc
    nb = tc // S5_BLOCK
    assert t_len % tc == 0 and nb % SUBLANE == 0 and p_main.dtype == BF16
    t3 = lambda a, b: pl.BlockSpec((None, a, b), lambda bi, j, t: (j, 0, 0))
    io = pl.BlockSpec((tc, LANE), lambda bi, j, t: (bi * nt + t, j))
    return pl.pallas_call(
        functools.partial(_s5_mix_kernel, tc=tc),
        grid=(batch, tiles, nt),
        in_specs=[io, t3(kdim, LANE), t3(kdim, 2 * ml), t3(2 * ml, kdim), t3(1, ml), t3(1, ml)],
        out_specs=io,
        out_shape=jax.ShapeDtypeStruct((m, tiles * LANE), F32),
        scratch_shapes=[
            pltpu.VMEM((SUBLANE, ml), F32), pltpu.VMEM((SUBLANE, ml), F32),
            pltpu.VMEM((8, SUBLANE, ml), F32),
            pltpu.VMEM((tc, LANE), F32),
            pltpu.VMEM((nb, ml), F32), pltpu.VMEM((nb, ml), F32),
            pltpu.VMEM((tc, LANE), F32),
        ],
        compiler_params=_cparams(("arbitrary", "arbitrary", "arbitrary")),
        name="s5_mix",
    )(p_main, wk, wb, wc, a8r, a8i)


def _shift_mix(z, prev_row, mu):
    row = lax.broadcasted_iota(jnp.int32, z.shape, 0)
    zs = jnp.where(row == 0, jnp.broadcast_to(prev_row, z.shape), pltpu.roll(z, 1, axis=0))
    return z + (zs - z) * mu


def _rwkv_prep_kernel(r_ref, k_ref, v_ref, lr_ref, rp_ref, kp_ref, vp_ref, lrp_ref,
                      mur_ref, muk_ref, muv_ref, mulr_ref,
                      w0_ref, w2_ref, a0_ref, a2_ref, g2_ref, kk_ref, ka_ref, rk_ref,
                      ro_ref, lw_ref, ko_ref, vo_ref, ao_ref, bo_ref, bon_ref, go_ref,
                      *, tiles_per_batch):
    i = pl.program_id(0)
    first = (i % tiles_per_batch) == 0

    def prev(ref):
        last = ref.shape[0] - 1
        return jnp.where(first, 0.0, ref[last:last + 1, :].astype(F32))

    r = _shift_mix(r_ref[...].astype(F32), prev(rp_ref), mur_ref[...])
    k = _shift_mix(k_ref[...].astype(F32), prev(kp_ref), muk_ref[...])
    v = _shift_mix(v_ref[...].astype(F32), prev(vp_ref), muv_ref[...])
    lr = _shift_mix(lr_ref[...], prev(lrp_ref), mulr_ref[...])

    wl = w0_ref[...] + jnp.dot(jnp.tanh(lr).astype(BF16), w2_ref[...].astype(BF16),
                               preferred_element_type=F32)
    w = -_softplus(-wl) - 0.5
    lw_ref[...] = -jnp.exp(w)
    a = _sigmoid(a0_ref[...] + jnp.dot(lr.astype(BF16), a2_ref[...].astype(BF16),
                                       preferred_element_type=F32))
    go_ref[...] = jnp.dot(_sigmoid(lr).astype(BF16), g2_ref[...].astype(BF16),
                          preferred_element_type=F32).astype(go_ref.dtype)

    ones = _block_ones(LANE, RWKV_HEAD)
    kk = k * kk_ref[...]
    kk = kk * lax.rsqrt(jnp.maximum(_seg_sum(kk * kk, ones), 1e-24))
    k = k * (1.0 + (a - 1.0) * ka_ref[...])
    ro_ref[...] = r.astype(ro_ref.dtype)
    ko_ref[...] = k.astype(ko_ref.dtype)
    vo_ref[...] = v.astype(vo_ref.dtype)
    ao_ref[...] = (-kk).astype(ao_ref.dtype)
    bo_ref[...] = (kk * a).astype(bo_ref.dtype)
    bon_ref[...] = (_seg_sum(r * k * rk_ref[...], ones) * v).astype(bon_ref.dtype)


def rwkv_prep(p_main, lr, mu, w0, w2, a0, a2, g2, k_k, k_a, r_k, *, batch, tm):
    m = p_main.shape[0]
    t_len = m // batch
    hw = w0.shape[0]
    lrw = lr.shape[1]
    w_rank, a_rank, g_rank = w2.shape[0], a2.shape[0], g2.shape[0]
    nblk8 = tm // SUBLANE
    halo = 2 * SUBLANE
    nblk16 = tm // halo
    assert t_len % tm == 0 and p_main.shape[1] == 4 * hw and p_main.dtype == BF16

    def cur(c):
        return pl.BlockSpec((tm, hw), lambda i: (i, c))

    def prv(c):
        return pl.BlockSpec((halo, hw), lambda i: (jnp.maximum(i * nblk16 - 1, 0), c))

    row = lambda n: pl.BlockSpec((1, n), lambda i: (0, 0))
    mat = lambda a: pl.BlockSpec(a.shape, lambda i: (0, 0))
    used = w_rank + a_rank + g_rank
    mu_lr = jnp.pad(mu[3 * hw:], (0, lrw - used))
    w2 = jnp.pad(w2, ((0, lrw - w_rank), (0, 0)))
    a2 = jnp.pad(a2, ((w_rank, lrw - w_rank - a_rank), (0, 0)))
    g2 = jnp.pad(g2, ((w_rank + a_rank, lrw - used), (0, 0)))
    out = jax.ShapeDtypeStruct((m, hw), BF16)
    out_lw = jax.ShapeDtypeStruct((m, hw), F32)
    return pl.pallas_call(
        functools.partial(_rwkv_prep_kernel, tiles_per_batch=t_len // tm),
        grid=(m // tm,),
        in_specs=[
            cur(1), cur(2), cur(3), pl.BlockSpec((tm, lrw), lambda i: (i, 0)),
            prv(1), prv(2), prv(3),
            pl.BlockSpec((SUBLANE, lrw), lambda i: (jnp.maximum(i * nblk8 - 1, 0), 0)),
            row(hw), row(hw), row(hw), row(lrw),
            row(hw), mat(w2), row(hw), mat(a2), mat(g2), row(hw), row(hw), row(hw),
        ],
        out_specs=[pl.BlockSpec((tm, hw), lambda i: (i, 0))] * 8,
        out_shape=(out, out_lw) + (out,) * 6,
        compiler_params=_cparams(("arbitrary",)),
        name="rwkv_prep",
    )(p_main, p_main, p_main, lr, p_main, p_main, p_main, lr,
      mu[:hw].reshape(1, hw), mu[hw:2 * hw].reshape(1, hw), mu[2 * hw:3 * hw].reshape(1, hw),
      mu_lr.reshape(1, lrw),
      w0.reshape(1, hw), w2, a0.reshape(1, hw), a2, g2, k_k.reshape(1, hw),
      k_a.reshape(1, hw), r_k.reshape(1, hw))


def _mm(a, b):
    return jnp.dot(a.astype(BF16), b.astype(BF16), preferred_element_type=F32)


def _mm_nt(a, b):
    return lax.dot_general(a.astype(BF16), b.astype(BF16), (((1,), (1,)), ((), ())),
                           preferred_element_type=F32)


def _mm_tn(a, b):
    return lax.dot_general(a.astype(BF16), b.astype(BF16), (((0,), (0,)), ((), ())),
                           preferred_element_type=F32)


def _split3(x):
    hi = x.astype(BF16)
    r1 = x - hi.astype(F32)
    mid = r1.astype(BF16)
    lo = (r1 - mid.astype(F32)).astype(BF16)
    return hi, mid, lo


def _rwkv_scan_kernel(r_ref, lw_ref, k_ref, v_ref, a_ref, b_ref, y_ref,
                      z_ref, m_scr, zc_scr, ra_scr, yv_scr, ge_scr, *, n_chunks):
    c_len = RW_CHUNK
    n2 = 2 * c_len

    @pl.when(pl.program_id(2) == 0)
    def _():
        z_ref[...] = jnp.zeros_like(z_ref)

    lane = lax.broadcasted_iota(jnp.int32, (c_len, LANE), 1)
    head_a = lane < RWKV_HEAD
    ri = lax.broadcasted_iota(jnp.int32, (n2, n2), 0)
    ci = lax.broadcasted_iota(jnp.int32, (n2, n2), 1)
    same = (ri // c_len) == (ci // c_len)
    strict = same & ((ci % c_len) < (ri % c_len))
    incl = same & ((ci % c_len) <= (ri % c_len))
    eye = (ri == ci).astype(F32)
    rc = lax.broadcasted_iota(jnp.int32, (c_len, c_len), 0)
    cc = lax.broadcasted_iota(jnp.int32, (c_len, c_len), 1)
    tri = (cc <= rc).astype(BF16)

    def stack(x):
        return jnp.concatenate([jnp.where(head_a, x, 0.0), jnp.where(head_a, 0.0, x)], axis=0)

    def fold(x):
        return x[:c_len, :] + x[c_len:, :]

    units = [(pp, c) for pp in range(RW_PAIRS) for c in range(n_chunks)]

    def window(pp, c):
        return slice(c * c_len, (c + 1) * c_len), slice(pp * LANE, (pp + 1) * LANE)

    pre = []
    for u, (pp, c) in enumerate(units):
        rows, lanes = window(pp, c)
        lw = lw_ref[rows, lanes]
        r, k, v, a, b = (ref[rows, lanes].astype(F32) for ref in (r_ref, k_ref, v_ref, a_ref, b_ref))
        cl = sum(jnp.dot(tri, part, preferred_element_type=F32) for part in _split3(lw))
        cl_end = cl[c_len - 1:c_len, :]
        g_inv = jnp.exp(-cl)
        g_tail = jnp.exp(cl_end - cl)
        r_t = r * jnp.exp(cl)
        ra_scr[u] = r_t
        ge_scr[u] = jnp.broadcast_to(jnp.exp(cl_end), (SUBLANE, LANE))
        pre.append(dict(
            a_s=stack(a * jnp.exp(cl - lw)).astype(BF16), r_s=stack(r_t).astype(BF16),
            b_s=stack(b * g_inv).astype(BF16), k_s=stack(k * g_inv).astype(BF16),
            v_s=stack(v).astype(BF16), bg_s=stack(b * g_tail).astype(BF16),
            kg_s=stack(k * g_tail).astype(BF16)))

    scs = [_mm_nt(jnp.concatenate([p["a_s"], p["r_s"]], axis=0),
                  jnp.concatenate([p["b_s"], p["k_s"]], axis=0)) for p in pre]
    l_ak = [jnp.where(strict, sc[:n2, n2:], 0.0).astype(BF16) for sc in scs]
    m_rb = [jnp.where(incl, sc[n2:, :n2], 0.0).astype(BF16) for sc in scs]
    m_rk = [jnp.where(incl, sc[n2:, n2:], 0.0).astype(BF16) for sc in scs]

    xs = [jnp.where(strict, sc[:n2, :n2], 0.0) for sc in scs]
    ts = [eye + x for x in xs]
    n = 2
    while n < c_len:
        xs = [_mm(x, x) for x in xs]
        ts = [t + _mm(t, x) for t, x in zip(ts, xs)]
        n *= 2

    lvs = [_mm(l, p["v_s"]) for l, p in zip(l_ak, pre)]
    tatv = [_mm(t, jnp.concatenate([p["a_s"], lv.astype(BF16)], axis=1)).astype(BF16)
            for t, p, lv in zip(ts, pre, lvs)]
    mtatv = [_mm(m, tv) for m, tv in zip(m_rb, tatv)]
    mkv = [_mm(m, p["v_s"]) for m, p in zip(m_rk, pre)]
    gz = [_mm_tn(tv, p["bg_s"]) for tv, p in zip(tatv, pre)]
    vk = [_mm_tn(p["v_s"], p["kg_s"]) for p in pre]
    for u in range(len(units)):
        ra_scr[u] = ra_scr[u] + fold(mtatv[u][:, :LANE])
        yv_scr[u] = fold(mtatv[u][:, LANE:] + mkv[u])
        m_scr[u] = gz[u][:LANE, :]
        zc_scr[u] = gz[u][LANE:, :] + vk[u]

    zts = [z_ref[pp] for pp in range(RW_PAIRS)]
    for c in range(n_chunks):
        for pp in range(RW_PAIRS):
            u = pp * n_chunks + c
            rows, lanes = window(pp, c)
            zt = zts[pp]
            y_ref[rows, lanes] = _mm_nt(ra_scr[u], zt) + yv_scr[u]
            zts[pp] = zt * ge_scr[u][0:1, :] + _mm(zt, m_scr[u]) + zc_scr[u]
    for pp in range(RW_PAIRS):
        z_ref[pp] = zts[pp]


def rwkv_scan(r, lw, k, v, a, b, *, batch, tb):
    m, hw = r.shape
    t_len = m // batch
    nt = t_len // tb
    nc = tb // RW_CHUNK
    lanes = RW_PAIRS * LANE
    nu = RW_PAIRS * nc
    assert t_len % tb == 0 and tb % RW_CHUNK == 0 and hw % lanes == 0
    spec = pl.BlockSpec((tb, lanes), lambda bi, p, t: (bi * nt + t, p))
    return pl.pallas_call(
        functools.partial(_rwkv_scan_kernel, n_chunks=nc),
        grid=(batch, hw // lanes, nt),
        in_specs=[spec] * 6,
        out_specs=spec,
        out_shape=jax.ShapeDtypeStruct((m, hw), F32),
        scratch_shapes=[
            pltpu.VMEM((RW_PAIRS, LANE, LANE), F32),
            pltpu.VMEM((nu, LANE, LANE), F32), pltpu.VMEM((nu, LANE, LANE), F32),
            pltpu.VMEM((nu, RW_CHUNK, LANE), F32), pltpu.VMEM((nu, RW_CHUNK, LANE), F32),
            pltpu.VMEM((nu, SUBLANE, LANE), F32),
        ],
        compiler_params=_cparams(("arbitrary", "arbitrary", "arbitrary")),
        name="rwkv_scan",
    )(r, lw, k, v, a, b)


def _even_out_kernel(ys_ref, wglu_ref, yr_ref, bon_ref, gt_ref, lw_ref, lb_ref, w_ref, x_ref,
                     g_ref, gn_ref, o_ref, hn_ref):
    ys = ys_ref[...]
    z = jnp.dot(ys.astype(BF16), wglu_ref[...], preferred_element_type=F32)
    a_s5 = (ys * _sigmoid(z)).astype(BF16)

    ones = _block_ones(LANE, RWKV_HEAD)
    y = yr_ref[...]
    inv_n = 1.0 / RWKV_HEAD
    mu = _seg_sum(y, ones) * inv_n
    yc = y - mu
    var = _seg_sum(yc * yc, ones) * inv_n
    yn = yc * lax.rsqrt(var + GN_EPS) * lw_ref[...] + lb_ref[...]
    a_rw = ((yn + bon_ref[...].astype(F32)) * gt_ref[...].astype(F32)).astype(BF16)

    half = a_s5.shape[1]
    out = (jnp.dot(a_s5, w_ref[:half, :], preferred_element_type=F32)
           + jnp.dot(a_rw, w_ref[half:, :], preferred_element_type=F32))
    x_new = x_ref[...] + _rms_rows(out, g_ref[...])
    o_ref[...] = x_new
    hn_ref[...] = _rms_rows(x_new, gn_ref[...]).astype(hn_ref.dtype)


def even_out_proj(y_s5, w_glu, y_rw, bonus, gate, lnx_w, lnx_b, w_out, layer, x, g, g_next, *, tm):
    m, hw = y_rw.shape
    s5w = y_s5.shape[1]
    d = w_out.shape[2]
    assert w_out.shape[1] == s5w + hw and m % tm == 0
    blk = lambda n: pl.BlockSpec((tm, n), lambda i: (i, 0))
    row = lambda n: pl.BlockSpec((1, n), lambda i: (0, 0))
    return pl.pallas_call(
        _even_out_kernel,
        grid=(m // tm,),
        in_specs=[
            blk(s5w), pl.BlockSpec((None, s5w, s5w), lambda i: (layer, 0, 0)),
            blk(hw), blk(hw), blk(hw), row(hw), row(hw),
            pl.BlockSpec((None, s5w + hw, d), lambda i: (layer, 0, 0)),
            blk(d), row(d), row(d),
        ],
        out_specs=[blk(d), blk(d)],
        out_shape=(jax.ShapeDtypeStruct((m, d), F32), jax.ShapeDtypeStruct((m, d), BF16)),
        compiler_params=_cparams(("arbitrary",)),
        name="even_out_proj",
    )(y_s5, w_glu, y_rw, bonus, gate, lnx_w.reshape(1, hw), lnx_b.reshape(1, hw), w_out, x,
      g.reshape(1, d), g_next.reshape(1, d))


def _lru_kernel(gate_ref, xb_ref, cw_ref, cb_ref, wr_ref, br_ref, wi_ref, bi_ref, lam_ref,
                o_ref, halo_ref, h_ref, a_scr, b_scr, *, tc):
    t_idx = pl.program_id(1)
    width = xb_ref.shape[1]
    nblk = tc // SUBLANE
    n_gate_blocks = width // LRU_BLOCK

    @pl.when(t_idx == 0)
    def _():
        halo_ref[...] = jnp.zeros_like(halo_ref)
        h_ref[...] = jnp.zeros_like(h_ref)

    xb = xb_ref[...].astype(F32)
    ext = jnp.concatenate([halo_ref[...], xb], axis=0)
    xc = xb * cw_ref[CONV_WIDTH - 1:CONV_WIDTH, :] + cb_ref[...]
    for d in range(1, CONV_WIDTH):
        sh = pltpu.roll(ext, d, axis=0)[SUBLANE:, :]
        xc = xc + sh * cw_ref[CONV_WIDTH - 1 - d:CONV_WIDTH - d, :]
    halo_ref[...] = xb[tc - SUBLANE:, :]

    sp = _softplus(-lam_ref[...])

    for n in range(n_gate_blocks):
        lanes = slice(n * LRU_BLOCK, (n + 1) * LRU_BLOCK)
        xn = xc[:, lanes]
        xnb = xn.astype(BF16)
        gr = jnp.dot(xnb, wr_ref[n].astype(BF16), preferred_element_type=F32) + br_ref[:, lanes]
        gi = jnp.dot(xnb, wi_ref[n].astype(BF16), preferred_element_type=F32) + bi_ref[:, lanes]
        log_a = -LRU_C * _sigmoid(gr) * sp[:, lanes]
        a = jnp.exp(log_a)
        mult = jnp.sqrt(-jnp.tanh(log_a) * (a * a + 1.0))
        a_scr[:, lanes] = a
        b_scr[:, lanes] = mult * _sigmoid(gi) * xn

    grp = LRU_SCAN_LANES
    row = lax.broadcasted_iota(jnp.int32, (SUBLANE, grp), 0)

    def blk(bidx, carry):
        rows = pl.ds(pl.multiple_of(bidx * SUBLANE, SUBLANE), SUBLANE)
        for g in range(width // grp):
            lanes = slice(g * grp, (g + 1) * grp)
            av = a_scr[rows, lanes]
            bv = b_scr[rows, lanes]
            for d in (1, 2, 4):
                keep = row >= d
                ash = jnp.where(keep, pltpu.roll(av, d, axis=0), 1.0)
                bsh = jnp.where(keep, pltpu.roll(bv, d, axis=0), 0.0)
                bv = bv + av * bsh
                av = av * ash
            hv = bv + av * h_ref[:, lanes]
            b_scr[rows, lanes] = hv
            h_ref[:, lanes] = jnp.broadcast_to(hv[SUBLANE - 1:SUBLANE, :], (SUBLANE, grp))
        return carry

    lax.fori_loop(0, nblk, blk, 0)
    o_ref[...] = (b_scr[...] * gate_ref[...].astype(F32)).astype(o_ref.dtype)


def lru_block(p_odd, conv_w, conv_b, w_r, b_r, w_i, b_i, lam, layer, *, batch, tc):
    m = p_odd.shape[0]
    width = lam.shape[0]
    t_len = m // batch
    nt = t_len // tc
    assert t_len % tc == 0 and p_odd.shape[1] == 2 * width
    row = pl.BlockSpec((1, width), lambda b, t: (0, 0))
    blkw = pl.BlockSpec((None,) + w_r.shape[1:], lambda b, t: (layer, 0, 0, 0))
    return pl.pallas_call(
        functools.partial(_lru_kernel, tc=tc),
        grid=(batch, nt),
        in_specs=[
            pl.BlockSpec((tc, width), lambda b, t: (b * nt + t, 0)),
            pl.BlockSpec((tc, width), lambda b, t: (b * nt + t, 1)),
            pl.BlockSpec((CONV_WIDTH, width), lambda b, t: (0, 0)),
            row, blkw, row, blkw, row, row,
        ],
        out_specs=pl.BlockSpec((tc, width), lambda b, t: (b * nt + t, 0)),
        out_shape=jax.ShapeDtypeStruct((m, width), BF16),
        scratch_shapes=[
            pltpu.VMEM((SUBLANE, width), F32), pltpu.VMEM((SUBLANE, width), F32),
            pltpu.VMEM((tc, width), F32), pltpu.VMEM((tc, width), F32),
        ],
        compiler_params=_cparams(("arbitrary", "arbitrary")),
        name="lru_block",
    )(p_odd, p_odd, conv_w, conv_b.reshape(1, width), w_r, b_r.reshape(1, width),
      w_i, b_i.reshape(1, width), lam.reshape(1, width))


def _s5_block_weights(bb_re, bb_im, c_re, c_im):
    c, g, p = bb_re.shape
    gpt = LANE // c
    tiles = g // gpt
    eye = jnp.eye(gpt, dtype=F32)

    def b_blk(bb):
        bb = bb.reshape(c, tiles, gpt, p)
        return jnp.einsum('cjgp,gh->jgchp', bb, eye).reshape(tiles, gpt * c, gpt * p)

    def c_blk(cm):
        cm = cm.reshape(tiles, gpt, c, p)
        return jnp.einsum('jgcp,gh->jhpgc', cm, eye).reshape(tiles, gpt * p, gpt * c)

    return b_blk(bb_re), b_blk(bb_im), c_blk(c_re), c_blk(c_im)


def _even_mixer(x, h, g_post, g_ffn, batch, idx, w_in, shift_mu, lam_re, lam_im, log_dt,
                b_re, b_im, c_re, c_im, d_skip, w_glu, w0, w2, a0, a2, g2, k_k, k_a, r_k,
                lnx_w, lnx_b, w_out):
    hw = w0.shape[0]
    s5w = d_skip.shape[0]
    n_main = s5w + 3 * hw
    p_main = in_proj(h, w_in, idx, n_main, tm=2048, tn=512, out_dtype=BF16, name="even_in_proj")
    lrw = w_in.shape[2] - n_main
    lr_pad = -(-lrw // LANE) * LANE
    w_lr = jnp.pad(w_in[idx, :, n_main:], ((0, 0), (0, lr_pad - lrw)))[None]
    lr = in_proj(h, w_lr, 0, lr_pad, tm=2048, tn=lr_pad, out_dtype=F32, name="even_lr_proj")

    a_re, a_im, bb_re, bb_im = s5_discretise(lam_re, lam_im, log_dt, b_re, b_im)
    bblk_re, bblk_im, cblk_re, cblk_im = _s5_block_weights(bb_re, bb_im, c_re, c_im)
    tiles = bblk_re.shape[0]
    wk, wb, wc, a8r, a8i = s5_tables(bblk_re, bblk_im, cblk_re, cblk_im,
                                     a_re.reshape(tiles, 1, -1), a_im.reshape(tiles, 1, -1),
                                     d_skip.reshape(tiles, 1, LANE))
    y_s5 = s5_mix(p_main, wk, wb, wc, a8r, a8i, batch=batch, tc=2048)

    r, lw, k, v, a, b, bonus, g = rwkv_prep(p_main, lr, shift_mu, w0, w2, a0, a2, g2,
                                            k_k, k_a, r_k.reshape(-1), batch=batch, tm=256)
    y_rw = rwkv_scan(r, lw, k, v, a, b, batch=batch, tb=512)

    return even_out_proj(y_s5, w_glu, y_rw, bonus, g, lnx_w, lnx_b, w_out, idx, x, g_post, g_ffn,
                         tm=256)


def _odd_mixer(x, h, g_post, g_ffn, batch, idx, w_in, conv_w, conv_b, w_r, b_r, w_i, b_i,
               lam, w_out):
    width = lam.shape[0]
    p_odd = in_proj(h, w_in, idx, 2 * width, tm=2048, tn=512, out_dtype=BF16,
                    n_gelu_tiles=width // 512, name="odd_in_proj")
    hg = lru_block(p_odd, conv_w, conv_b, w_r, b_r, w_i, b_i, lam, idx, batch=batch, tc=256)
    return out_proj_norm_residual([hg], w_out, idx, x, g_post, g_ffn, tm=512,
                                  name="odd_out_proj")


def _ffn(x, h, g_post, g_next, layer, w_gate, w_up, w_down):
    act, w_down_bf16 = swiglu_up(h, w_gate, w_up, w_down, layer, tm=2048, tn=512)
    return matmul_norm_residual(act, w_down_bf16[None], 0, x, g_post, g_next, tm=1024, tk=512,
                                n_chunk=512, name="ffn_down")


def kernel(x, ev_w_in, ev_shift_mu, s5_lam_re, s5_lam_im, s5_log_dt, s5_b_re, s5_b_im, s5_c_re, s5_c_im, s5_d, s5_w_glu, rw_w0, rw_w2, rw_a0, rw_a2, rw_g2, rw_k_k, rw_k_a, rw_r_k, rw_lnx_w, rw_lnx_b, ev_w_out, od_w_in, od_conv_w, od_conv_b, lru_w_r, lru_b_r, lru_w_i, lru_b_i, lru_lam, od_w_out, ffn_w_gate, ffn_w_up, ffn_w_down, norm_mix_pre, norm_mix_post, norm_ffn_pre, norm_ffn_post):
    batch, t_len, d = x.shape
    depth = ffn_w_gate.shape[0]
    xf = x.reshape(batch * t_len, d)
    s5_w_glu, ev_w_out, od_w_out = (w.astype(BF16) for w in (s5_w_glu, ev_w_out, od_w_out))
    hm = rmsnorm(xf, norm_mix_pre[0], tm=512)
    for layer in range(depth):
        i = layer // 2
        if layer % 2 == 0:
            xf, hf = _even_mixer(xf, hm, norm_mix_post[layer],
                                 norm_ffn_pre[layer], batch, i,
                                 ev_w_in, ev_shift_mu[i], s5_lam_re[i], s5_lam_im[i],
                                 s5_log_dt[i], s5_b_re[i], s5_b_im[i], s5_c_re[i], s5_c_im[i],
                                 s5_d[i], s5_w_glu, rw_w0[i], rw_w2[i], rw_a0[i], rw_a2[i],
                                 rw_g2[i], rw_k_k[i], rw_k_a[i], rw_r_k[i], rw_lnx_w[i],
                                 rw_lnx_b[i], ev_w_out)
        else:
            xf, hf = _odd_mixer(xf, hm, norm_mix_post[layer],
                                norm_ffn_pre[layer], batch, i,
                                od_w_in, od_conv_w[i], od_conv_b[i], lru_w_r, lru_b_r[i],
                                lru_w_i, lru_b_i[i], lru_lam[i], od_w_out)
        if layer + 1 < depth:
            xf, hm = _ffn(xf, hf, norm_ffn_post[layer], norm_mix_pre[layer + 1], layer,
                          ffn_w_gate, ffn_w_up, ffn_w_down)
        else:
            xf = _ffn(xf, hf, norm_ffn_post[layer], None, layer,
                      ffn_w_gate, ffn_w_up, ffn_w_down)
    return xf.reshape(batch, t_len, d)
```

```python
import functools
import math

import jax
import jax.numpy as jnp
from jax import lax
from jax.experimental import pallas as pl
from jax.experimental.pallas import tpu as pltpu

F32 = jnp.float32
BF16 = jnp.bfloat16

NORM_EPS = 1e-6
GN_EPS = 64e-5
LRU_C = 8.0
RWKV_HEAD = 64
CONV_WIDTH = 4
LRU_BLOCK = 256

LANE = 128
SUBLANE = 8
VMEM_LIMIT = 56 * 1024 * 1024

RW_CHUNK = 64
RW_PAIRS = 4
S5_BLOCK = 8
LRU_SCAN_LANES = 512
FFN_CAST_TILES = 2


def _cparams(sem):
    return pltpu.CompilerParams(dimension_semantics=sem, vmem_limit_bytes=VMEM_LIMIT)


def _gelu(x):
    c = math.sqrt(2.0 / math.pi)
    return 0.5 * x * (1.0 + jnp.tanh(c * (x + 0.044715 * (x * x * x))))


def _sigmoid(x):
    return 1.0 / (1.0 + jnp.exp(-x))


def _softplus(x):
    return jnp.maximum(x, 0.0) + jnp.log(1.0 + jnp.exp(-jnp.abs(x)))


def _rms_rows(y, g):
    ms = jnp.mean(y * y, axis=-1, keepdims=True)
    return y * lax.rsqrt(ms + NORM_EPS) * g


def _rmsnorm_kernel(x_ref, g_ref, o_ref):
    o_ref[...] = _rms_rows(x_ref[...], g_ref[...]).astype(o_ref.dtype)


def rmsnorm(x, g, *, tm):
    m, d = x.shape
    assert m % tm == 0
    return pl.pallas_call(
        _rmsnorm_kernel,
        grid=(m // tm,),
        in_specs=[pl.BlockSpec((tm, d), lambda i: (i, 0)), pl.BlockSpec((1, d), lambda i: (0, 0))],
        out_specs=pl.BlockSpec((tm, d), lambda i: (i, 0)),
        out_shape=jax.ShapeDtypeStruct((m, d), BF16),
        compiler_params=_cparams(("arbitrary",)),
        name="rmsnorm",
    )(x, g.reshape(1, d))


def _in_proj_kernel(h_ref, w_ref, o_ref, *, n_gelu_tiles):
    j = pl.program_id(1)
    acc = jnp.dot(h_ref[...], w_ref[...].astype(BF16), preferred_element_type=F32)
    if n_gelu_tiles == 0:
        o_ref[...] = acc.astype(o_ref.dtype)
    else:
        @pl.when(j < n_gelu_tiles)
        def _():
            o_ref[...] = _gelu(acc).astype(o_ref.dtype)

        @pl.when(j >= n_gelu_tiles)
        def _():
            o_ref[...] = acc.astype(o_ref.dtype)


def in_proj(h, w, layer, n_out, *, tm, tn, out_dtype, n_gelu_tiles=0, name):
    m, d = h.shape
    assert m % tm == 0 and n_out % tn == 0 and w.shape[1] == d and h.dtype == BF16
    return pl.pallas_call(
        functools.partial(_in_proj_kernel, n_gelu_tiles=n_gelu_tiles),
        grid=(m // tm, n_out // tn),
        in_specs=[
            pl.BlockSpec((tm, d), lambda i, j: (i, 0)),
            pl.BlockSpec((None, d, tn), lambda i, j: (layer, 0, j)),
        ],
        out_specs=pl.BlockSpec((tm, tn), lambda i, j: (i, j)),
        out_shape=jax.ShapeDtypeStruct((m, n_out), out_dtype),
        compiler_params=_cparams(("arbitrary", "arbitrary")),
        name=name,
    )(h, w)


def _swiglu_up_kernel(h_ref, wg_ref, wu_ref, wd_ref, o_ref, wdb_ref):
    h = h_ref[...]
    gate = jnp.dot(h, wg_ref[...].astype(BF16), preferred_element_type=F32)
    up = jnp.dot(h, wu_ref[...].astype(BF16), preferred_element_type=F32)
    o_ref[...] = (gate * _sigmoid(gate) * up).astype(o_ref.dtype)

    @pl.when(pl.program_id(0) < FFN_CAST_TILES)
    def _():
        wdb_ref[...] = wd_ref[...].astype(wdb_ref.dtype)


def swiglu_up(h, w_gate, w_up, w_down, layer, *, tm, tn):
    m, d = h.shape
    n = w_gate.shape[2]
    nj = n // tn
    rows = tn // FFN_CAST_TILES
    last = n // rows - 1
    assert m % tm == 0 and n % tn == 0 and tn % FFN_CAST_TILES == 0 and h.dtype == BF16
    assert m // tm >= FFN_CAST_TILES and w_down.shape[1:] == (n, d)

    def cast_block(i, j):
        return jnp.where(i < FFN_CAST_TILES, FFN_CAST_TILES * j + i, last)

    return pl.pallas_call(
        _swiglu_up_kernel,
        grid=(m // tm, nj),
        in_specs=[
            pl.BlockSpec((tm, d), lambda i, j: (i, 0)),
            pl.BlockSpec((None, d, tn), lambda i, j: (layer, 0, j)),
            pl.BlockSpec((None, d, tn), lambda i, j: (layer, 0, j)),
            pl.BlockSpec((None, rows, d), lambda i, j: (layer, cast_block(i, j), 0)),
        ],
        out_specs=[pl.BlockSpec((tm, tn), lambda i, j: (i, j)),
                   pl.BlockSpec((rows, d), lambda i, j: (cast_block(i, j), 0))],
        out_shape=(jax.ShapeDtypeStruct((m, n), BF16), jax.ShapeDtypeStruct((n, d), BF16)),
        compiler_params=_cparams(("arbitrary", "arbitrary")),
        name="ffn_up",
    )(h, w_gate, w_up, w_down)


def _mm_norm_res_kernel(a_ref, w_ref, x_ref, g_ref, *rest, emit_next, n_chunk):
    if emit_next:
        gn_ref, o_ref, hn_ref = rest
    else:
        (o_ref,) = rest
    k = pl.program_id(1)
    d = o_ref.shape[1]

    def accumulate(first):
        a = a_ref[...]
        for n in range(0, d, n_chunk):
            cols = slice(n, n + n_chunk)
            part = jnp.dot(a, w_ref[:, cols], preferred_element_type=F32)
            if first:
                o_ref[:, cols] = part
            else:
                o_ref[:, cols] += part

    @pl.when(k == 0)
    def _():
        accumulate(True)

    @pl.when(k > 0)
    def _():
        accumulate(False)

    @pl.when(k == pl.num_programs(1) - 1)
    def _():
        x_new = x_ref[...] + _rms_rows(o_ref[...], g_ref[...])
        o_ref[...] = x_new
        if emit_next:
            hn_ref[...] = _rms_rows(x_new, gn_ref[...]).astype(hn_ref.dtype)


def matmul_norm_residual(a, w, layer, x, g, g_next, *, tm, tk, n_chunk, name):
    m, kdim = a.shape
    d = w.shape[2]
    assert m % tm == 0 and kdim % tk == 0 and w.shape[1] == kdim and d % n_chunk == 0
    emit_next = g_next is not None
    row = pl.BlockSpec((1, d), lambda i, k: (0, 0))
    blk = pl.BlockSpec((tm, d), lambda i, k: (i, 0))
    f32_out = jax.ShapeDtypeStruct((m, d), F32)
    operands = [a, w, x, g.reshape(1, d)] + ([g_next.reshape(1, d)] if emit_next else [])
    return pl.pallas_call(
        functools.partial(_mm_norm_res_kernel, emit_next=emit_next, n_chunk=n_chunk),
        grid=(m // tm, kdim // tk),
        in_specs=[
            pl.BlockSpec((tm, tk), lambda i, k: (i, k)),
            pl.BlockSpec((None, tk, d), lambda i, k: (layer, k, 0)),
            blk, row,
        ] + ([row] if emit_next else []),
        out_specs=[blk, blk] if emit_next else blk,
        out_shape=(f32_out, jax.ShapeDtypeStruct((m, d), BF16)) if emit_next else f32_out,
        compiler_params=_cparams(("arbitrary", "arbitrary")),
        name=name,
    )(*operands)


def _out_proj_kernel(*refs, n_lhs):
    a_refs = refs[:n_lhs]
    w_ref, x_ref, g_ref, gn_ref, o_ref, hn_ref = refs[n_lhs:]
    y = None
    k0 = 0
    for a_ref in a_refs:
        kw = a_ref.shape[1]
        part = jnp.dot(a_ref[...], w_ref[k0:k0 + kw, :], preferred_element_type=F32)
        y = part if y is None else y + part
        k0 += kw
    x_new = x_ref[...] + _rms_rows(y, g_ref[...])
    o_ref[...] = x_new
    hn_ref[...] = _rms_rows(x_new, gn_ref[...]).astype(hn_ref.dtype)


def out_proj_norm_residual(lhs, w, layer, x, g, g_next, *, tm, name):
    m = x.shape[0]
    kdim, d = w.shape[1], w.shape[2]
    assert sum(a.shape[1] for a in lhs) == kdim and m % tm == 0
    row = pl.BlockSpec((1, d), lambda i: (0, 0))
    return pl.pallas_call(
        functools.partial(_out_proj_kernel, n_lhs=len(lhs)),
        grid=(m // tm,),
        in_specs=[pl.BlockSpec((tm, a.shape[1]), lambda i: (i, 0)) for a in lhs] + [
            pl.BlockSpec((None, kdim, d), lambda i: (layer, 0, 0)),
            pl.BlockSpec((tm, d), lambda i: (i, 0)),
            row, row,
        ],
        out_specs=[pl.BlockSpec((tm, d), lambda i: (i, 0))] * 2,
        out_shape=(jax.ShapeDtypeStruct((m, d), F32), jax.ShapeDtypeStruct((m, d), BF16)),
        compiler_params=_cparams(("arbitrary",)),
        name=name,
    )(*lhs, w, x, g.reshape(1, d), g_next.reshape(1, d))


def _block_ones(n, seg):
    r = lax.broadcasted_iota(jnp.int32, (n, n), 0) // seg
    c = lax.broadcasted_iota(jnp.int32, (n, n), 1) // seg
    return (r == c).astype(BF16)


def _seg_sum(x, ones):
    outs = []
    for j in range(x.shape[1] // LANE):
        xj = x[:, j * LANE:(j + 1) * LANE]
        hi = xj.astype(BF16)
        lo = (xj - hi.astype(F32)).astype(BF16)
        s = (jnp.dot(hi, ones, preferred_element_type=F32)
             + jnp.dot(lo, ones, preferred_element_type=F32))
        outs.append(s)
    return jnp.concatenate(outs, axis=1) if len(outs) > 1 else outs[0]


def _s5_disc_kernel(lr_ref, li_ref, ldt_ref, bre_ref, bim_ref,
                    are_ref, aim_ref, bbre_ref, bbim_ref):
    lr = lr_ref[...]
    li = li_ref[...]
    dt = jnp.exp(ldt_ref[...])
    mag = jnp.exp(lr * dt)
    a_re = mag * jnp.cos(li * dt)
    a_im = mag * jnp.sin(li * dt)
    den = lr * lr + li * li
    nr = a_re - 1.0
    ni = a_im
    gam_re = (nr * lr + ni * li) / den
    gam_im = (ni * lr - nr * li) / den
    are_ref[...] = a_re
    aim_ref[...] = a_im
    for c in range(bre_ref.shape[0]):
        br = bre_ref[c]
        bi = bim_ref[c]
        bbre_ref[c] = gam_re * br - gam_im * bi
        bbim_ref[c] = gam_re * bi + gam_im * br


def s5_discretise(lam_re, lam_im, log_dt, b_re, b_im):
    g, p, c = b_re.shape
    ldt = jnp.broadcast_to(log_dt[:, None], (g, p))
    b_re_t = jnp.transpose(b_re, (2, 0, 1))
    b_im_t = jnp.transpose(b_im, (2, 0, 1))
    gp = jax.ShapeDtypeStruct((g, p), F32)
    cgp = jax.ShapeDtypeStruct((c, g, p), F32)
    return pl.pallas_call(
        _s5_disc_kernel,
        out_shape=(gp, gp, cgp, cgp),
        name="s5_discretise",
    )(lam_re, lam_im, ldt, b_re_t, b_im_t)


def _cmul(ar, ai, br, bi):
    return ar * br - ai * bi, ar * bi + ai * br


def _s5_tables_kernel(bre_ref, bim_ref, cre_ref, cim_ref, ar_ref, ai_ref, acr_ref, aci_ref, d_ref,
                      wk_ref, wb_ref, wc_ref, a8r_ref, a8i_ref):
    bre, bim = bre_ref[...], bim_ref[...]
    cre, cim = cre_ref[...], cim_ref[...]
    ar, ai = ar_ref[...], ai_ref[...]
    acr, aci = acr_ref[...], aci_ref[...]
    ml = bre.shape[1]
    eye = (lax.broadcasted_iota(jnp.int32, (LANE, LANE), 0)
           == lax.broadcasted_iota(jnp.int32, (LANE, LANE), 1)).astype(F32)

    pr, pi = [jnp.ones_like(ar)], [jnp.zeros_like(ai)]
    for _ in range(S5_BLOCK):
        nr, ni = _cmul(pr[-1], pi[-1], ar, ai)
        pr.append(nr)
        pi.append(ni)
    qr, qi = [acr], [aci]
    for _ in range(S5_BLOCK - 1):
        nr, ni = _cmul(qr[-1], qi[-1], acr, aci)
        qr.append(nr)
        qi.append(ni)

    hi = lax.Precision.HIGHEST
    for tau in range(S5_BLOCK):
        mre = bre * pr[tau] - bim * pi[tau]
        mim = bre * pi[tau] + bim * pr[tau]
        k = (jnp.dot(mre, cre, preferred_element_type=F32, precision=hi)
             - jnp.dot(mim, cim, preferred_element_type=F32, precision=hi))
        if tau == 0:
            k = k + eye * d_ref[...]
        wk_ref[tau * LANE:(tau + 1) * LANE, :] = k.astype(wk_ref.dtype)
        r = S5_BLOCK - 1 - tau
        wb_ref[r * LANE:(r + 1) * LANE, :ml] = mre.astype(wb_ref.dtype)
        wb_ref[r * LANE:(r + 1) * LANE, ml:] = mim.astype(wb_ref.dtype)

    for r in range(S5_BLOCK):
        dr, di = qr[r], qi[r]
        wc_ref[:ml, r * LANE:(r + 1) * LANE] = (cre * dr - cim * di).astype(wc_ref.dtype)
        wc_ref[ml:, r * LANE:(r + 1) * LANE] = (-(cre * di + cim * dr)).astype(wc_ref.dtype)

    a8r_ref[...] = pr[S5_BLOCK]
    a8i_ref[...] = pi[S5_BLOCK]


def s5_tables(bblk_re, bblk_im, cblk_re, cblk_im, a_re, a_im, d_skip):
    tiles, _, ml = bblk_re.shape
    a_col_re = jnp.broadcast_to(jnp.swapaxes(a_re, 1, 2), (tiles, ml, LANE))
    a_col_im = jnp.broadcast_to(jnp.swapaxes(a_im, 1, 2), (tiles, ml, LANE))
    kdim = S5_BLOCK * LANE
    t3 = lambda a, b: pl.BlockSpec((None, a, b), lambda j: (j, 0, 0))
    return pl.pallas_call(
        _s5_tables_kernel,
        grid=(tiles,),
        in_specs=[t3(LANE, ml), t3(LANE, ml), t3(ml, LANE), t3(ml, LANE), t3(1, ml), t3(1, ml),
                  t3(ml, LANE), t3(ml, LANE), t3(1, LANE)],
        out_specs=[t3(kdim, LANE), t3(kdim, 2 * ml), t3(2 * ml, kdim), t3(1, ml), t3(1, ml)],
        out_shape=(jax.ShapeDtypeStruct((tiles, kdim, LANE), BF16),
                   jax.ShapeDtypeStruct((tiles, kdim, 2 * ml), BF16),
                   jax.ShapeDtypeStruct((tiles, 2 * ml, kdim), BF16),
                   jax.ShapeDtypeStruct((tiles, 1, ml), F32),
                   jax.ShapeDtypeStruct((tiles, 1, ml), F32)),
        compiler_params=_cparams(("arbitrary",)),
        name="s5_tables",
    )(bblk_re, bblk_im, cblk_re, cblk_im, a_re, a_im, a_col_re, a_col_im, d_skip)


def _s5_mix_kernel(u_ref, wk_ref, wb_ref, wc_ref, a8r_ref, a8i_ref, o_ref,
                   car_ref, cai_ref, tab_ref, uf_ref, xr_ref, xi_ref, y_ref, *, tc):
    t_idx = pl.program_id(2)
    ml = a8r_ref.shape[1]
    nb = tc // S5_BLOCK
    row = lax.broadcasted_iota(jnp.int32, (SUBLANE, ml), 0)

    @pl.when(t_idx == 0)
    def _():
        car_ref[...] = jnp.zeros_like(car_ref)
        cai_ref[...] = jnp.zeros_like(cai_ref)
        a1r = jnp.broadcast_to(a8r_ref[...], (SUBLANE, ml))
        a1i = jnp.broadcast_to(a8i_ref[...], (SUBLANE, ml))
        a2r, a2i = _cmul(a1r, a1i, a1r, a1i)
        a4r, a4i = _cmul(a2r, a2i, a2r, a2i)
        pr, pi = a1r, a1i
        cr, ci = a1r, a1i
        for r in range(1, SUBLANE):
            cr, ci = _cmul(cr, ci, a1r, a1i)
            pr = jnp.where(row >= r, cr, pr)
            pi = jnp.where(row >= r, ci, pi)
        for lvl, (d, xr, xi) in enumerate(((1, a1r, a1i), (2, a2r, a2i), (4, a4r, a4i))):
            tab_ref[2 * lvl] = jnp.where(row >= d, xr, 0.0)
            tab_ref[2 * lvl + 1] = jnp.where(row >= d, xi, 0.0)
        tab_ref[6] = pr
        tab_ref[7] = pi

    uf = u_ref[...].astype(F32)
    uf_ref[...] = uf

    pos = lax.broadcasted_iota(jnp.int32, (tc, LANE), 0) % S5_BLOCK
    shifted = [uf.astype(BF16)]
    for tau in range(1, S5_BLOCK):
        shifted.append(jnp.where(pos >= tau, pltpu.roll(uf, tau, axis=0), 0.0).astype(BF16))
    y_ref[...] = jnp.dot(jnp.concatenate(shifted, axis=1), wk_ref[...], preferred_element_type=F32)

    strided = [uf_ref[pl.ds(r, nb, stride=S5_BLOCK), :].astype(BF16) for r in range(S5_BLOCK)]
    x = jnp.dot(jnp.concatenate(strided, axis=1), wb_ref[...], preferred_element_type=F32)
    xr_ref[...] = x[:, :ml]
    xi_ref[...] = x[:, ml:]

    def blk(b, carry):
        rows = pl.ds(pl.multiple_of(b * SUBLANE, SUBLANE), SUBLANE)
        sr = xr_ref[rows, :]
        si = xi_ref[rows, :]
        for lvl, d in enumerate((1, 2, 4)):
            kr = tab_ref[2 * lvl]
            ki = tab_ref[2 * lvl + 1]
            rr = pltpu.roll(sr, d, axis=0)
            ri = pltpu.roll(si, d, axis=0)
            sr, si = sr + kr * rr - ki * ri, si + kr * ri + ki * rr
        pr = tab_ref[6]
        pi = tab_ref[7]
        cbr = car_ref[...]
        cbi = cai_ref[...]
        sr = sr + pr * cbr - pi * cbi
        si = si + pr * cbi + pi * cbr
        xr_ref[rows, :] = jnp.where(row == 0, cbr, pltpu.roll(sr, 1, axis=0))
        xi_ref[rows, :] = jnp.where(row == 0, cbi, pltpu.roll(si, 1, axis=0))
        car_ref[...] = jnp.broadcast_to(sr[SUBLANE - 1:SUBLANE, :], (SUBLANE, ml))
        cai_ref[...] = jnp.broadcast_to(si[SUBLANE - 1:SUBLANE, :], (SUBLANE, ml))
        return carry

    lax.fori_loop(0, nb // SUBLANE, blk, 0)

    s_in = jnp.concatenate([xr_ref[...], xi_ref[...]], axis=1).astype(BF16)
    z = jnp.dot(s_in, wc_ref[...], preferred_element_type=F32)
    for r in range(S5_BLOCK):
        rows = pl.ds(r, nb, stride=S5_BLOCK)
        y_ref[rows, :] = y_ref[rows, :] + z[:, r * LANE:(r + 1) * LANE]
    o_ref[...] = _gelu(y_ref[...])


def s5_mix(p_main, wk, wb, wc, a8r, a8i, *, batch, tc):
    m = p_main.shape[0]
    t_len = m // batch
    tiles, kdim, _ = wk.shape
    ml = a8r.shape[2]
    nt = t_len // tc
    nb = tc // S5_BLOCK
    assert t_len % tc == 0 and nb % SUBLANE == 0 and p_main.dtype == BF16
    t3 = lambda a, b: pl.BlockSpec((None, a, b), lambda bi, j, t: (j, 0, 0))
    io = pl.BlockSpec((tc, LANE), lambda bi, j, t: (bi * nt + t, j))
    return pl.pallas_call(
        functools.partial(_s5_mix_kernel, tc=tc),
        grid=(batch, tiles, nt),
        in_specs=[io, t3(kdim, LANE), t3(kdim, 2 * ml), t3(2 * ml, kdim), t3(1, ml), t3(1, ml)],
        out_specs=io,
        out_shape=jax.ShapeDtypeStruct((m, tiles * LANE), F32),
        scratch_shapes=[
            pltpu.VMEM((SUBLANE, ml), F32), pltpu.VMEM((SUBLANE, ml), F32),
            pltpu.VMEM((8, SUBLANE, ml), F32),
            pltpu.VMEM((tc, LANE), F32),
            pltpu.VMEM((nb, ml), F32), pltpu.VMEM((nb, ml), F32),
            pltpu.VMEM((tc, LANE), F32),
        ],
        compiler_params=_cparams(("arbitrary", "arbitrary", "arbitrary")),
        name="s5_mix",
    )(p_main, wk, wb, wc, a8r, a8i)


def _shift_mix(z, prev_row, mu):
    row = lax.broadcasted_iota(jnp.int32, z.shape, 0)
    zs = jnp.where(row == 0, jnp.broadcast_to(prev_row, z.shape), pltpu.roll(z, 1, axis=0))
    return z + (zs - z) * mu


def _rwkv_prep_kernel(r_ref, k_ref, v_ref, lr_ref, rp_ref, kp_ref, vp_ref, lrp_ref,
                      mur_ref, muk_ref, muv_ref, mulr_ref,
                      w0_ref, w2_ref, a0_ref, a2_ref, g2_ref, kk_ref, ka_ref, rk_ref,
                      ro_ref, lw_ref, ko_ref, vo_ref, ao_ref, bo_ref, bon_ref, go_ref,
                      *, tiles_per_batch):
    i = pl.program_id(0)
    first = (i % tiles_per_batch) == 0

    def prev(ref):
        last = ref.shape[0] - 1
        return jnp.where(first, 0.0, ref[last:last + 1, :].astype(F32))

    r = _shift_mix(r_ref[...].astype(F32), prev(rp_ref), mur_ref[...])
    k = _shift_mix(k_ref[...].astype(F32), prev(kp_ref), muk_ref[...])
    v = _shift_mix(v_ref[...].astype(F32), prev(vp_ref), muv_ref[...])
    lr = _shift_mix(lr_ref[...], prev(lrp_ref), mulr_ref[...])

    wl = w0_ref[...] + jnp.dot(jnp.tanh(lr).astype(BF16), w2_ref[...].astype(BF16),
                               preferred_element_type=F32)
    w = -_softplus(-wl) - 0.5
    lw_ref[...] = -jnp.exp(w)
    a = _sigmoid(a0_ref[...] + jnp.dot(lr.astype(BF16), a2_ref[...].astype(BF16),
                                       preferred_element_type=F32))
    go_ref[...] = jnp.dot(_sigmoid(lr).astype(BF16), g2_ref[...].astype(BF16),
                          preferred_element_type=F32).astype(go_ref.dtype)

    ones = _block_ones(LANE, RWKV_HEAD)
    kk = k * kk_ref[...]
    kk = kk * lax.rsqrt(jnp.maximum(_seg_sum(kk * kk, ones), 1e-24))
    k = k * (1.0 + (a - 1.0) * ka_ref[...])
    ro_ref[...] = r.astype(ro_ref.dtype)
    ko_ref[...] = k.astype(ko_ref.dtype)
    vo_ref[...] = v.astype(vo_ref.dtype)
    ao_ref[...] = (-kk).astype(ao_ref.dtype)
    bo_ref[...] = (kk * a).astype(bo_ref.dtype)
    bon_ref[...] = (_seg_sum(r * k * rk_ref[...], ones) * v).astype(bon_ref.dtype)


def rwkv_prep(p_main, lr, mu, w0, w2, a0, a2, g2, k_k, k_a, r_k, *, batch, tm):
    m = p_main.shape[0]
    t_len = m // batch
    hw = w0.shape[0]
    lrw = lr.shape[1]
    w_rank, a_rank, g_rank = w2.shape[0], a2.shape[0], g2.shape[0]
    nblk8 = tm // SUBLANE
    halo = 2 * SUBLANE
    nblk16 = tm // halo
    assert t_len % tm == 0 and p_main.shape[1] == 4 * hw and p_main.dtype == BF16

    def cur(c):
        return pl.BlockSpec((tm, hw), lambda i: (i, c))

    def prv(c):
        return pl.BlockSpec((halo, hw), lambda i: (jnp.maximum(i * nblk16 - 1, 0), c))

    row = lambda n: pl.BlockSpec((1, n), lambda i: (0, 0))
    mat = lambda a: pl.BlockSpec(a.shape, lambda i: (0, 0))
    used = w_rank + a_rank + g_rank
    mu_lr = jnp.pad(mu[3 * hw:], (0, lrw - used))
    w2 = jnp.pad(w2, ((0, lrw - w_rank), (0, 0)))
    a2 = jnp.pad(a2, ((w_rank, lrw - w_rank - a_rank), (0, 0)))
    g2 = jnp.pad(g2, ((w_rank + a_rank, lrw - used), (0, 0)))
    out = jax.ShapeDtypeStruct((m, hw), BF16)
    out_lw = jax.ShapeDtypeStruct((m, hw), F32)
    return pl.pallas_call(
        functools.partial(_rwkv_prep_kernel, tiles_per_batch=t_len // tm),
        grid=(m // tm,),
        in_specs=[
            cur(1), cur(2), cur(3), pl.BlockSpec((tm, lrw), lambda i: (i, 0)),
            prv(1), prv(2), prv(3),
            pl.BlockSpec((SUBLANE, lrw), lambda i: (jnp.maximum(i * nblk8 - 1, 0), 0)),
            row(hw), row(hw), row(hw), row(lrw),
            row(hw), mat(w2), row(hw), mat(a2), mat(g2), row(hw), row(hw), row(hw),
        ],
        out_specs=[pl.BlockSpec((tm, hw), lambda i: (i, 0))] * 8,
        out_shape=(out, out_lw) + (out,) * 6,
        compiler_params=_cparams(("arbitrary",)),
        name="rwkv_prep",
    )(p_main, p_main, p_main, lr, p_main, p_main, p_main, lr,
      mu[:hw].reshape(1, hw), mu[hw:2 * hw].reshape(1, hw), mu[2 * hw:3 * hw].reshape(1, hw),
      mu_lr.reshape(1, lrw),
      w0.reshape(1, hw), w2, a0.reshape(1, hw), a2, g2, k_k.reshape(1, hw),
      k_a.reshape(1, hw), r_k.reshape(1, hw))


def _mm(a, b):
    return jnp.dot(a.astype(BF16), b.astype(BF16), preferred_element_type=F32)


def _mm_nt(a, b):
    return lax.dot_general(a.astype(BF16), b.astype(BF16), (((1,), (1,)), ((), ())),
                           preferred_element_type=F32)


def _mm_tn(a, b):
    return lax.dot_general(a.astype(BF16), b.astype(BF16), (((0,), (0,)), ((), ())),
                           preferred_element_type=F32)


def _split3(x):
    hi = x.astype(BF16)
    r1 = x - hi.astype(F32)
    mid = r1.astype(BF16)
    lo = (r1 - mid.astype(F32)).astype(BF16)
    return hi, mid, lo


def _rwkv_scan_kernel(r_ref, lw_ref, k_ref, v_ref, a_ref, b_ref, y_ref,
                      z_ref, m_scr, zc_scr, ra_scr, yv_scr, ge_scr, *, n_chunks):
    c_len = RW_CHUNK
    n2 = 2 * c_len

    @pl.when(pl.program_id(2) == 0)
    def _():
        z_ref[...] = jnp.zeros_like(z_ref)

    lane = lax.broadcasted_iota(jnp.int32, (c_len, LANE), 1)
    head_a = lane < RWKV_HEAD
    ri = lax.broadcasted_iota(jnp.int32, (n2, n2), 0)
    ci = lax.broadcasted_iota(jnp.int32, (n2, n2), 1)
    same = (ri // c_len) == (ci // c_len)
    strict = same & ((ci % c_len) < (ri % c_len))
    incl = same & ((ci % c_len) <= (ri % c_len))
    eye = (ri == ci).astype(F32)
    rc = lax.broadcasted_iota(jnp.int32, (c_len, c_len), 0)
    cc = lax.broadcasted_iota(jnp.int32, (c_len, c_len), 1)
    tri = (cc <= rc).astype(BF16)

    def stack(x):
        return jnp.concatenate([jnp.where(head_a, x, 0.0), jnp.where(head_a, 0.0, x)], axis=0)

    def fold(x):
        return x[:c_len, :] + x[c_len:, :]

    units = [(pp, c) for pp in range(RW_PAIRS) for c in range(n_chunks)]

    def window(pp, c):
        return slice(c * c_len, (c + 1) * c_len), slice(pp * LANE, (pp + 1) * LANE)

    pre = []
    for u, (pp, c) in enumerate(units):
        rows, lanes = window(pp, c)
        lw = lw_ref[rows, lanes]
        r, k, v, a, b = (ref[rows, lanes].astype(F32) for ref in (r_ref, k_ref, v_ref, a_ref, b_ref))
        cl = sum(jnp.dot(tri, part, preferred_element_type=F32) for part in _split3(lw))
        cl_end = cl[c_len - 1:c_len, :]
        g_inv = jnp.exp(-cl)
        g_tail = jnp.exp(cl_end - cl)
        r_t = r * jnp.exp(cl)
        ra_scr[u] = r_t
        ge_scr[u] = jnp.broadcast_to(jnp.exp(cl_end), (SUBLANE, LANE))
        pre.append(dict(
            a_s=stack(a * jnp.exp(cl - lw)).astype(BF16), r_s=stack(r_t).astype(BF16),
            b_s=stack(b * g_inv).astype(BF16), k_s=stack(k * g_inv).astype(BF16),
            v_s=stack(v).astype(BF16), bg_s=stack(b * g_tail).astype(BF16),
            kg_s=stack(k * g_tail).astype(BF16)))

    scs = [_mm_nt(jnp.concatenate([p["a_s"], p["r_s"]], axis=0),
                  jnp.concatenate([p["b_s"], p["k_s"]], axis=0)) for p in pre]
    l_ak = [jnp.where(strict, sc[:n2, n2:], 0.0).astype(BF16) for sc in scs]
    m_rb = [jnp.where(incl, sc[n2:, :n2], 0.0).astype(BF16) for sc in scs]
    m_rk = [jnp.where(incl, sc[n2:, n2:], 0.0).astype(BF16) for sc in scs]

    xs = [jnp.where(strict, sc[:n2, :n2], 0.0) for sc in scs]
    ts = [eye + x for x in xs]
    n = 2
    while n < c_len:
        xs = [_mm(x, x) for x in xs]
        ts = [t + _mm(t, x) for t, x in zip(ts, xs)]
        n *= 2

    lvs = [_mm(l, p["v_s"]) for l, p in zip(l_ak, pre)]
    tatv = [_mm(t, jnp.concatenate([p["a_s"], lv.astype(BF16)], axis=1)).astype(BF16)
            for t, p, lv in zip(ts, pre, lvs)]
    mtatv = [_mm(m, tv) for m, tv in zip(m_rb, tatv)]
    mkv = [_mm(m, p["v_s"]) for m, p in zip(m_rk, pre)]
    gz = [_mm_tn(tv, p["bg_s"]) for tv, p in zip(tatv, pre)]
    vk = [_mm_tn(p["v_s"], p["kg_s"]) for p in pre]
    for u in range(len(units)):
        ra_scr[u] = ra_scr[u] + fold(mtatv[u][:, :LANE])
        yv_scr[u] = fold(mtatv[u][:, LANE:] + mkv[u])
        m_scr[u] = gz[u][:LANE, :]
        zc_scr[u] = gz[u][LANE:, :] + vk[u]

    zts = [z_ref[pp] for pp in range(RW_PAIRS)]
    for c in range(n_chunks):
        for pp in range(RW_PAIRS):
            u = pp * n_chunks + c
            rows, lanes = window(pp, c)
            zt = zts[pp]
            y_ref[rows, lanes] = _mm_nt(ra_scr[u], zt) + yv_scr[u]
            zts[pp] = zt * ge_scr[u][0:1, :] + _mm(zt, m_scr[u]) + zc_scr[u]
    for pp in range(RW_PAIRS):
        z_ref[pp] = zts[pp]


def rwkv_scan(r, lw, k, v, a, b, *, batch, tb):
    m, hw = r.shape
    t_len = m // batch
    nt = t_len // tb
    nc = tb // RW_CHUNK
    lanes = RW_PAIRS * LANE
    nu = RW_PAIRS * nc
    assert t_len % tb == 0 and tb % RW_CHUNK == 0 and hw % lanes == 0
    spec = pl.BlockSpec((tb, lanes), lambda bi, p, t: (bi * nt + t, p))
    return pl.pallas_call(
        functools.partial(_rwkv_scan_kernel, n_chunks=nc),
        grid=(batch, hw // lanes, nt),
        in_specs=[spec] * 6,
        out_specs=spec,
        out_shape=jax.ShapeDtypeStruct((m, hw), F32),
        scratch_shapes=[
            pltpu.VMEM((RW_PAIRS, LANE, LANE), F32),
            pltpu.VMEM((nu, LANE, LANE), F32), pltpu.VMEM((nu, LANE, LANE), F32),
            pltpu.VMEM((nu, RW_CHUNK, LANE), F32), pltpu.VMEM((nu, RW_CHUNK, LANE), F32),
            pltpu.VMEM((nu, SUBLANE, LANE), F32),
        ],
        compiler_params=_cparams(("arbitrary", "arbitrary", "arbitrary")),
        name="rwkv_scan",
    )(r, lw, k, v, a, b)


def _even_out_kernel(ys_ref, wglu_ref, yr_ref, bon_ref, gt_ref, lw_ref, lb_ref, w_ref, x_ref,
                     g_ref, gn_ref, o_ref, hn_ref):
    ys = ys_ref[...]
    z = jnp.dot(ys.astype(BF16), wglu_ref[...], preferred_element_type=F32)
    a_s5 = (ys * _sigmoid(z)).astype(BF16)

    ones = _block_ones(LANE, RWKV_HEAD)
    y = yr_ref[...]
    inv_n = 1.0 / RWKV_HEAD
    mu = _seg_sum(y, ones) * inv_n
    yc = y - mu
    var = _seg_sum(yc * yc, ones) * inv_n
    yn = yc * lax.rsqrt(var + GN_EPS) * lw_ref[...] + lb_ref[...]
    a_rw = ((yn + bon_ref[...].astype(F32)) * gt_ref[...].astype(F32)).astype(BF16)

    half = a_s5.shape[1]
    out = (jnp.dot(a_s5, w_ref[:half, :], preferred_element_type=F32)
           + jnp.dot(a_rw, w_ref[half:, :], preferred_element_type=F32))
    x_new = x_ref[...] + _rms_rows(out, g_ref[...])
    o_ref[...] = x_new
    hn_ref[...] = _rms_rows(x_new, gn_ref[...]).astype(hn_ref.dtype)


def even_out_proj(y_s5, w_glu, y_rw, bonus, gate, lnx_w, lnx_b, w_out, layer, x, g, g_next, *, tm):
    m, hw = y_rw.shape
    s5w = y_s5.shape[1]
    d = w_out.shape[2]
    assert w_out.shape[1] == s5w + hw and m % tm == 0
    blk = lambda n: pl.BlockSpec((tm, n), lambda i: (i, 0))
    row = lambda n: pl.BlockSpec((1, n), lambda i: (0, 0))
    return pl.pallas_call(
        _even_out_kernel,
        grid=(m // tm,),
        in_specs=[
            blk(s5w), pl.BlockSpec((None, s5w, s5w), lambda i: (layer, 0, 0)),
            blk(hw), blk(hw), blk(hw), row(hw), row(hw),
            pl.BlockSpec((None, s5w + hw, d), lambda i: (layer, 0, 0)),
            blk(d), row(d), row(d),
        ],
        out_specs=[blk(d), blk(d)],
        out_shape=(jax.ShapeDtypeStruct((m, d), F32), jax.ShapeDtypeStruct((m, d), BF16)),
        compiler_params=_cparams(("arbitrary",)),
        name="even_out_proj",
    )(y_s5, w_glu, y_rw, bonus, gate, lnx_w.reshape(1, hw), lnx_b.reshape(1, hw), w_out, x,
      g.reshape(1, d), g_next.reshape(1, d))


def _lru_kernel(gate_ref, xb_ref, cw_ref, cb_ref, wr_ref, br_ref, wi_ref, bi_ref, lam_ref,
                o_ref, halo_ref, h_ref, a_scr, b_scr, *, tc):
    t_idx = pl.program_id(1)
    width = xb_ref.shape[1]
    nblk = tc // SUBLANE
    n_gate_blocks = width // LRU_BLOCK

    @pl.when(t_idx == 0)
    def _():
        halo_ref[...] = jnp.zeros_like(halo_ref)
        h_ref[...] = jnp.zeros_like(h_ref)

    xb = xb_ref[...].astype(F32)
    ext = jnp.concatenate([halo_ref[...], xb], axis=0)
    xc = xb * cw_ref[CONV_WIDTH - 1:CONV_WIDTH, :] + cb_ref[...]
    for d in range(1, CONV_WIDTH):
        sh = pltpu.roll(ext, d, axis=0)[SUBLANE:, :]
        xc = xc + sh * cw_ref[CONV_WIDTH - 1 - d:CONV_WIDTH - d, :]
    halo_ref[...] = xb[tc - SUBLANE:, :]

    sp = _softplus(-lam_ref[...])

    for n in range(n_gate_blocks):
        lanes = slice(n * LRU_BLOCK, (n + 1) * LRU_BLOCK)
        xn = xc[:, lanes]
        xnb = xn.astype(BF16)
        gr = jnp.dot(xnb, wr_ref[n].astype(BF16), preferred_element_type=F32) + br_ref[:, lanes]
        gi = jnp.dot(xnb, wi_ref[n].astype(BF16), preferred_element_type=F32) + bi_ref[:, lanes]
        log_a = -LRU_C * _sigmoid(gr) * sp[:, lanes]
        a = jnp.exp(log_a)
        mult = jnp.sqrt(-jnp.tanh(log_a) * (a * a + 1.0))
        a_scr[:, lanes] = a
        b_scr[:, lanes] = mult * _sigmoid(gi) * xn

    grp = LRU_SCAN_LANES
    row = lax.broadcasted_iota(jnp.int32, (SUBLANE, grp), 0)

    def blk(bidx, carry):
        rows = pl.ds(pl.multiple_of(bidx * SUBLANE, SUBLANE), SUBLANE)
        for g in range(width // grp):
            lanes = slice(g * grp, (g + 1) * grp)
            av = a_scr[rows, lanes]
            bv = b_scr[rows, lanes]
            for d in (1, 2, 4):
                keep = row >= d
                ash = jnp.where(keep, pltpu.roll(av, d, axis=0), 1.0)
                bsh = jnp.where(keep, pltpu.roll(bv, d, axis=0), 0.0)
                bv = bv + av * bsh
                av = av * ash
            hv = bv + av * h_ref[:, lanes]
            b_scr[rows, lanes] = hv
            h_ref[:, lanes] = jnp.broadcast_to(hv[SUBLANE - 1:SUBLANE, :], (SUBLANE, grp))
        return carry

    lax.fori_loop(0, nblk, blk, 0)
    o_ref[...] = (b_scr[...] * gate_ref[...].astype(F32)).astype(o_ref.dtype)


def lru_block(p_odd, conv_w, conv_b, w_r, b_r, w_i, b_i, lam, layer, *, batch, tc):
    m = p_odd.shape[0]
    width = lam.shape[0]
    t_len = m // batch
    nt = t_len // tc
    assert t_len % tc == 0 and p_odd.shape[1] == 2 * width
    row = pl.BlockSpec((1, width), lambda b, t: (0, 0))
    blkw = pl.BlockSpec((None,) + w_r.shape[1:], lambda b, t: (layer, 0, 0, 0))
    return pl.pallas_call(
        functools.partial(_lru_kernel, tc=tc),
        grid=(batch, nt),
        in_specs=[
            pl.BlockSpec((tc, width), lambda b, t: (b * nt + t, 0)),
            pl.BlockSpec((tc, width), lambda b, t: (b * nt + t, 1)),
            pl.BlockSpec((CONV_WIDTH, width), lambda b, t: (0, 0)),
            row, blkw, row, blkw, row, row,
        ],
        out_specs=pl.BlockSpec((tc, width), lambda b, t: (b * nt + t, 0)),
        out_shape=jax.ShapeDtypeStruct((m, width), BF16),
        scratch_shapes=[
            pltpu.VMEM((SUBLANE, width), F32), pltpu.VMEM((SUBLANE, width), F32),
            pltpu.VMEM((tc, width), F32), pltpu.VMEM((tc, width), F32),
        ],
        compiler_params=_cparams(("arbitrary", "arbitrary")),
        name="lru_block",
    )(p_odd, p_odd, conv_w, conv_b.reshape(1, width), w_r, b_r.reshape(1, width),
      w_i, b_i.reshape(1, width), lam.reshape(1, width))


def _s5_block_weights(bb_re, bb_im, c_re, c_im):
    c, g, p = bb_re.shape
    gpt = LANE // c
    tiles = g // gpt
    eye = jnp.eye(gpt, dtype=F32)

    def b_blk(bb):
        bb = bb.reshape(c, tiles, gpt, p)
        return jnp.einsum('cjgp,gh->jgchp', bb, eye).reshape(tiles, gpt * c, gpt * p)

    def c_blk(cm):
        cm = cm.reshape(tiles, gpt, c, p)
        return jnp.einsum('jgcp,gh->jhpgc', cm, eye).reshape(tiles, gpt * p, gpt * c)

    return b_blk(bb_re), b_blk(bb_im), c_blk(c_re), c_blk(c_im)


def _even_mixer(x, h, g_post, g_ffn, batch, idx, w_in, shift_mu, lam_re, lam_im, log_dt,
                b_re, b_im, c_re, c_im, d_skip, w_glu, w0, w2, a0, a2, g2, k_k, k_a, r_k,
                lnx_w, lnx_b, w_out):
    hw = w0.shape[0]
    s5w = d_skip.shape[0]
    n_main = s5w + 3 * hw
    p_main = in_proj(h, w_in, idx, n_main, tm=2048, tn=512, out_dtype=BF16, name="even_in_proj")
    lrw = w_in.shape[2] - n_main
    lr_pad = -(-lrw // LANE) * LANE
    w_lr = jnp.pad(w_in[idx, :, n_main:], ((0, 0), (0, lr_pad - lrw)))[None]
    lr = in_proj(h, w_lr, 0, lr_pad, tm=2048, tn=lr_pad, out_dtype=F32, name="even_lr_proj")

    a_re, a_im, bb_re, bb_im = s5_discretise(lam_re, lam_im, log_dt, b_re, b_im)
    bblk_re, bblk_im, cblk_re, cblk_im = _s5_block_weights(bb_re, bb_im, c_re, c_im)
    tiles = bblk_re.shape[0]
    wk, wb, wc, a8r, a8i = s5_tables(bblk_re, bblk_im, cblk_re, cblk_im,
                                     a_re.reshape(tiles, 1, -1), a_im.reshape(tiles, 1, -1),
                                     d_skip.reshape(tiles, 1, LANE))
    y_s5 = s5_mix(p_main, wk, wb, wc, a8r, a8i, batch=batch, tc=2048)

    r, lw, k, v, a, b, bonus, g = rwkv_prep(p_main, lr, shift_mu, w0, w2, a0, a2, g2,
                                            k_k, k_a, r_k.reshape(-1), batch=batch, tm=256)
    y_rw = rwkv_scan(r, lw, k, v, a, b, batch=batch, tb=512)

    return even_out_proj(y_s5, w_glu, y_rw, bonus, g, lnx_w, lnx_b, w_out, idx, x, g_post, g_ffn,
                         tm=256)


def _odd_mixer(x, h, g_post, g_ffn, batch, idx, w_in, conv_w, conv_b, w_r, b_r, w_i, b_i,
               lam, w_out):
    width = lam.shape[0]
    p_odd = in_proj(h, w_in, idx, 2 * width, tm=2048, tn=512, out_dtype=BF16,
                    n_gelu_tiles=width // 512, name="odd_in_proj")
    hg = lru_block(p_odd, conv_w, conv_b, w_r, b_r, w_i, b_i, lam, idx, batch=batch, tc=256)
    return out_proj_norm_residual([hg], w_out, idx, x, g_post, g_ffn, tm=512,
                                  name="odd_out_proj")


def _ffn(x, h, g_post, g_next, layer, w_gate, w_up, w_down):
    act, w_down_bf16 = swiglu_up(h, w_gate, w_up, w_down, layer, tm=2048, tn=512)
    return matmul_norm_residual(act, w_down_bf16[None], 0, x, g_post, g_next, tm=1024, tk=512,
                                n_chunk=512, name="ffn_down")


def kernel(x, ev_w_in, ev_shift_mu, s5_lam_re, s5_lam_im, s5_log_dt, s5_b_re, s5_b_im, s5_c_re, s5_c_im, s5_d, s5_w_glu, rw_w0, rw_w2, rw_a0, rw_a2, rw_g2, rw_k_k, rw_k_a, rw_r_k, rw_lnx_w, rw_lnx_b, ev_w_out, od_w_in, od_conv_w, od_conv_b, lru_w_r, lru_b_r, lru_w_i, lru_b_i, lru_lam, od_w_out, ffn_w_gate, ffn_w_up, ffn_w_down, norm_mix_pre, norm_mix_post, norm_ffn_pre, norm_ffn_post):
    batch, t_len, d = x.shape
    depth = ffn_w_gate.shape[0]
    xf = x.reshape(batch * t_len, d)
    s5_w_glu, ev_w_out, od_w_out = (w.astype(BF16) for w in (s5_w_glu, ev_w_out, od_w_out))
    hm = rmsnorm(xf, norm_mix_pre[0], tm=512)
    for layer in range(depth):
        i = layer // 2
        if layer % 2 == 0:
            xf, hf = _even_mixer(xf, hm, norm_mix_post[layer],
                                 norm_ffn_pre[layer], batch, i,
                                 ev_w_in, ev_shift_mu[i], s5_lam_re[i], s5_lam_im[i],
                                 s5_log_dt[i], s5_b_re[i], s5_b_im[i], s5_c_re[i], s5_c_im[i],
                                 s5_d[i], s5_w_glu, rw_w0[i], rw_w2[i], rw_a0[i], rw_a2[i],
                                 rw_g2[i], rw_k_k[i], rw_k_a[i], rw_r_k[i], rw_lnx_w[i],
                                 rw_lnx_b[i], ev_w_out)
        else:
            xf, hf = _odd_mixer(xf, hm, norm_mix_post[layer],
                                norm_ffn_pre[layer], batch, i,
                                od_w_in, od_conv_w[i], od_conv_b[i], lru_w_r, lru_b_r[i],
                                lru_w_i, lru_b_i[i], lru_lam[i], od_w_out)
        if layer + 1 < depth:
            xf, hm = _ffn(xf, hf, norm_ffn_post[layer], norm_mix_pre[layer + 1], layer,
                          ffn_w_gate, ffn_w_up, ffn_w_down)
        else:
            xf = _ffn(xf, hf, norm_ffn_post[layer], None, layer,
                      ffn_w_gate, ffn_w_up, ffn_w_down)
    return xf.reshape(batch, t_len, d)
```

```python
import functools
import math

import jax
import jax.numpy as jnp
from jax import lax
from jax.experimental import pallas as pl
from jax.experimental.pallas import tpu as pltpu

F32 = jnp.float32
BF16 = jnp.bfloat16

NORM_EPS = 1e-6
GN_EPS = 64e-5
LRU_C = 8.0
RWKV_HEAD = 64
CONV_WIDTH = 4
LRU_BLOCK = 256

LANE = 128
SUBLANE = 8
VMEM_LIMIT = 56 * 1024 * 1024

RW_CHUNK = 64
RW_PAIRS = 4
S5_BLOCK = 8
LRU_SCAN_LANES = 512
FFN_CAST_TILES = 2
MM_ROW_CHUNK = 512


def _cparams(sem):
    return pltpu.CompilerParams(dimension_semantics=sem, vmem_limit_bytes=VMEM_LIMIT)


def _gelu(x):
    c = math.sqrt(2.0 / math.pi)
    return 0.5 * x * (1.0 + jnp.tanh(c * (x + 0.044715 * (x * x * x))))


def _sigmoid(x):
    return 1.0 / (1.0 + jnp.exp(-x))


def _softplus(x):
    return jnp.maximum(x, 0.0) + jnp.log(1.0 + jnp.exp(-jnp.abs(x)))


def _rms_rows(y, g):
    ms = jnp.mean(y * y, axis=-1, keepdims=True)
    return y * lax.rsqrt(ms + NORM_EPS) * g


def _rmsnorm_kernel(x_ref, g_ref, o_ref):
    o_ref[...] = _rms_rows(x_ref[...], g_ref[...]).astype(o_ref.dtype)


def rmsnorm(x, g, *, tm):
    m, d = x.shape
    assert m % tm == 0
    return pl.pallas_call(
        _rmsnorm_kernel,
        grid=(m // tm,),
        in_specs=[pl.BlockSpec((tm, d), lambda i: (i, 0)), pl.BlockSpec((1, d), lambda i: (0, 0))],
        out_specs=pl.BlockSpec((tm, d), lambda i: (i, 0)),
        out_shape=jax.ShapeDtypeStruct((m, d), BF16),
        compiler_params=_cparams(("arbitrary",)),
        name="rmsnorm",
    )(x, g.reshape(1, d))


def _in_proj_kernel(h_ref, w_ref, o_ref, *, n_gelu_tiles):
    j = pl.program_id(1)
    w = w_ref[...].astype(BF16)

    def project(apply_gelu):
        for r0 in range(0, h_ref.shape[0], MM_ROW_CHUNK):
            rows = slice(r0, r0 + MM_ROW_CHUNK)
            acc = jnp.dot(h_ref[rows, :], w, preferred_element_type=F32)
            o_ref[rows, :] = (_gelu(acc) if apply_gelu else acc).astype(o_ref.dtype)

    if n_gelu_tiles == 0:
        project(False)
    else:
        @pl.when(j < n_gelu_tiles)
        def _():
            project(True)

        @pl.when(j >= n_gelu_tiles)
        def _():
            project(False)


def in_proj(h, w, layer, n_out, *, tm, tn, out_dtype, n_gelu_tiles=0, name):
    m, d = h.shape
    assert m % tm == 0 and n_out % tn == 0 and w.shape[1] == d and h.dtype == BF16
    return pl.pallas_call(
        functools.partial(_in_proj_kernel, n_gelu_tiles=n_gelu_tiles),
        grid=(m // tm, n_out // tn),
        in_specs=[
            pl.BlockSpec((tm, d), lambda i, j: (i, 0)),
            pl.BlockSpec((None, d, tn), lambda i, j: (layer, 0, j)),
        ],
        out_specs=pl.BlockSpec((tm, tn), lambda i, j: (i, j)),
        out_shape=jax.ShapeDtypeStruct((m, n_out), out_dtype),
        compiler_params=_cparams(("arbitrary", "arbitrary")),
        name=name,
    )(h, w)


def _swiglu_up_kernel(h_ref, wg_ref, wu_ref, wd_ref, o_ref, wdb_ref):
    wg = wg_ref[...].astype(BF16)
    wu = wu_ref[...].astype(BF16)
    for r0 in range(0, h_ref.shape[0], MM_ROW_CHUNK):
        rows = slice(r0, r0 + MM_ROW_CHUNK)
        h = h_ref[rows, :]
        gate = jnp.dot(h, wg, preferred_element_type=F32)
        up = jnp.dot(h, wu, preferred_element_type=F32)
        o_ref[rows, :] = (gate * _sigmoid(gate) * up).astype(o_ref.dtype)

    @pl.when(pl.program_id(0) < FFN_CAST_TILES)
    def _():
        wdb_ref[...] = wd_ref[...].astype(wdb_ref.dtype)


def swiglu_up(h, w_gate, w_up, w_down, layer, *, tm, tn):
    m, d = h.shape
    n = w_gate.shape[2]
    nj = n // tn
    rows = tn // FFN_CAST_TILES
    last = n // rows - 1
    assert m % tm == 0 and n % tn == 0 and tn % FFN_CAST_TILES == 0 and h.dtype == BF16
    assert m // tm >= FFN_CAST_TILES and w_down.shape[1:] == (n, d)

    def cast_block(i, j):
        return jnp.where(i < FFN_CAST_TILES, FFN_CAST_TILES * j + i, last)

    return pl.pallas_call(
        _swiglu_up_kernel,
        grid=(m // tm, nj),
        in_specs=[
            pl.BlockSpec((tm, d), lambda i, j: (i, 0)),
            pl.BlockSpec((None, d, tn), lambda i, j: (layer, 0, j)),
            pl.BlockSpec((None, d, tn), lambda i, j: (layer, 0, j)),
            pl.BlockSpec((None, rows, d), lambda i, j: (layer, cast_block(i, j), 0)),
        ],
        out_specs=[pl.BlockSpec((tm, tn), lambda i, j: (i, j)),
                   pl.BlockSpec((rows, d), lambda i, j: (cast_block(i, j), 0))],
        out_shape=(jax.ShapeDtypeStruct((m, n), BF16), jax.ShapeDtypeStruct((n, d), BF16)),
        compiler_params=_cparams(("arbitrary", "arbitrary")),
        name="ffn_up",
    )(h, w_gate, w_up, w_down)


def _mm_norm_res_kernel(a_ref, w_ref, x_ref, g_ref, *rest, emit_next, n_chunk):
    if emit_next:
        gn_ref, o_ref, hn_ref = rest
    else:
        (o_ref,) = rest
    k = pl.program_id(1)
    d = o_ref.shape[1]

    def accumulate(first):
        a = a_ref[...]
        for n in range(0, d, n_chunk):
            cols = slice(n, n + n_chunk)
            part = jnp.dot(a, w_ref[:, cols], preferred_element_type=F32)
            if first:
                o_ref[:, cols] = part
            else:
                o_ref[:, cols] += part

    @pl.when(k == 0)
    def _():
        accumulate(True)

    @pl.when(k > 0)
    def _():
        accumulate(False)

    @pl.when(k == pl.num_programs(1) - 1)
    def _():
        x_new = x_ref[...] + _rms_rows(o_ref[...], g_ref[...])
        o_ref[...] = x_new
        if emit_next:
            hn_ref[...] = _rms_rows(x_new, gn_ref[...]).astype(hn_ref.dtype)


def matmul_norm_residual(a, w, layer, x, g, g_next, *, tm, tk, n_chunk, name):
    m, kdim = a.shape
    d = w.shape[2]
    assert m % tm == 0 and kdim % tk == 0 and w.shape[1] == kdim and d % n_chunk == 0
    emit_next = g_next is not None
    row = pl.BlockSpec((1, d), lambda i, k: (0, 0))
    blk = pl.BlockSpec((tm, d), lambda i, k: (i, 0))
    f32_out = jax.ShapeDtypeStruct((m, d), F32)
    operands = [a, w, x, g.reshape(1, d)] + ([g_next.reshape(1, d)] if emit_next else [])
    return pl.pallas_call(
        functools.partial(_mm_norm_res_kernel, emit_next=emit_next, n_chunk=n_chunk),
        grid=(m // tm, kdim // tk),
        in_specs=[
            pl.BlockSpec((tm, tk), lambda i, k: (i, k)),
            pl.BlockSpec((None, tk, d), lambda i, k: (layer, k, 0)),
            blk, row,
        ] + ([row] if emit_next else []),
        out_specs=[blk, blk] if emit_next else blk,
        out_shape=(f32_out, jax.ShapeDtypeStruct((m, d), BF16)) if emit_next else f32_out,
        compiler_params=_cparams(("arbitrary", "arbitrary")),
        name=name,
    )(*operands)


def _out_proj_kernel(*refs, n_lhs):
    a_refs = refs[:n_lhs]
    w_ref, x_ref, g_ref, gn_ref, o_ref, hn_ref = refs[n_lhs:]
    y = None
    k0 = 0
    for a_ref in a_refs:
        kw = a_ref.shape[1]
        part = jnp.dot(a_ref[...], w_ref[k0:k0 + kw, :], preferred_element_type=F32)
        y = part if y is None else y + part
        k0 += kw
    x_new = x_ref[...] + _rms_rows(y, g_ref[...])
    o_ref[...] = x_new
    hn_ref[...] = _rms_rows(x_new, gn_ref[...]).astype(hn_ref.dtype)


def out_proj_norm_residual(lhs, w, layer, x, g, g_next, *, tm, name):
    m = x.shape[0]
    kdim, d = w.shape[1], w.shape[2]
    assert sum(a.shape[1] for a in lhs) == kdim and m % tm == 0
    row = pl.BlockSpec((1, d), lambda i: (0, 0))
    return pl.pallas_call(
        functools.partial(_out_proj_kernel, n_lhs=len(lhs)),
        grid=(m // tm,),
        in_specs=[pl.BlockSpec((tm, a.shape[1]), lambda i: (i, 0)) for a in lhs] + [
            pl.BlockSpec((None, kdim, d), lambda i: (layer, 0, 0)),
            pl.BlockSpec((tm, d), lambda i: (i, 0)),
            row, row,
        ],
        out_specs=[pl.BlockSpec((tm, d), lambda i: (i, 0))] * 2,
        out_shape=(jax.ShapeDtypeStruct((m, d), F32), jax.ShapeDtypeStruct((m, d), BF16)),
        compiler_params=_cparams(("arbitrary",)),
        name=name,
    )(*lhs, w, x, g.reshape(1, d), g_next.reshape(1, d))


def _block_ones(n, seg):
    r = lax.broadcasted_iota(jnp.int32, (n, n), 0) // seg
    c = lax.broadcasted_iota(jnp.int32, (n, n), 1) // seg
    return (r == c).astype(BF16)


def _seg_sum(x, ones):
    outs = []
    for j in range(x.shape[1] // LANE):
        xj = x[:, j * LANE:(j + 1) * LANE]
        hi = xj.astype(BF16)
        lo = (xj - hi.astype(F32)).astype(BF16)
        s = (jnp.dot(hi, ones, preferred_element_type=F32)
             + jnp.dot(lo, ones, preferred_element_type=F32))
        outs.append(s)
    return jnp.concatenate(outs, axis=1) if len(outs) > 1 else outs[0]


def _s5_disc_kernel(lr_ref, li_ref, ldt_ref, bre_ref, bim_ref,
                    are_ref, aim_ref, bbre_ref, bbim_ref):
    lr = lr_ref[...]
    li = li_ref[...]
    dt = jnp.exp(ldt_ref[...])
    mag = jnp.exp(lr * dt)
    a_re = mag * jnp.cos(li * dt)
    a_im = mag * jnp.sin(li * dt)
    den = lr * lr + li * li
    nr = a_re - 1.0
    ni = a_im
    gam_re = (nr * lr + ni * li) / den
    gam_im = (ni * lr - nr * li) / den
    are_ref[...] = a_re
    aim_ref[...] = a_im
    for c in range(bre_ref.shape[0]):
        br = bre_ref[c]
        bi = bim_ref[c]
        bbre_ref[c] = gam_re * br - gam_im * bi
        bbim_ref[c] = gam_re * bi + gam_im * br


def s5_discretise(lam_re, lam_im, log_dt, b_re, b_im):
    g, p, c = b_re.shape
    ldt = jnp.broadcast_to(log_dt[:, None], (g, p))
    b_re_t = jnp.transpose(b_re, (2, 0, 1))
    b_im_t = jnp.transpose(b_im, (2, 0, 1))
    gp = jax.ShapeDtypeStruct((g, p), F32)
    cgp = jax.ShapeDtypeStruct((c, g, p), F32)
    return pl.pallas_call(
        _s5_disc_kernel,
        out_shape=(gp, gp, cgp, cgp),
        name="s5_discretise",
    )(lam_re, lam_im, ldt, b_re_t, b_im_t)


def _cmul(ar, ai, br, bi):
    return ar * br - ai * bi, ar * bi + ai * br


def _s5_tables_kernel(bre_ref, bim_ref, cre_ref, cim_ref, ar_ref, ai_ref, acr_ref, aci_ref, d_ref,
                      wk_ref, wb_ref, wc_ref, a8r_ref, a8i_ref):
    bre, bim = bre_ref[...], bim_ref[...]
    cre, cim = cre_ref[...], cim_ref[...]
    ar, ai = ar_ref[...], ai_ref[...]
    acr, aci = acr_ref[...], aci_ref[...]
    ml = bre.shape[1]
    eye = (lax.broadcasted_iota(jnp.int32, (LANE, LANE), 0)
           == lax.broadcasted_iota(jnp.int32, (LANE, LANE), 1)).astype(F32)

    pr, pi = [jnp.ones_like(ar)], [jnp.zeros_like(ai)]
    for _ in range(S5_BLOCK):
        nr, ni = _cmul(pr[-1], pi[-1], ar, ai)
        pr.append(nr)
        pi.append(ni)
    qr, qi = [acr], [aci]
    for _ in range(S5_BLOCK - 1):
        nr, ni = _cmul(qr[-1], qi[-1], acr, aci)
        qr.append(nr)
        qi.append(ni)

    hi = lax.Precision.HIGHEST
    for tau in range(S5_BLOCK):
        mre = bre * pr[tau] - bim * pi[tau]
        mim = bre * pi[tau] + bim * pr[tau]
        k = (jnp.dot(mre, cre, preferred_element_type=F32, precision=hi)
             - jnp.dot(mim, cim, preferred_element_type=F32, precision=hi))
        if tau == 0:
            k = k + eye * d_ref[...]
        wk_ref[tau * LANE:(tau + 1) * LANE, :] = k.astype(wk_ref.dtype)
        r = S5_BLOCK - 1 - tau
        wb_ref[r * LANE:(r + 1) * LANE, :ml] = mre.astype(wb_ref.dtype)
        wb_ref[r * LANE:(r + 1) * LANE, ml:] = mim.astype(wb_ref.dtype)

    for r in range(S5_BLOCK):
        dr, di = qr[r], qi[r]
        wc_ref[:ml, r * LANE:(r + 1) * LANE] = (cre * dr - cim * di).astype(wc_ref.dtype)
        wc_ref[ml:, r * LANE:(r + 1) * LANE] = (-(cre * di + cim * dr)).astype(wc_ref.dtype)

    a8r_ref[...] = pr[S5_BLOCK]
    a8i_ref[...] = pi[S5_BLOCK]


def s5_tables(bblk_re, bblk_im, cblk_re, cblk_im, a_re, a_im, d_skip):
    tiles, _, ml = bblk_re.shape
    a_col_re = jnp.broadcast_to(jnp.swapaxes(a_re, 1, 2), (tiles, ml, LANE))
    a_col_im = jnp.broadcast_to(jnp.swapaxes(a_im, 1, 2), (tiles, ml, LANE))
    kdim = S5_BLOCK * LANE
    t3 = lambda a, b: pl.BlockSpec((None, a, b), lambda j: (j, 0, 0))
    return pl.pallas_call(
        _s5_tables_kernel,
        grid=(tiles,),
        in_specs=[t3(LANE, ml), t3(LANE, ml), t3(ml, LANE), t3(ml, LANE), t3(1, ml), t3(1, ml),
                  t3(ml, LANE), t3(ml, LANE), t3(1, LANE)],
        out_specs=[t3(kdim, LANE), t3(kdim, 2 * ml), t3(2 * ml, kdim), t3(1, ml), t3(1, ml)],
        out_shape=(jax.ShapeDtypeStruct((tiles, kdim, LANE), BF16),
                   jax.ShapeDtypeStruct((tiles, kdim, 2 * ml), BF16),
                   jax.ShapeDtypeStruct((tiles, 2 * ml, kdim), BF16),
                   jax.ShapeDtypeStruct((tiles, 1, ml), F32),
                   jax.ShapeDtypeStruct((tiles, 1, ml), F32)),
        compiler_params=_cparams(("arbitrary",)),
        name="s5_tables",
    )(bblk_re, bblk_im, cblk_re, cblk_im, a_re, a_im, a_col_re, a_col_im, d_skip)


def _s5_mix_kernel(u_ref, wk_ref, wb_ref, wc_ref, a8r_ref, a8i_ref, o_ref,
                   car_ref, cai_ref, tab_ref, uf_ref, xr_ref, xi_ref, y_ref, *, tc):
    t_idx = pl.program_id(2)
    ml = a8r_ref.shape[1]
    nb = tc // S5_BLOCK
    row = lax.broadcasted_iota(jnp.int32, (SUBLANE, ml), 0)

    @pl.when(t_idx == 0)
    def _():
        car_ref[...] = jnp.zeros_like(car_ref)
        cai_ref[...] = jnp.zeros_like(cai_ref)
        a1r = jnp.broadcast_to(a8r_ref[...], (SUBLANE, ml))
        a1i = jnp.broadcast_to(a8i_ref[...], (SUBLANE, ml))
        a2r, a2i = _cmul(a1r, a1i, a1r, a1i)
        a4r, a4i = _cmul(a2r, a2i, a2r, a2i)
        pr, pi = a1r, a1i
        cr, ci = a1r, a1i
        for r in range(1, SUBLANE):
            cr, ci = _cmul(cr, ci, a1r, a1i)
            pr = jnp.where(row >= r, cr, pr)
            pi = jnp.where(row >= r, ci, pi)
        for lvl, (d, xr, xi) in enumerate(((1, a1r, a1i), (2, a2r, a2i), (4, a4r, a4i))):
            tab_ref[2 * lvl] = jnp.where(row >= d, xr, 0.0)
            tab_ref[2 * lvl + 1] = jnp.where(row >= d, xi, 0.0)
        tab_ref[6] = pr
        tab_ref[7] = pi

    uf = u_ref[...].astype(F32)
    uf_ref[...] = uf

    pos = lax.broadcasted_iota(jnp.int32, (tc, LANE), 0) % S5_BLOCK
    shifted = [uf.astype(BF16)]
    for tau in range(1, S5_BLOCK):
        shifted.append(jnp.where(pos >= tau, pltpu.roll(uf, tau, axis=0), 0.0).astype(BF16))
    y_ref[...] = jnp.dot(jnp.concatenate(shifted, axis=1), wk_ref[...], preferred_element_type=F32)

    strided = [uf_ref[pl.ds(r, nb, stride=S5_BLOCK), :].astype(BF16) for r in range(S5_BLOCK)]
    x = jnp.dot(jnp.concatenate(strided, axis=1), wb_ref[...], preferred_element_type=F32)
    xr_ref[...] = x[:, :ml]
    xi_ref[...] = x[:, ml:]

    def blk(b, carry):
        rows = pl.ds(pl.multiple_of(b * SUBLANE, SUBLANE), SUBLANE)
        sr = xr_ref[rows, :]
        si = xi_ref[rows, :]
        for lvl, d in enumerate((1, 2, 4)):
            kr = tab_ref[2 * lvl]
            ki = tab_ref[2 * lvl + 1]
            rr = pltpu.roll(sr, d, axis=0)
            ri = pltpu.roll(si, d, axis=0)
            sr, si = sr + kr * rr - ki * ri, si + kr * ri + ki * rr
        pr = tab_ref[6]
        pi = tab_ref[7]
        cbr = car_ref[...]
        cbi = cai_ref[...]
        sr = sr + pr * cbr - pi * cbi
        si = si + pr * cbi + pi * cbr
        xr_ref[rows, :] = jnp.where(row == 0, cbr, pltpu.roll(sr, 1, axis=0))
        xi_ref[rows, :] = jnp.where(row == 0, cbi, pltpu.roll(si, 1, axis=0))
        car_ref[...] = jnp.broadcast_to(sr[SUBLANE - 1:SUBLANE, :], (SUBLANE, ml))
        cai_ref[...] = jnp.broadcast_to(si[SUBLANE - 1:SUBLANE, :], (SUBLANE, ml))
        return carry

    lax.fori_loop(0, nb // SUBLANE, blk, 0)

    s_in = jnp.concatenate([xr_ref[...], xi_ref[...]], axis=1).astype(BF16)
    z = jnp.dot(s_in, wc_ref[...], preferred_element_type=F32)
    for r in range(S5_BLOCK):
        rows = pl.ds(r, nb, stride=S5_BLOCK)
        y_ref[rows, :] = y_ref[rows, :] + z[:, r * LANE:(r + 1) * LANE]
    o_ref[...] = _gelu(y_ref[...])


def s5_mix(p_main, wk, wb, wc, a8r, a8i, *, batch, tc):
    m = p_main.shape[0]
    t_len = m // batch
    tiles, kdim, _ = wk.shape
    ml = a8r.shape[2]
    nt = t_len // tc
    nb = tc // S5_BLOCK
    assert t_len % tc == 0 and nb % SUBLANE == 0 and p_main.dtype == BF16
    t3 = lambda a, b: pl.BlockSpec((None, a, b), lambda bi, j, t: (j, 0, 0))
    io = pl.BlockSpec((tc, LANE), lambda bi, j, t: (bi * nt + t, j))
    return pl.pallas_call(
        functools.partial(_s5_mix_kernel, tc=tc),
        grid=(batch, tiles, nt),
        in_specs=[io, t3(kdim, LANE), t3(kdim, 2 * ml), t3(2 * ml, kdim), t3(1, ml), t3(1, ml)],
        out_specs=io,
        out_shape=jax.ShapeDtypeStruct((m, tiles * LANE), F32),
        scratch_shapes=[
            pltpu.VMEM((SUBLANE, ml), F32), pltpu.VMEM((SUBLANE, ml), F32),
            pltpu.VMEM((8, SUBLANE, ml), F32),
            pltpu.VMEM((tc, LANE), F32),
            pltpu.VMEM((nb, ml), F32), pltpu.VMEM((nb, ml), F32),
            pltpu.VMEM((tc, LANE), F32),
        ],
        compiler_params=_cparams(("arbitrary", "arbitrary", "arbitrary")),
        name="s5_mix",
    )(p_main, wk, wb, wc, a8r, a8i)


def _shift_mix(z, prev_row, mu):
    row = lax.broadcasted_iota(jnp.int32, z.shape, 0)
    zs = jnp.where(row == 0, jnp.broadcast_to(prev_row, z.shape), pltpu.roll(z, 1, axis=0))
    return z + (zs - z) * mu


def _rwkv_prep_kernel(r_ref, k_ref, v_ref, lr_ref, rp_ref, kp_ref, vp_ref, lrp_ref,
                      mur_ref, muk_ref, muv_ref, mulr_ref,
                      w0_ref, w2_ref, a0_ref, a2_ref, g2_ref, kk_ref, ka_ref, rk_ref,
                      ro_ref, lw_ref, ko_ref, vo_ref, ao_ref, bo_ref, bon_ref, go_ref,
                      *, tiles_per_batch):
    i = pl.program_id(0)
    first = (i % tiles_per_batch) == 0

    def prev(ref):
        last = ref.shape[0] - 1
        return jnp.where(first, 0.0, ref[last:last + 1, :].astype(F32))

    r = _shift_mix(r_ref[...].astype(F32), prev(rp_ref), mur_ref[...])
    k = _shift_mix(k_ref[...].astype(F32), prev(kp_ref), muk_ref[...])
    v = _shift_mix(v_ref[...].astype(F32), prev(vp_ref), muv_ref[...])
    lr = _shift_mix(lr_ref[...], prev(lrp_ref), mulr_ref[...])

    wl = w0_ref[...] + jnp.dot(jnp.tanh(lr).astype(BF16), w2_ref[...].astype(BF16),
                               preferred_element_type=F32)
    w = -_softplus(-wl) - 0.5
    lw_ref[...] = -jnp.exp(w)
    a = _sigmoid(a0_ref[...] + jnp.dot(lr.astype(BF16), a2_ref[...].astype(BF16),
                                       preferred_element_type=F32))
    go_ref[...] = jnp.dot(_sigmoid(lr).astype(BF16), g2_ref[...].astype(BF16),
                          preferred_element_type=F32).astype(go_ref.dtype)

    ones = _block_ones(LANE, RWKV_HEAD)
    kk = k * kk_ref[...]
    kk = kk * lax.rsqrt(jnp.maximum(_seg_sum(kk * kk, ones), 1e-24))
    k = k * (1.0 + (a - 1.0) * ka_ref[...])
    ro_ref[...] = r.astype(ro_ref.dtype)
    ko_ref[...] = k.astype(ko_ref.dtype)
    vo_ref[...] = v.astype(vo_ref.dtype)
    ao_ref[...] = (-kk).astype(ao_ref.dtype)
    bo_ref[...] = (kk * a).astype(bo_ref.dtype)
    bon_ref[...] = (_seg_sum(r * k * rk_ref[...], ones) * v).astype(bon_ref.dtype)


def rwkv_prep(p_main, lr, mu, w0, w2, a0, a2, g2, k_k, k_a, r_k, *, batch, tm):
    m = p_main.shape[0]
    t_len = m // batch
    hw = w0.shape[0]
    lrw = lr.shape[1]
    w_rank, a_rank, g_rank = w2.shape[0], a2.shape[0], g2.shape[0]
    nblk8 = tm // SUBLANE
    halo = 2 * SUBLANE
    nblk16 = tm // halo
    assert t_len % tm == 0 and p_main.shape[1] == 4 * hw and p_main.dtype == BF16

    def cur(c):
        return pl.BlockSpec((tm, hw), lambda i: (i, c))

    def prv(c):
        return pl.BlockSpec((halo, hw), lambda i: (jnp.maximum(i * nblk16 - 1, 0), c))

    row = lambda n: pl.BlockSpec((1, n), lambda i: (0, 0))
    mat = lambda a: pl.BlockSpec(a.shape, lambda i: (0, 0))
    used = w_rank + a_rank + g_rank
    mu_lr = jnp.pad(mu[3 * hw:], (0, lrw - used))
    w2 = jnp.pad(w2, ((0, lrw - w_rank), (0, 0)))
    a2 = jnp.pad(a2, ((w_rank, lrw - w_rank - a_rank), (0, 0)))
    g2 = jnp.pad(g2, ((w_rank + a_rank, lrw - used), (0, 0)))
    out = jax.ShapeDtypeStruct((m, hw), BF16)
    out_lw = jax.ShapeDtypeStruct((m, hw), F32)
    return pl.pallas_call(
        functools.partial(_rwkv_prep_kernel, tiles_per_batch=t_len // tm),
        grid=(m // tm,),
        in_specs=[
            cur(1), cur(2), cur(3), pl.BlockSpec((tm, lrw), lambda i: (i, 0)),
            prv(1), prv(2), prv(3),
            pl.BlockSpec((SUBLANE, lrw), lambda i: (jnp.maximum(i * nblk8 - 1, 0), 0)),
            row(hw), row(hw), row(hw), row(lrw),
            row(hw), mat(w2), row(hw), mat(a2), mat(g2), row(hw), row(hw), row(hw),
        ],
        out_specs=[pl.BlockSpec((tm, hw), lambda i: (i, 0))] * 8,
        out_shape=(out, out_lw) + (out,) * 6,
        compiler_params=_cparams(("arbitrary",)),
        name="rwkv_prep",
    )(p_main, p_main, p_main, lr, p_main, p_main, p_main, lr,
      mu[:hw].reshape(1, hw), mu[hw:2 * hw].reshape(1, hw), mu[2 * hw:3 * hw].reshape(1, hw),
      mu_lr.reshape(1, lrw),
      w0.reshape(1, hw), w2, a0.reshape(1, hw), a2, g2, k_k.reshape(1, hw),
      k_a.reshape(1, hw), r_k.reshape(1, hw))


def _mm(a, b):
    return jnp.dot(a.astype(BF16), b.astype(BF16), preferred_element_type=F32)


def _mm_nt(a, b):
    return lax.dot_general(a.astype(BF16), b.astype(BF16), (((1,), (1,)), ((), ())),
                           preferred_element_type=F32)


def _mm_tn(a, b):
    return lax.dot_general(a.astype(BF16), b.astype(BF16), (((0,), (0,)), ((), ())),
                           preferred_element_type=F32)


def _split3(x):
    hi = x.astype(BF16)
    r1 = x - hi.astype(F32)
    mid = r1.astype(BF16)
    lo = (r1 - mid.astype(F32)).astype(BF16)
    return hi, mid, lo


def _rwkv_scan_kernel(r_ref, lw_ref, k_ref, v_ref, a_ref, b_ref, y_ref,
                      z_ref, m_scr, zc_scr, ra_scr, yv_scr, ge_scr, *, n_chunks):
    c_len = RW_CHUNK
    n2 = 2 * c_len

    @pl.when(pl.program_id(2) == 0)
    def _():
        z_ref[...] = jnp.zeros_like(z_ref)

    lane = lax.broadcasted_iota(jnp.int32, (c_len, LANE), 1)
    head_a = lane < RWKV_HEAD
    ri = lax.broadcasted_iota(jnp.int32, (n2, n2), 0)
    ci = lax.broadcasted_iota(jnp.int32, (n2, n2), 1)
    same = (ri // c_len) == (ci // c_len)
    strict = same & ((ci % c_len) < (ri % c_len))
    incl = same & ((ci % c_len) <= (ri % c_len))
    eye = (ri == ci).astype(F32)
    rc = lax.broadcasted_iota(jnp.int32, (c_len, c_len), 0)
    cc = lax.broadcasted_iota(jnp.int32, (c_len, c_len), 1)
    tri = (cc <= rc).astype(BF16)

    def stack(x):
        return jnp.concatenate([jnp.where(head_a, x, 0.0), jnp.where(head_a, 0.0, x)], axis=0)

    def fold(x):
        return x[:c_len, :] + x[c_len:, :]

    units = [(pp, c) for pp in range(RW_PAIRS) for c in range(n_chunks)]

    def window(pp, c):
        return slice(c * c_len, (c + 1) * c_len), slice(pp * LANE, (pp + 1) * LANE)

    pre = []
    for u, (pp, c) in enumerate(units):
        rows, lanes = window(pp, c)
        lw = lw_ref[rows, lanes]
        r, k, v, a, b = (ref[rows, lanes].astype(F32) for ref in (r_ref, k_ref, v_ref, a_ref, b_ref))
        cl = sum(jnp.dot(tri, part, preferred_element_type=F32) for part in _split3(lw))
        cl_end = cl[c_len - 1:c_len, :]
        g_inv = jnp.exp(-cl)
        g_tail = jnp.exp(cl_end - cl)
        r_t = r * jnp.exp(cl)
        ra_scr[u] = r_t
        ge_scr[u] = jnp.broadcast_to(jnp.exp(cl_end), (SUBLANE, LANE))
        pre.append(dict(
            a_s=stack(a * jnp.exp(cl - lw)).astype(BF16), r_s=stack(r_t).astype(BF16),
            b_s=stack(b * g_inv).astype(BF16), k_s=stack(k * g_inv).astype(BF16),
            v_s=stack(v).astype(BF16), bg_s=stack(b * g_tail).astype(BF16),
            kg_s=stack(k * g_tail).astype(BF16)))

    scs = [_mm_nt(jnp.concatenate([p["a_s"], p["r_s"]], axis=0),
                  jnp.concatenate([p["b_s"], p["k_s"]], axis=0)) for p in pre]
    l_ak = [jnp.where(strict, sc[:n2, n2:], 0.0).astype(BF16) for sc in scs]
    m_rb = [jnp.where(incl, sc[n2:, :n2], 0.0).astype(BF16) for sc in scs]
    m_rk = [jnp.where(incl, sc[n2:, n2:], 0.0).astype(BF16) for sc in scs]

    xs = [jnp.where(strict, sc[:n2, :n2], 0.0) for sc in scs]
    ts = [eye + x for x in xs]
    n = 2
    while n < c_len:
        xs = [_mm(x, x) for x in xs]
        ts = [t + _mm(t, x) for t, x in zip(ts, xs)]
        n *= 2

    lvs = [_mm(l, p["v_s"]) for l, p in zip(l_ak, pre)]
    tatv = [_mm(t, jnp.concatenate([p["a_s"], lv.astype(BF16)], axis=1)).astype(BF16)
            for t, p, lv in zip(ts, pre, lvs)]
    mtatv = [_mm(m, tv) for m, tv in zip(m_rb, tatv)]
    mkv = [_mm(m, p["v_s"]) for m, p in zip(m_rk, pre)]
    gz = [_mm_tn(tv, p["bg_s"]) for tv, p in zip(tatv, pre)]
    vk = [_mm_tn(p["v_s"], p["kg_s"]) for p in pre]
    for u in range(len(units)):
        ra_scr[u] = ra_scr[u] + fold(mtatv[u][:, :LANE])
        yv_scr[u] = fold(mtatv[u][:, LANE:] + mkv[u])
        m_scr[u] = gz[u][:LANE, :]
        zc_scr[u] = gz[u][LANE:, :] + vk[u]

    zts = [z_ref[pp] for pp in range(RW_PAIRS)]
    for c in range(n_chunks):
        for pp in range(RW_PAIRS):
            u = pp * n_chunks + c
            rows, lanes = window(pp, c)
            zt = zts[pp]
            y_ref[rows, lanes] = _mm_nt(ra_scr[u], zt) + yv_scr[u]
            zts[pp] = zt * ge_scr[u][0:1, :] + _mm(zt, m_scr[u]) + zc_scr[u]
    for pp in range(RW_PAIRS):
        z_ref[pp] = zts[pp]


def rwkv_scan(r, lw, k, v, a, b, *, batch, tb):
    m, hw = r.shape
    t_len = m // batch
    nt = t_len // tb
    nc = tb // RW_CHUNK
    lanes = RW_PAIRS * LANE
    nu = RW_PAIRS * nc
    assert t_len % tb == 0 and tb % RW_CHUNK == 0 and hw % lanes == 0
    spec = pl.BlockSpec((tb, lanes), lambda bi, p, t: (bi * nt + t, p))
    return pl.pallas_call(
        functools.partial(_rwkv_scan_kernel, n_chunks=nc),
        grid=(batch, hw // lanes, nt),
        in_specs=[spec] * 6,
        out_specs=spec,
        out_shape=jax.ShapeDtypeStruct((m, hw), F32),
        scratch_shapes=[
            pltpu.VMEM((RW_PAIRS, LANE, LANE), F32),
            pltpu.VMEM((nu, LANE, LANE), F32), pltpu.VMEM((nu, LANE, LANE), F32),
            pltpu.VMEM((nu, RW_CHUNK, LANE), F32), pltpu.VMEM((nu, RW_CHUNK, LANE), F32),
            pltpu.VMEM((nu, SUBLANE, LANE), F32),
        ],
        compiler_params=_cparams(("arbitrary", "arbitrary", "arbitrary")),
        name="rwkv_scan",
    )(r, lw, k, v, a, b)


def _even_out_kernel(ys_ref, wglu_ref, yr_ref, bon_ref, gt_ref, lw_ref, lb_ref, w_ref, x_ref,
                     g_ref, gn_ref, o_ref, hn_ref):
    ys = ys_ref[...]
    z = jnp.dot(ys.astype(BF16), wglu_ref[...], preferred_element_type=F32)
    a_s5 = (ys * _sigmoid(z)).astype(BF16)

    ones = _block_ones(LANE, RWKV_HEAD)
    y = yr_ref[...]
    inv_n = 1.0 / RWKV_HEAD
    mu = _seg_sum(y, ones) * inv_n
    yc = y - mu
    var = _seg_sum(yc * yc, ones) * inv_n
    yn = yc * lax.rsqrt(var + GN_EPS) * lw_ref[...] + lb_ref[...]
    a_rw = ((yn + bon_ref[...].astype(F32)) * gt_ref[...].astype(F32)).astype(BF16)

    half = a_s5.shape[1]
    out = (jnp.dot(a_s5, w_ref[:half, :], preferred_element_type=F32)
           + jnp.dot(a_rw, w_ref[half:, :], preferred_element_type=F32))
    x_new = x_ref[...] + _rms_rows(out, g_ref[...])
    o_ref[...] = x_new
    hn_ref[...] = _rms_rows(x_new, gn_ref[...]).astype(hn_ref.dtype)


def even_out_proj(y_s5, w_glu, y_rw, bonus, gate, lnx_w, lnx_b, w_out, layer, x, g, g_next, *, tm):
    m, hw = y_rw.shape
    s5w = y_s5.shape[1]
    d = w_out.shape[2]
    assert w_out.shape[1] == s5w + hw and m % tm == 0
    blk = lambda n: pl.BlockSpec((tm, n), lambda i: (i, 0))
    row = lambda n: pl.BlockSpec((1, n), lambda i: (0, 0))
    return pl.pallas_call(
        _even_out_kernel,
        grid=(m // tm,),
        in_specs=[
            blk(s5w), pl.BlockSpec((None, s5w, s5w), lambda i: (layer, 0, 0)),
            blk(hw), blk(hw), blk(hw), row(hw), row(hw),
            pl.BlockSpec((None, s5w + hw, d), lambda i: (layer, 0, 0)),
            blk(d), row(d), row(d),
        ],
        out_specs=[blk(d), blk(d)],
        out_shape=(jax.ShapeDtypeStruct((m, d), F32), jax.ShapeDtypeStruct((m, d), BF16)),
        compiler_params=_cparams(("arbitrary",)),
        name="even_out_proj",
    )(y_s5, w_glu, y_rw, bonus, gate, lnx_w.reshape(1, hw), lnx_b.reshape(1, hw), w_out, x,
      g.reshape(1, d), g_next.reshape(1, d))


def _lru_kernel(gate_ref, xb_ref, cw_ref, cb_ref, wr_ref, br_ref, wi_ref, bi_ref, lam_ref,
                o_ref, halo_ref, h_ref, a_scr, b_scr, *, tc):
    t_idx = pl.program_id(1)
    width = xb_ref.shape[1]
    nblk = tc // SUBLANE
    n_gate_blocks = width // LRU_BLOCK

    @pl.when(t_idx == 0)
    def _():
        halo_ref[...] = jnp.zeros_like(halo_ref)
        h_ref[...] = jnp.zeros_like(h_ref)

    xb = xb_ref[...].astype(F32)
    ext = jnp.concatenate([halo_ref[...], xb], axis=0)
    xc = xb * cw_ref[CONV_WIDTH - 1:CONV_WIDTH, :] + cb_ref[...]
    for d in range(1, CONV_WIDTH):
        sh = pltpu.roll(ext, d, axis=0)[SUBLANE:, :]
        xc = xc + sh * cw_ref[CONV_WIDTH - 1 - d:CONV_WIDTH - d, :]
    halo_ref[...] = xb[tc - SUBLANE:, :]

    sp = _softplus(-lam_ref[...])

    for n in range(n_gate_blocks):
        lanes = slice(n * LRU_BLOCK, (n + 1) * LRU_BLOCK)
        xn = xc[:, lanes]
        xnb = xn.astype(BF16)
        gr = jnp.dot(xnb, wr_ref[n].astype(BF16), preferred_element_type=F32) + br_ref[:, lanes]
        gi = jnp.dot(xnb, wi_ref[n].astype(BF16), preferred_element_type=F32) + bi_ref[:, lanes]
        log_a = -LRU_C * _sigmoid(gr) * sp[:, lanes]
        a = jnp.exp(log_a)
        mult = jnp.sqrt(-jnp.tanh(log_a) * (a * a + 1.0))
        a_scr[:, lanes] = a
        b_scr[:, lanes] = mult * _sigmoid(gi) * xn

    grp = LRU_SCAN_LANES
    row = lax.broadcasted_iota(jnp.int32, (SUBLANE, grp), 0)

    def blk(bidx, carry):
        rows = pl.ds(pl.multiple_of(bidx * SUBLANE, SUBLANE), SUBLANE)
        for g in range(width // grp):
            lanes = slice(g * grp, (g + 1) * grp)
            av = a_scr[rows, lanes]
            bv = b_scr[rows, lanes]
            for d in (1, 2, 4):
                keep = row >= d
                ash = jnp.where(keep, pltpu.roll(av, d, axis=0), 1.0)
                bsh = jnp.where(keep, pltpu.roll(bv, d, axis=0), 0.0)
                bv = bv + av * bsh
                av = av * ash
            hv = bv + av * h_ref[:, lanes]
            b_scr[rows, lanes] = hv
            h_ref[:, lanes] = jnp.broadcast_to(hv[SUBLANE - 1:SUBLANE, :], (SUBLANE, grp))
        return carry

    lax.fori_loop(0, nblk, blk, 0)
    o_ref[...] = (b_scr[...] * gate_ref[...].astype(F32)).astype(o_ref.dtype)


def lru_block(p_odd, conv_w, conv_b, w_r, b_r, w_i, b_i, lam, layer, *, batch, tc):
    m = p_odd.shape[0]
    width = lam.shape[0]
    t_len = m // batch
    nt = t_len // tc
    assert t_len % tc == 0 and p_odd.shape[1] == 2 * width
    row = pl.BlockSpec((1, width), lambda b, t: (0, 0))
    blkw = pl.BlockSpec((None,) + w_r.shape[1:], lambda b, t: (layer, 0, 0, 0))
    return pl.pallas_call(
        functools.partial(_lru_kernel, tc=tc),
        grid=(batch, nt),
        in_specs=[
            pl.BlockSpec((tc, width), lambda b, t: (b * nt + t, 0)),
            pl.BlockSpec((tc, width), lambda b, t: (b * nt + t, 1)),
            pl.BlockSpec((CONV_WIDTH, width), lambda b, t: (0, 0)),
            row, blkw, row, blkw, row, row,
        ],
        out_specs=pl.BlockSpec((tc, width), lambda b, t: (b * nt + t, 0)),
        out_shape=jax.ShapeDtypeStruct((m, width), BF16),
        scratch_shapes=[
            pltpu.VMEM((SUBLANE, width), F32), pltpu.VMEM((SUBLANE, width), F32),
            pltpu.VMEM((tc, width), F32), pltpu.VMEM((tc, width), F32),
        ],
        compiler_params=_cparams(("arbitrary", "arbitrary")),
        name="lru_block",
    )(p_odd, p_odd, conv_w, conv_b.reshape(1, width), w_r, b_r.reshape(1, width),
      w_i, b_i.reshape(1, width), lam.reshape(1, width))


def _s5_block_weights(bb_re, bb_im, c_re, c_im):
    c, g, p = bb_re.shape
    gpt = LANE // c
    tiles = g // gpt
    eye = jnp.eye(gpt, dtype=F32)

    def b_blk(bb):
        bb = bb.reshape(c, tiles, gpt, p)
        return jnp.einsum('cjgp,gh->jgchp', bb, eye).reshape(tiles, gpt * c, gpt * p)

    def c_blk(cm):
        cm = cm.reshape(tiles, gpt, c, p)
        return jnp.einsum('jgcp,gh->jhpgc', cm, eye).reshape(tiles, gpt * p, gpt * c)

    return b_blk(bb_re), b_blk(bb_im), c_blk(c_re), c_blk(c_im)


def _even_mixer(x, h, g_post, g_ffn, batch, idx, w_in, shift_mu, lam_re, lam_im, log_dt,
                b_re, b_im, c_re, c_im, d_skip, w_glu, w0, w2, a0, a2, g2, k_k, k_a, r_k,
                lnx_w, lnx_b, w_out):
    hw = w0.shape[0]
    s5w = d_skip.shape[0]
    n_main = s5w + 3 * hw
    p_main = in_proj(h, w_in, idx, n_main, tm=2048, tn=512, out_dtype=BF16, name="even_in_proj")
    lrw = w_in.shape[2] - n_main
    lr_pad = -(-lrw // LANE) * LANE
    w_lr = jnp.pad(w_in[idx, :, n_main:], ((0, 0), (0, lr_pad - lrw)))[None]
    lr = in_proj(h, w_lr, 0, lr_pad, tm=2048, tn=lr_pad, out_dtype=F32, name="even_lr_proj")

    a_re, a_im, bb_re, bb_im = s5_discretise(lam_re, lam_im, log_dt, b_re, b_im)
    bblk_re, bblk_im, cblk_re, cblk_im = _s5_block_weights(bb_re, bb_im, c_re, c_im)
    tiles = bblk_re.shape[0]
    wk, wb, wc, a8r, a8i = s5_tables(bblk_re, bblk_im, cblk_re, cblk_im,
                                     a_re.reshape(tiles, 1, -1), a_im.reshape(tiles, 1, -1),
                                     d_skip.reshape(tiles, 1, LANE))
    y_s5 = s5_mix(p_main, wk, wb, wc, a8r, a8i, batch=batch, tc=2048)

    r, lw, k, v, a, b, bonus, g = rwkv_prep(p_main, lr, shift_mu, w0, w2, a0, a2, g2,
                                            k_k, k_a, r_k.reshape(-1), batch=batch, tm=256)
    y_rw = rwkv_scan(r, lw, k, v, a, b, batch=batch, tb=512)

    return even_out_proj(y_s5, w_glu, y_rw, bonus, g, lnx_w, lnx_b, w_out, idx, x, g_post, g_ffn,
                         tm=256)


def _odd_mixer(x, h, g_post, g_ffn, batch, idx, w_in, conv_w, conv_b, w_r, b_r, w_i, b_i,
               lam, w_out):
    width = lam.shape[0]
    p_odd = in_proj(h, w_in, idx, 2 * width, tm=2048, tn=512, out_dtype=BF16,
                    n_gelu_tiles=width // 512, name="odd_in_proj")
    hg = lru_block(p_odd, conv_w, conv_b, w_r, b_r, w_i, b_i, lam, idx, batch=batch, tc=256)
    return out_proj_norm_residual([hg], w_out, idx, x, g_post, g_ffn, tm=512,
                                  name="odd_out_proj")


def _ffn(x, h, g_post, g_next, layer, w_gate, w_up, w_down):
    act, w_down_bf16 = swiglu_up(h, w_gate, w_up, w_down, layer, tm=2048, tn=512)
    return matmul_norm_residual(act, w_down_bf16[None], 0, x, g_post, g_next, tm=1024, tk=512,
                                n_chunk=512, name="ffn_down")


def kernel(x, ev_w_in, ev_shift_mu, s5_lam_re, s5_lam_im, s5_log_dt, s5_b_re, s5_b_im, s5_c_re, s5_c_im, s5_d, s5_w_glu, rw_w0, rw_w2, rw_a0, rw_a2, rw_g2, rw_k_k, rw_k_a, rw_r_k, rw_lnx_w, rw_lnx_b, ev_w_out, od_w_in, od_conv_w, od_conv_b, lru_w_r, lru_b_r, lru_w_i, lru_b_i, lru_lam, od_w_out, ffn_w_gate, ffn_w_up, ffn_w_down, norm_mix_pre, norm_mix_post, norm_ffn_pre, norm_ffn_post):
    batch, t_len, d = x.shape
    depth = ffn_w_gate.shape[0]
    xf = x.reshape(batch * t_len, d)
    s5_w_glu, ev_w_out, od_w_out = (w.astype(BF16) for w in (s5_w_glu, ev_w_out, od_w_out))
    hm = rmsnorm(xf, norm_mix_pre[0], tm=512)
    for layer in range(depth):
        i = layer // 2
        if layer % 2 == 0:
            xf, hf = _even_mixer(xf, hm, norm_mix_post[layer],
                                 norm_ffn_pre[layer], batch, i,
                                 ev_w_in, ev_shift_mu[i], s5_lam_re[i], s5_lam_im[i],
                                 s5_log_dt[i], s5_b_re[i], s5_b_im[i], s5_c_re[i], s5_c_im[i],
                                 s5_d[i], s5_w_glu, rw_w0[i], rw_w2[i], rw_a0[i], rw_a2[i],
                                 rw_g2[i], rw_k_k[i], rw_k_a[i], rw_r_k[i], rw_lnx_w[i],
                                 rw_lnx_b[i], ev_w_out)
        else:
            xf, hf = _odd_mixer(xf, hm, norm_mix_post[layer],
                                norm_ffn_pre[layer], batch, i,
                                od_w_in, od_conv_w[i], od_conv_b[i], lru_w_r, lru_b_r[i],
                                lru_w_i, lru_b_i[i], lru_lam[i], od_w_out)
        if layer + 1 < depth:
            xf, hm = _ffn(xf, hf, norm_ffn_post[layer], norm_mix_pre[layer + 1], layer,
                          ffn_w_gate, ffn_w_up, ffn_w_down)
        else:
            xf = _ffn(xf, hf, norm_ffn_post[layer], None, layer,
                      ffn_w_gate, ffn_w_up, ffn_w_down)
    return xf.reshape(batch, t_len, d)
```
